```python
import math
import jax, jax.numpy as jnp
from jax import lax
import numpy as np

D_MODEL = 1024
BATCH = 4
SEQ = 8192
DEPTH = 2

GRID_W = 64
CTX_LEN = 256
MLA_HEADS = 8
QK_NOPE = 64
QK_ROPE = 32
V_DIM = 64
Q_LORA = 256
KV_LORA = 128
ROPE_AXIS = QK_ROPE // 2
ROPE_BASE = 10000.0
Q_BLOCK = 128
ATTN_SCALE = (QK_NOPE + QK_ROPE) ** -0.5
POOL_WINDOWS = (2, 4, 8, 16)
POOL_GROUP = 64
POOL_WIDTH = POOL_GROUP * len(POOL_WINDOWS)
CONV_WIDTH = 256
CONV_K = 31
MLA_WIDTH = MLA_HEADS * V_DIM
MIX_WIDTH = MLA_WIDTH + POOL_WIDTH + CONV_WIDTH
OFF_KV = Q_LORA
OFF_KR = OFF_KV + KV_LORA
OFF_POOL = OFF_KR + QK_ROPE
OFF_CONV = OFF_POOL + POOL_WIDTH
IN_COLS = OFF_CONV + 2 * CONV_WIDTH
N_EXPERTS = 32
TOP_K = 4
N_GROUPS = 4
TOPK_GROUPS = 2
D_EXPERT = 256
D_SHARED = 256
ROUTED_SCALE = 2.5
ALPHA = (2 * DEPTH) ** 0.25
BETA = (8 * DEPTH) ** -0.25
EPS = 1e-6

kernel_name = 'hybrid_mla_pool_conv_moe_dit'


def _layernorm(x, g=None, b=None):
    x32 = x.astype(jnp.float32)
    mu = jnp.mean(x32, axis=-1, keepdims=True)
    var = jnp.mean(jnp.square(x32 - mu), axis=-1, keepdims=True)
    y = (x32 - mu) * lax.rsqrt(var + EPS)
    if g is not None:
        y = y * g + b
    return y.astype(x.dtype)


def _rmsnorm(x, g):
    x32 = x.astype(jnp.float32)
    y = x32 * lax.rsqrt(jnp.mean(jnp.square(x32), axis=-1, keepdims=True) + EPS)
    return (y * g).astype(x.dtype)


def _modulate(z, shift, scale):
    return _layernorm(z) * (1.0 + scale) + shift


def _post_norm(z, y, gate, g, b):
    return _layernorm(ALPHA * z + gate * y, g, b)


def _axial_tables(row, col):
    inv = ROPE_BASE ** (-jnp.arange(0, ROPE_AXIS, 2, dtype=jnp.float32) / ROPE_AXIS)
    ang_r = row.astype(jnp.float32)[:, None] * inv
    ang_c = col.astype(jnp.float32)[:, None] * inv
    return (jnp.cos(ang_r), jnp.sin(ang_r), jnp.cos(ang_c), jnp.sin(ang_c))


def _rotate(x, cos, sin):
    x1, x2 = jnp.split(x, 2, axis=-1)
    return jnp.concatenate([x1 * cos - x2 * sin, x2 * cos + x1 * sin], axis=-1)


def _rope_2d(x, tabs):
    cr, sr, cc, sc = (t.astype(x.dtype) for t in tabs)
    return jnp.concatenate([_rotate(x[..., :ROPE_AXIS], cr, sr),
                            _rotate(x[..., ROPE_AXIS:], cc, sc)], axis=-1)


def _mla_q(q_c, q_norm_g, w_uq):
    b, n, _ = q_c.shape
    q = (_rmsnorm(q_c, q_norm_g) @ w_uq).reshape(b, n, MLA_HEADS, QK_NOPE + QK_ROPE)
    return q[..., :QK_NOPE], q[..., QK_NOPE:]


def _mla_kv(kv_c, k_rope, kv_norm_g, w_ukv):
    b, n, _ = kv_c.shape
    kv = (_rmsnorm(kv_c, kv_norm_g) @ w_ukv).reshape(b, n, MLA_HEADS, QK_NOPE + V_DIM)
    return kv[..., :QK_NOPE], kv[..., QK_NOPE:], k_rope


def _latent_attention(qn, qr_rot, qr_raw, kn_all, v_all, kr_ctx, kr_lat):
    b, n, h, _ = qn.shape
    nb = n // Q_BLOCK

    def blocks(t):
        return jnp.swapaxes(t.reshape(b, nb, Q_BLOCK, h, t.shape[-1]), 0, 1)

    def one_block(args):
        qn_b, qrr_b, qr0_b = args
        s_nope = jnp.einsum('bqhd,bkhd->bhqk', qn_b, kn_all)
        s_rope = jnp.concatenate([
            jnp.einsum('bqhr,bkr->bhqk', qr0_b, kr_ctx),
            jnp.einsum('bqhr,bkr->bhqk', qrr_b, kr_lat)],
            axis=-1)
        p = jax.nn.softmax((s_nope + s_rope).astype(jnp.float32) * ATTN_SCALE, axis=-1)
        return jnp.einsum('bhqk,bkhd->bqhd', p.astype(v_all.dtype), v_all)

    o = lax.map(one_block, (blocks(qn), blocks(qr_rot), blocks(qr_raw)))
    return jnp.swapaxes(o, 0, 1).reshape(b, n, h * V_DIM)


def _context_attention(qn, qr, kn, kr, v):
    b, n = qn.shape[:2]
    s = jnp.einsum('bqhd,bkhd->bhqk', qn, kn) + jnp.einsum('bqhr,bkr->bhqk', qr, kr)
    p = jax.nn.softmax(s.astype(jnp.float32) * ATTN_SCALE, axis=-1)
    return jnp.einsum('bhqk,bkhd->bqhd', p.astype(v.dtype), v).reshape(b, n, MLA_WIDTH)


def _multiscale_pool(u, pool_w, pool_scale):
    b, n, ch = u.shape
    u32 = u.astype(jnp.float32)
    cs = jnp.concatenate([jnp.zeros((b, 1, ch), jnp.float32), jnp.cumsum(u32, axis=1)], axis=1)
    t = jnp.arange(n)
    outs = []
    for gi, w in enumerate(POOL_WINDOWS):
        lo = w // 2
        hi = w - 1 - lo
        start = jnp.clip(t - lo, 0, n)
        end = jnp.clip(t + hi + 1, 0, n)
        seg = cs[..., gi * POOL_GROUP:(gi + 1) * POOL_GROUP]
        mean = (jnp.take(seg, end, axis=1) - jnp.take(seg, start, axis=1)) / \
            (end - start).astype(jnp.float32)[None, :, None]
        diff = (mean - u32[..., gi * POOL_GROUP:(gi + 1) * POOL_GROUP]).astype(u.dtype)
        outs.append(diff @ pool_w[gi])
    return jnp.concatenate(outs, axis=-1) * pool_scale


def _conformer_conv(u2, conv_dw, conv_b, conv_ln_g, conv_ln_b, conv_pw):
    a, g = jnp.split(u2, 2, axis=-1)
    u = a * jax.nn.sigmoid(g)
    y = lax.conv_general_dilated(
        u, conv_dw[:, None, :], window_strides=(1,),
        padding=[(CONV_K // 2, CONV_K // 2)],
        dimension_numbers=('NWC', 'WIO', 'NWC'),
        feature_group_count=CONV_WIDTH) + conv_b
    y = jax.nn.silu(_layernorm(y, conv_ln_g, conv_ln_b))
    return y @ conv_pw


def _mix_out(att, p, pool_w, pool_scale, conv_dw, conv_b, conv_ln_g, conv_ln_b, conv_pw, w_out):
    pool = _multiscale_pool(p[..., OFF_POOL:OFF_CONV], pool_w, pool_scale)
    conv = _conformer_conv(p[..., OFF_CONV:], conv_dw, conv_b, conv_ln_g, conv_ln_b, conv_pw)
    return jnp.concatenate([att, pool, conv], axis=-1) @ w_out


def _swiglu(t, wg, wu, wd):
    return (jax.nn.silu(t @ wg) * (t @ wu)) @ wd


def _moe(h, router_w, router_bias, exp_wg, exp_wu, exp_wd, sh_wg, sh_wu, sh_wd):
    shape = h.shape
    t = h.reshape(-1, shape[-1])
    scores = jax.nn.sigmoid((t @ router_w).astype(jnp.float32))
    sel = scores + router_bias
    grp = sel.reshape(-1, N_GROUPS, N_EXPERTS // N_GROUPS)
    gscore = jnp.sum(lax.top_k(grp, 2)[0], axis=-1)
    _, gidx = lax.top_k(gscore, TOPK_GROUPS)
    gmask = jnp.sum(jax.nn.one_hot(gidx, N_GROUPS), axis=1) > 0
    emask = jnp.repeat(gmask, N_EXPERTS // N_GROUPS, axis=-1)
    _, eidx = lax.top_k(jnp.where(emask, sel, -jnp.inf), TOP_K)
    w = jnp.take_along_axis(scores, eidx, axis=-1)
    w = w / jnp.sum(w, axis=-1, keepdims=True) * ROUTED_SCALE
    gates = jnp.einsum('tk,tke->te', w, jax.nn.one_hot(eidx, N_EXPERTS)).astype(t.dtype)

    def add_expert(acc, xs):
        g_e, wg_e, wu_e, wd_e = xs
        return acc + g_e[:, None] * _swiglu(t, wg_e, wu_e, wd_e), None

    out, _ = lax.scan(add_expert, _swiglu(t, sh_wg, sh_wu, sh_wd),
                      (gates.T, exp_wg, exp_wu, exp_wd))
    return out.reshape(shape)


def setup_inputs(seed: int = 0) -> dict:
    key = jax.random.key(seed)
    ks = iter(jax.random.split(key, 40))
    L = DEPTH

    def nrm(shape, scale=1.0):
        return jax.random.normal(next(ks), shape, jnp.float32) * scale

    def gain(shape):
        return 1.0 + nrm(shape, 0.02)

    return {
        'x': nrm((BATCH, SEQ, D_MODEL)),
        'c': nrm((BATCH, D_MODEL)),
        'ctx': nrm((BATCH, CTX_LEN, D_MODEL)),
        'c_ctx': nrm((D_MODEL,)),
        'ada_w': nrm((L, D_MODEL, 6 * D_MODEL), 0.5 * D_MODEL ** -0.5),
        'ada_b': nrm((L, 6 * D_MODEL), 0.02),
        'w_in': nrm((L, D_MODEL, IN_COLS), D_MODEL ** -0.5),
        'q_norm_g': gain((L, Q_LORA)),
        'w_uq': nrm((L, Q_LORA, MLA_HEADS * (QK_NOPE + QK_ROPE)), Q_LORA ** -0.5),
        'kv_norm_g': gain((L, KV_LORA)),
        'w_ukv': nrm((L, KV_LORA, MLA_HEADS * (QK_NOPE + V_DIM)), KV_LORA ** -0.5),
        'pool_w': nrm((L, len(POOL_WINDOWS), POOL_GROUP, POOL_GROUP), POOL_GROUP ** -0.5),
        'pool_scale': gain((L, POOL_WIDTH)),
        'conv_dw': nrm((L, CONV_K, CONV_WIDTH), CONV_K ** -0.5),
        'conv_b': nrm((L, CONV_WIDTH), 0.02),
        'conv_ln_g': gain((L, CONV_WIDTH)),
        'conv_ln_b': nrm((L, CONV_WIDTH), 0.02),
        'conv_pw': nrm((L, CONV_WIDTH, CONV_WIDTH), CONV_WIDTH ** -0.5),
        'w_out': nrm((L, MIX_WIDTH, D_MODEL), BETA * MIX_WIDTH ** -0.5),
        'ln1_g': gain((L, D_MODEL)),
        'ln1_b': nrm((L, D_MODEL), 0.02),
        'router_w': nrm((L, D_MODEL, N_EXPERTS), D_MODEL ** -0.5),
        'router_bias': nrm((L, N_EXPERTS), 0.01),
        'exp_wg': nrm((L, N_EXPERTS, D_MODEL, D_EXPERT), D_MODEL ** -0.5),
        'exp_wu': nrm((L, N_EXPERTS, D_MODEL, D_EXPERT), D_MODEL ** -0.5),
        'exp_wd': nrm((L, N_EXPERTS, D_EXPERT, D_MODEL), BETA * D_EXPERT ** -0.5),
        'sh_wg': nrm((L, D_MODEL, D_SHARED), D_MODEL ** -0.5),
        'sh_wu': nrm((L, D_MODEL, D_SHARED), D_MODEL ** -0.5),
        'sh_wd': nrm((L, D_SHARED, D_MODEL), BETA * D_SHARED ** -0.5),
        'ln2_g': gain((L, D_MODEL)),
        'ln2_b': nrm((L, D_MODEL), 0.02),
    }


def reference(x, c, ctx, c_ctx, ada_w, ada_b, w_in, q_norm_g, w_uq, kv_norm_g, w_ukv,
              pool_w, pool_scale, conv_dw, conv_b, conv_ln_g, conv_ln_b, conv_pw, w_out,
              ln1_g, ln1_b, router_w, router_bias, exp_wg, exp_wu, exp_wd,
              sh_wg, sh_wu, sh_wd, ln2_g, ln2_b):
    n_lat = x.shape[1]
    n_rows = n_lat // GRID_W
    row = jnp.broadcast_to(jnp.arange(n_rows)[:, None], (n_rows, GRID_W)).reshape(-1)
    col = jnp.broadcast_to(jnp.arange(GRID_W)[None, :], (n_rows, GRID_W)).reshape(-1)
    tabs_k = _axial_tables(row, col)
    tabs_q = tuple(tb[:, None, :] for tb in tabs_k)

    zl, zc = x, ctx
    for l in range(DEPTH):
        last = l == DEPTH - 1
        mod_l = jnp.split((jax.nn.silu(c) @ ada_w[l] + ada_b[l])[:, None, :], 6, axis=-1)
        mod_c = jnp.split(jax.nn.silu(c_ctx) @ ada_w[l] + ada_b[l], 6, axis=-1)
        mix_p = (pool_w[l], pool_scale[l], conv_dw[l], conv_b[l], conv_ln_g[l],
                 conv_ln_b[l], conv_pw[l], w_out[l])
        moe_p = (router_w[l], router_bias[l], exp_wg[l], exp_wu[l], exp_wd[l],
                 sh_wg[l], sh_wu[l], sh_wd[l])

        pl = _modulate(zl, mod_l[0], mod_l[1]) @ w_in[l]
        hc = _modulate(zc, mod_c[0], mod_c[1])
        if last:
            pc_kv = hc @ w_in[l][:, OFF_KV:OFF_POOL]
        else:
            pc = hc @ w_in[l]
            pc_kv = pc[..., OFF_KV:OFF_POOL]
        kn_c, v_c, kr_c = _mla_kv(pc_kv[..., :KV_LORA], pc_kv[..., KV_LORA:],
                                  kv_norm_g[l], w_ukv[l])
        kn_l, v_l, kr_l = _mla_kv(pl[..., OFF_KV:OFF_KR], pl[..., OFF_KR:OFF_POOL],
                                  kv_norm_g[l], w_ukv[l])
        qn_l, qr_l = _mla_q(pl[..., :OFF_KV], q_norm_g[l], w_uq[l])
        att_l = _latent_attention(qn_l, _rope_2d(qr_l, tabs_q), qr_l,
                                  jnp.concatenate([kn_c, kn_l], axis=1),
                                  jnp.concatenate([v_c, v_l], axis=1),
                                  kr_c, _rope_2d(kr_l, tabs_k))
        zl = _post_norm(zl, _mix_out(att_l, pl, *mix_p), mod_l[2], ln1_g[l], ln1_b[l])
        zl = _post_norm(zl, _moe(_modulate(zl, mod_l[3], mod_l[4]), *moe_p),
                        mod_l[5], ln2_g[l], ln2_b[l])

        if not last:
            qn_c, qr_c = _mla_q(pc[..., :OFF_KV], q_norm_g[l], w_uq[l])
            att_c = _context_attention(qn_c, qr_c, kn_c, kr_c, v_c)
            zc = _post_norm(zc, _mix_out(att_c, pc, *mix_p), mod_c[2], ln1_g[l], ln1_b[l])
            zc = _post_norm(zc, _moe(_modulate(zc, mod_c[3], mod_c[4]), *moe_p),
                            mod_c[5], ln2_g[l], ln2_b[l])
    return zl
```

```python
import functools
import math

import jax
import jax.numpy as jnp
from jax import lax
from jax.experimental import pallas as pl
from jax.experimental.pallas import tpu as pltpu

GRID_W = 64
MLA_HEADS = 8
QK_NOPE = 64
QK_ROPE = 32
V_DIM = 64
Q_LORA = 256
KV_LORA = 128
ROPE_AXIS = QK_ROPE // 2
ROPE_BASE = 10000.0
ATTN_SCALE = (QK_NOPE + QK_ROPE) ** -0.5
POOL_WINDOWS = (2, 4, 8, 16)
POOL_GROUP = 64
POOL_WIDTH = POOL_GROUP * len(POOL_WINDOWS)
CONV_WIDTH = 256
CONV_K = 31
MLA_WIDTH = MLA_HEADS * V_DIM
OFF_KV = Q_LORA
OFF_KR = OFF_KV + KV_LORA
OFF_POOL = OFF_KR + QK_ROPE
OFF_CONV = OFF_POOL + POOL_WIDTH
N_EXPERTS = 32
TOP_K = 4
N_GROUPS = 4
TOPK_GROUPS = 2
GROUP_SIZE = N_EXPERTS // N_GROUPS
ROUTED_SCALE = 2.5
EPS = 1e-6

LANES = 128
HEAD_PAD = LANES
QK_WIDTH = MLA_HEADS * HEAD_PAD
HALO = 16
P_COLS = 1280
PC_Q, PC_KV, PC_KR, PC_POOL, PC_CA, PC_CG = 0, 256, 384, 512, 768, 1024
VMEM_LIMIT = 48 * 1024 * 1024
LOG2E = math.log2(math.e)

F32 = jnp.float32
BF16 = jnp.bfloat16


def _cparams(sem):
    return pltpu.CompilerParams(dimension_semantics=sem, vmem_limit_bytes=VMEM_LIMIT)


def _ln(x):
    mu = jnp.mean(x, axis=-1, keepdims=True)
    xc = x - mu
    var = jnp.mean(xc * xc, axis=-1, keepdims=True)
    return xc * lax.rsqrt(var + EPS)


def _rms(x):
    return x * lax.rsqrt(jnp.mean(x * x, axis=-1, keepdims=True) + EPS)


def _sigmoid(x):
    return 1.0 / (1.0 + jnp.exp(-x))


def _dot(a, b):
    return jnp.dot(a, b, preferred_element_type=F32)


def _dot_nt(a, b):
    return lax.dot_general(a, b, (((1,), (1,)), ((), ())), preferred_element_type=F32)


def _ada_kernel(c_ref, w_ref, b_ref, o_ref):
    x = c_ref[...]
    x = x * _sigmoid(x)
    o_ref[0] = _dot(x.astype(BF16), w_ref[0].astype(BF16)) + b_ref[0]


def _ada(cond, ada_w, ada_b):
    depth, d, n = ada_w.shape
    tn = 1536
    return pl.pallas_call(
        _ada_kernel,
        grid=(depth, n // tn),
        in_specs=[
            pl.BlockSpec((8, d), lambda l, j: (0, 0)),
            pl.BlockSpec((1, d, tn), lambda l, j: (l, 0, j)),
            pl.BlockSpec((1, 1, tn), lambda l, j: (l, 0, j)),
        ],
        out_specs=pl.BlockSpec((1, 8, tn), lambda l, j: (l, 0, j)),
        out_shape=jax.ShapeDtypeStruct((depth, 8, n), F32),
        compiler_params=_cparams(("parallel", "parallel")),
        name="ada",
    )(cond, ada_w, ada_b.reshape(depth, 1, n))


def _proj_kernel(z_ref, mod_ref, win_ref, qg_ref, kvg_ref, wqa_ref, wqb_ref, wka_ref, wkb_ref,
                 wv_ref, vb_ref, cq_ref, sq_ref, ck_ref, sk_ref,
                 q_ref, k_ref, v_ref, up_ref, uc_ref):
    z = z_ref[0]
    shift = mod_ref[0, 0:1, :]
    scale = mod_ref[0, 1:2, :]
    h = _ln(z) * (1.0 + scale) + shift
    p = _dot(h.astype(BF16), win_ref[...])
    qn = (_rms(p[:, PC_Q:PC_KV]) * qg_ref[...]).astype(BF16)
    kvn = (_rms(p[:, PC_KV:PC_KR]) * kvg_ref[...]).astype(BF16)
    xk = jnp.concatenate([kvn, p[:, PC_KR:PC_POOL].astype(BF16)], axis=1)
    qa = _dot(qn, wqa_ref[...])
    qb = _dot(qn, wqb_ref[...])
    ka = _dot(xk, wka_ref[...])
    kb = _dot(xk, wkb_ref[...])
    cq, sq, ck, sk = cq_ref[...], sq_ref[...], ck_ref[...], sk_ref[...]
    for hd in range(MLA_HEADS):
        sl = slice(hd * HEAD_PAD, (hd + 1) * HEAD_PAD)
        q_ref[0, :, sl] = (qa[:, sl] * cq + qb[:, sl] * sq).astype(BF16)
        k_ref[0, :, sl] = (ka[:, sl] * ck + kb[:, sl] * sk).astype(BF16)
    v_ref[0] = (_dot(kvn, wv_ref[...]) + vb_ref[...]).astype(BF16)
    up_ref[0] = p[:, PC_POOL:PC_CA]
    uc_ref[0] = p[:, PC_CA:PC_CG] * _sigmoid(p[:, PC_CG:P_COLS])


def _proj(z, mod, w, tabs, tm):
    b, s, d = z.shape
    full = lambda shape: pl.BlockSpec(shape, lambda i, j: (0,) * len(shape))
    tok = lambda width: pl.BlockSpec((1, tm, width), lambda i, j: (i, j, 0))
    tab = pl.BlockSpec((tm, HEAD_PAD), lambda i, j: (j, 0))
    return pl.pallas_call(
        _proj_kernel,
        grid=(b, s // tm),
        in_specs=[
            tok(d),
            pl.BlockSpec((1, 6, d), lambda i, j: (i, 0, 0)),
            full((d, P_COLS)), full((1, Q_LORA)), full((1, KV_LORA)),
            full((Q_LORA, QK_WIDTH)), full((Q_LORA, QK_WIDTH)),
            full((2 * KV_LORA, QK_WIDTH)), full((2 * KV_LORA, QK_WIDTH)),
            full((KV_LORA, QK_WIDTH)), full((1, QK_WIDTH)),
            tab, tab, tab, tab,
        ],
        out_specs=[tok(QK_WIDTH), tok(QK_WIDTH), tok(QK_WIDTH), tok(POOL_WIDTH), tok(CONV_WIDTH)],
        out_shape=[
            jax.ShapeDtypeStruct((b, s, QK_WIDTH), BF16),
            jax.ShapeDtypeStruct((b, s, QK_WIDTH), BF16),
            jax.ShapeDtypeStruct((b, s, QK_WIDTH), BF16),
            jax.ShapeDtypeStruct((b, s, POOL_WIDTH), F32),
            jax.ShapeDtypeStruct((b, s, CONV_WIDTH), F32),
        ],
        compiler_params=_cparams(("parallel", "parallel")),
        name="proj",
    )(z, mod, w["w_in"], w["q_g"], w["kv_g"], w["wqa"], w["wqb"], w["wka"], w["wkb"],
      w["wv"], w["vb"], *tabs)


HEADS_PER_STEP = 2


def _attn_body(q_ref, kc_ref, vc_ref, kl_ref, vl_ref, o_ref, *, bk):
    n_lat = 0 if kl_ref is None else kl_ref.shape[1] // bk
    accs = []
    for hh in range(HEADS_PER_STEP):
        sl = slice(hh * HEAD_PAD, (hh + 1) * HEAD_PAD)
        q = q_ref[0, :, sl]
        s = _dot_nt(q, kc_ref[0, :, sl])
        m = jnp.max(s, axis=1, keepdims=True)
        p = jnp.exp2(s - m)
        acc = _dot(p.astype(BF16), vc_ref[0, :, sl])

        def step(c, carry, sl=sl, q=q):
            m, acc = carry
            r0 = pl.multiple_of(c * bk, bk)
            s = _dot_nt(q, kl_ref[0, pl.ds(r0, bk), sl])
            m_new = jnp.maximum(m, jnp.max(s, axis=1, keepdims=True))
            p = jnp.exp2(s - m_new)
            acc = acc * jnp.exp2(m - m_new) + _dot(p.astype(BF16), vl_ref[0, pl.ds(r0, bk), sl])
            return m_new, acc

        if n_lat:
            m, acc = lax.fori_loop(0, n_lat, step, (m, acc))
        accs.append(acc)
    lane = lax.broadcasted_iota(jnp.int32, accs[0].shape, 1)
    o_even = accs[0] / accs[0][:, V_DIM:V_DIM + 1]
    o_odd = accs[1] / accs[1][:, 0:1]
    o_ref[0] = jnp.where(lane < V_DIM, o_even, o_odd).astype(o_ref.dtype)


def _attn_kernel_full(q_ref, kc_ref, vc_ref, kl_ref, vl_ref, o_ref, *, bk):
    _attn_body(q_ref, kc_ref, vc_ref, kl_ref, vl_ref, o_ref, bk=bk)


def _attn_kernel_ctx(q_ref, kc_ref, vc_ref, o_ref, *, bk):
    _attn_body(q_ref, kc_ref, vc_ref, None, None, o_ref, bk=bk)


def _attn(q, kc, vc, kl, vl, bq, bk):
    b, s, _ = q.shape
    nc = kc.shape[1]
    w2 = HEADS_PER_STEP * HEAD_PAD
    qspec = pl.BlockSpec((1, bq, w2), lambda i, h, j: (i, j, h))
    cspec = pl.BlockSpec((1, nc, w2), lambda i, h, j: (i, 0, h))
    in_specs = [qspec, cspec, cspec]
    args = [q, kc, vc]
    if kl is None:
        body = functools.partial(_attn_kernel_ctx, bk=bk)
    else:
        lspec = pl.BlockSpec((1, kl.shape[1], w2), lambda i, h, j: (i, 0, h))
        in_specs += [lspec, lspec]
        args += [kl, vl]
        body = functools.partial(_attn_kernel_full, bk=bk)
    return pl.pallas_call(
        body,
        grid=(b, MLA_HEADS // HEADS_PER_STEP, s // bq),
        in_specs=in_specs,
        out_specs=pl.BlockSpec((1, bq, HEADS_PER_STEP * V_DIM), lambda i, h, j: (i, j, h)),
        out_shape=jax.ShapeDtypeStruct((b, s, MLA_WIDTH), BF16),
        compiler_params=_cparams(("parallel", "parallel", "arbitrary")),
        name="attn",
    )(*args)


def _fill_ext(ext_ref, main_ref, left_ref, right_ref, tm):
    j = pl.program_id(1)
    last = pl.num_programs(1) - 1
    ext_ref[0:HALO, :] = jnp.where(j > 0, left_ref[0], 0.0)
    ext_ref[HALO:HALO + tm, :] = main_ref[0]
    ext_ref[HALO + tm:HALO + tm + HALO, :] = jnp.where(j < last, right_ref[0], 0.0)


def _mix_kernel(att_ref, up_ref, upl_ref, upr_ref, uc_ref, ucl_ref, ucr_ref, z_ref, mod_ref,
                poolw_ref, pools_ref, dw_ref, cb_ref, clg_ref, clb_ref, cpw_ref, wout_ref,
                l1g_ref, l1b_ref, z1_ref, h_ref, ext_ref, *, tm, seq, alpha):
    j = pl.program_id(1)

    _fill_ext(ext_ref, up_ref, upl_ref, upr_ref, tm)
    x = up_ref[0]
    lane = lax.broadcasted_iota(jnp.int32, (tm, POOL_WIDTH), 1)
    t = lax.broadcasted_iota(jnp.int32, (tm, POOL_WIDTH), 0) + j * tm

    def shifted(d):
        return ext_ref[HALO + d:HALO + d + tm, :]

    run = x + shifted(-1)
    sums = [run]
    for lo in (2, 4, 8):
        for d in list(range(-lo, -lo // 2)) + list(range(lo // 2, lo)):
            run = run + shifted(d)
        sums.append(run)
    wsum = jnp.where(lane < POOL_GROUP, sums[0],
                     jnp.where(lane < 2 * POOL_GROUP, sums[1],
                               jnp.where(lane < 3 * POOL_GROUP, sums[2], sums[3])))
    lo = jnp.where(lane < POOL_GROUP, 1,
                   jnp.where(lane < 2 * POOL_GROUP, 2, jnp.where(lane < 3 * POOL_GROUP, 4, 8)))
    cnt = jnp.minimum(t + lo, seq) - jnp.maximum(t - lo, 0)
    diff = wsum / cnt.astype(F32) - x
    pooled = _dot(diff.astype(BF16), poolw_ref[...]) * pools_ref[...]

    _fill_ext(ext_ref, uc_ref, ucl_ref, ucr_ref, tm)
    acc = jnp.zeros((tm, CONV_WIDTH), F32) + cb_ref[...]
    for k in range(CONV_K):
        acc = acc + dw_ref[k:k + 1, :] * shifted(k - CONV_K // 2)
    y = _ln(acc) * clg_ref[...] + clb_ref[...]
    y = y * _sigmoid(y)
    conv = _dot(y.astype(BF16), cpw_ref[...])

    cat = jnp.concatenate([att_ref[0], pooled.astype(BF16), conv.astype(BF16)], axis=1)
    out = _dot(cat, wout_ref[...])
    gate = mod_ref[0, 2:3, :]
    z1 = _ln(alpha * z_ref[0] + gate * out) * l1g_ref[...] + l1b_ref[...]
    z1_ref[0] = z1
    h_ref[0] = (_ln(z1) * (1.0 + mod_ref[0, 4:5, :]) + mod_ref[0, 3:4, :]).astype(BF16)


def _mix(att, up, uc, z, mod, w, tm, alpha):
    b, s, d = z.shape
    hb = tm // HALO
    nhb = s // HALO
    full = lambda shape: pl.BlockSpec(shape, lambda i, j: (0,) * len(shape))
    tok = lambda width: pl.BlockSpec((1, tm, width), lambda i, j: (i, j, 0))
    left = lambda width: pl.BlockSpec((1, HALO, width),
                                      lambda i, j: (i, jnp.maximum(j * hb - 1, 0), 0))
    right = lambda width: pl.BlockSpec((1, HALO, width),
                                       lambda i, j: (i, jnp.minimum((j + 1) * hb, nhb - 1), 0))
    body = functools.partial(_mix_kernel, tm=tm, seq=s, alpha=alpha)
    return pl.pallas_call(
        body,
        grid=(b, s // tm),
        in_specs=[
            tok(MLA_WIDTH),
            tok(POOL_WIDTH), left(POOL_WIDTH), right(POOL_WIDTH),
            tok(CONV_WIDTH), left(CONV_WIDTH), right(CONV_WIDTH),
            tok(d),
            pl.BlockSpec((1, 6, d), lambda i, j: (i, 0, 0)),
            full((POOL_WIDTH, POOL_WIDTH)), full((1, POOL_WIDTH)),
            full((CONV_K + 1, CONV_WIDTH)), full((1, CONV_WIDTH)), full((1, CONV_WIDTH)),
            full((1, CONV_WIDTH)), full((CONV_WIDTH, CONV_WIDTH)),
            full((d, d)), full((1, d)), full((1, d)),
        ],
        out_specs=[tok(d), tok(d)],
        out_shape=[jax.ShapeDtypeStruct((b, s, d), F32), jax.ShapeDtypeStruct((b, s, d), BF16)],
        scratch_shapes=[pltpu.VMEM((tm + 2 * HALO, POOL_WIDTH), F32)],
        compiler_params=_cparams(("parallel", "parallel")),
        name="mix",
    )(att, up, up, up, uc, uc, uc, z, mod, w["pool_w"], w["pool_s"], w["conv_dw"], w["conv_b"],
      w["conv_ln_g"], w["conv_ln_b"], w["conv_pw"], w["w_out"], w["ln1_g"], w["ln1_b"])


def _route_kernel(h_ref, rw_ref, rb_ref, g_ref):
    tm = h_ref.shape[0]
    logits = _dot(h_ref[...], rw_ref[...])
    scores = _sigmoid(logits.T[0:N_EXPERTS, :])
    sel = scores + rb_ref[...]
    neg = -jnp.inf

    row8 = lax.broadcasted_iota(jnp.int32, (GROUP_SIZE, tm), 0)
    gscore = []
    for g in range(N_GROUPS):
        xg = sel[g * GROUP_SIZE:(g + 1) * GROUP_SIZE, :]
        m1 = jnp.max(xg, axis=0, keepdims=True)
        i1 = jnp.min(jnp.where(xg == m1, row8, GROUP_SIZE), axis=0, keepdims=True)
        m2 = jnp.max(jnp.where(row8 == i1, neg, xg), axis=0, keepdims=True)
        gscore.append(m1 + m2)
    masked = []
    for g in range(N_GROUPS):
        ahead = jnp.zeros((1, tm), F32)
        for g2 in range(N_GROUPS):
            if g2 < g:
                ahead = ahead + jnp.where(gscore[g2] >= gscore[g], 1.0, 0.0)
            elif g2 > g:
                ahead = ahead + jnp.where(gscore[g2] > gscore[g], 1.0, 0.0)
        keep = ahead < TOPK_GROUPS
        masked.append(jnp.where(keep, sel[g * GROUP_SIZE:(g + 1) * GROUP_SIZE, :], neg))
    masked = jnp.concatenate(masked, axis=0)

    row = lax.broadcasted_iota(jnp.int32, (N_EXPERTS, tm), 0)
    chosen = jnp.zeros((N_EXPERTS, tm), F32)
    for _ in range(TOP_K):
        m = jnp.max(masked, axis=0, keepdims=True)
        idx = jnp.min(jnp.where(masked == m, row, N_EXPERTS), axis=0, keepdims=True)
        hit = row == idx
        chosen = jnp.where(hit, 1.0, chosen)
        masked = jnp.where(hit, neg, masked)
    w = chosen * scores
    gates = w / jnp.sum(w, axis=0, keepdims=True) * ROUTED_SCALE
    pad_row = lax.broadcasted_iota(jnp.int32, (LANES - N_EXPERTS, tm), 0)
    pad = jnp.where(pad_row == 0, 1.0, 0.0)
    g_ref[...] = jnp.concatenate([gates, pad], axis=0).T


def _route(h, rw, rb, tm):
    t, d = h.shape
    return pl.pallas_call(
        _route_kernel,
        grid=(t // tm,),
        in_specs=[
            pl.BlockSpec((tm, d), lambda i: (i, 0)),
            pl.BlockSpec((d, LANES), lambda i: (0, 0)),
            pl.BlockSpec((N_EXPERTS, 1), lambda i: (0, 0)),
        ],
        out_specs=pl.BlockSpec((tm, LANES), lambda i: (i, 0)),
        out_shape=jax.ShapeDtypeStruct((t, LANES), F32),
        compiler_params=_cparams(("parallel",)),
        name="route",
    )(h, rw, rb)


def _moe_kernel(h_ref, g_ref, wg_ref, wu_ref, wd_ref, z1_ref, mod_ref, l2g_ref, l2b_ref,
                o_ref, acc_ref, *, alpha):
    e = pl.program_id(1)

    @pl.when(e == 0)
    def _():
        acc_ref[...] = jnp.zeros_like(acc_ref)

    x = h_ref[...]
    hg = _dot(x, wg_ref[0])
    hu = _dot(x, wu_ref[0])
    lane = lax.broadcasted_iota(jnp.int32, g_ref.shape, 1)
    gcol = jnp.sum(jnp.where(lane == e, g_ref[...], 0.0), axis=1, keepdims=True)
    a = hg * _sigmoid(hg) * hu * gcol
    acc_ref[...] += _dot(a.astype(BF16), wd_ref[0])

    @pl.when(e == pl.num_programs(1) - 1)
    def _():
        gate = mod_ref[0, 5:6, :]
        o_ref[...] = _ln(alpha * z1_ref[...] + gate * acc_ref[...]) * l2g_ref[...] + l2b_ref[...]


def _moe(h, gates, wg, wu, wd, z1, mod, l2g, l2b, tm, tiles_per_batch, alpha):
    t, d = h.shape
    ne, _, de = wg.shape
    body = functools.partial(_moe_kernel, alpha=alpha)
    return pl.pallas_call(
        body,
        grid=(t // tm, ne),
        in_specs=[
            pl.BlockSpec((tm, d), lambda i, e: (i, 0)),
            pl.BlockSpec((tm, LANES), lambda i, e: (i, 0)),
            pl.BlockSpec((1, d, de), lambda i, e: (e, 0, 0)),
            pl.BlockSpec((1, d, de), lambda i, e: (e, 0, 0)),
            pl.BlockSpec((1, de, d), lambda i, e: (e, 0, 0)),
            pl.BlockSpec((tm, d), lambda i, e: (i, 0)),
            pl.BlockSpec((1, 6, d), lambda i, e: (i // tiles_per_batch, 0, 0)),
            pl.BlockSpec((1, d), lambda i, e: (0, 0)),
            pl.BlockSpec((1, d), lambda i, e: (0, 0)),
        ],
        out_specs=pl.BlockSpec((tm, d), lambda i, e: (i, 0)),
        out_shape=jax.ShapeDtypeStruct((t, d), F32),
        scratch_shapes=[pltpu.VMEM((tm, d), F32)],
        compiler_params=_cparams(("parallel", "arbitrary")),
        name="moe",
    )(h, gates, wg, wu, wd, z1, mod, l2g, l2b)


def _rope_swap(r):
    r4 = r.reshape(r.shape[:-1] + (2, 2, ROPE_AXIS // 2))
    return jnp.stack([-r4[..., 1, :], r4[..., 0, :]], axis=-2).reshape(r.shape)


def _prep_layer(l, w_in, q_norm_g, w_uq, kv_norm_g, w_ukv, pool_w, pool_scale, conv_dw, conv_b,
                conv_ln_g, conv_ln_b, conv_pw, w_out, ln1_g, ln1_b, router_w, router_bias,
                exp_wg, exp_wu, exp_wd, sh_wg, sh_wu, sh_wd, ln2_g, ln2_b):
    d = w_in.shape[1]
    wi = w_in[l]
    w_in_p = jnp.concatenate([
        wi[:, 0:OFF_KR], wi[:, OFF_KR:OFF_POOL], jnp.zeros((d, PC_POOL - PC_KR - QK_ROPE), F32),
        wi[:, OFF_POOL:]], axis=1).astype(BF16)

    uq = w_uq[l].reshape(Q_LORA, MLA_HEADS, QK_NOPE + QK_ROPE)
    nope, rope = uq[..., :QK_NOPE], uq[..., QK_NOPE:]
    wqa = jnp.concatenate([nope, rope, rope], axis=-1).reshape(Q_LORA, QK_WIDTH).astype(BF16)
    wqb = jnp.concatenate([jnp.zeros_like(nope), jnp.zeros_like(rope), _rope_swap(rope)],
                          axis=-1).reshape(Q_LORA, QK_WIDTH).astype(BF16)

    ukv = w_ukv[l].reshape(KV_LORA, MLA_HEADS, QK_NOPE + V_DIM)
    kn, vv = ukv[..., :QK_NOPE], ukv[..., QK_NOPE:]
    z32 = jnp.zeros((KV_LORA, MLA_HEADS, QK_ROPE), F32)
    top = jnp.concatenate([kn, z32, z32], axis=-1)
    eye = jnp.broadcast_to(jnp.eye(QK_ROPE, dtype=F32)[:, None, :], (QK_ROPE, MLA_HEADS, QK_ROPE))
    zr = jnp.zeros((QK_ROPE, MLA_HEADS, QK_ROPE), F32)
    zn = jnp.zeros((QK_ROPE, MLA_HEADS, QK_NOPE), F32)
    padrows = jnp.zeros((KV_LORA - QK_ROPE, MLA_HEADS, HEAD_PAD), F32)

    def kmat(kr_rows):
        return jnp.concatenate([top, kr_rows, padrows], axis=0).reshape(2 * KV_LORA, QK_WIDTH).astype(BF16)

    wka_lat = kmat(jnp.concatenate([zn, zr, eye], axis=-1))
    wkb_lat = kmat(jnp.concatenate([zn, zr, _rope_swap(eye)], axis=-1))
    wka_ctx = kmat(jnp.concatenate([zn, eye, zr], axis=-1))
    wkb_ctx = jnp.zeros((2 * KV_LORA, QK_WIDTH), BF16)

    zv = jnp.zeros_like(vv)
    even = (jnp.arange(MLA_HEADS) % 2 == 0)[None, :, None]
    wv = jnp.where(even, jnp.concatenate([vv, zv], -1), jnp.concatenate([zv, vv], -1))
    wv = wv.reshape(KV_LORA, QK_WIDTH).astype(BF16)
    lane = jnp.arange(QK_WIDTH) % (2 * HEAD_PAD)
    vb = ((lane == V_DIM) | (lane == HEAD_PAD)).astype(F32)[None, :]

    pw = jnp.zeros((POOL_WIDTH, POOL_WIDTH), F32)
    for gi in range(len(POOL_WINDOWS)):
        pw = pw.at[gi * POOL_GROUP:(gi + 1) * POOL_GROUP, gi * POOL_GROUP:(gi + 1) * POOL_GROUP].set(pool_w[l, gi])

    common = dict(
        w_in=w_in_p, q_g=q_norm_g[l][None], kv_g=kv_norm_g[l][None], wqa=wqa, wqb=wqb, wv=wv, vb=vb,
        pool_w=pw.astype(BF16), pool_s=pool_scale[l][None],
        conv_dw=jnp.concatenate([conv_dw[l], jnp.zeros((1, CONV_WIDTH), F32)], axis=0),
        conv_b=conv_b[l][None], conv_ln_g=conv_ln_g[l][None], conv_ln_b=conv_ln_b[l][None],
        conv_pw=conv_pw[l].astype(BF16), w_out=w_out[l].astype(BF16),
        ln1_g=ln1_g[l][None], ln1_b=ln1_b[l][None],
        router_w=jnp.pad(router_w[l], ((0, 0), (0, LANES - N_EXPERTS))).astype(BF16),
        router_b=router_bias[l][:, None],
        wg=jnp.concatenate([exp_wg[l], sh_wg[l][None]], axis=0).astype(BF16),
        wu=jnp.concatenate([exp_wu[l], sh_wu[l][None]], axis=0).astype(BF16),
        wd=jnp.concatenate([exp_wd[l], sh_wd[l][None]], axis=0).astype(BF16),
        ln2_g=ln2_g[l][None], ln2_b=ln2_b[l][None],
    )
    return dict(common, wka=wka_lat, wkb=wkb_lat), dict(common, wka=wka_ctx, wkb=wkb_ctx)


def _rope_tables(n_lat, n_ctx):
    t = jnp.arange(n_lat)
    inv = ROPE_BASE ** (-jnp.arange(0, ROPE_AXIS, 2, dtype=F32) / ROPE_AXIS)
    ang_r = (t // GRID_W).astype(F32)[:, None] * inv
    ang_c = (t % GRID_W).astype(F32)[:, None] * inv
    cos = jnp.concatenate([jnp.cos(ang_r), jnp.cos(ang_r), jnp.cos(ang_c), jnp.cos(ang_c)], axis=1)
    sin = jnp.concatenate([jnp.sin(ang_r), jnp.sin(ang_r), jnp.sin(ang_c), jnp.sin(ang_c)], axis=1)
    ck = jnp.concatenate([jnp.ones((n_lat, HEAD_PAD - QK_ROPE), F32), cos], axis=1)
    sk = jnp.concatenate([jnp.zeros((n_lat, HEAD_PAD - QK_ROPE), F32), sin], axis=1)
    qs = ATTN_SCALE * LOG2E
    lat = (ck * qs, sk * qs, ck, sk)
    one = jnp.ones((n_ctx, HEAD_PAD), F32)
    zero = jnp.zeros((n_ctx, HEAD_PAD), F32)
    ctx = (one * qs, zero, one, zero)
    return lat, ctx


def _sublayers(z, mod, w, att, up, uc, tm, tm_moe, alpha):
    b, s, d = z.shape
    z1, h = _mix(att, up, uc, z, mod, w, tm, alpha)
    h2 = h.reshape(b * s, d)
    gates = _route(h2, w["router_w"], w["router_b"], tm)
    out = _moe(h2, gates, w["wg"], w["wu"], w["wd"], z1.reshape(b * s, d), mod,
               w["ln2_g"], w["ln2_b"], tm_moe, s // tm_moe, alpha)
    return out.reshape(b, s, d)


def kernel(x, c, ctx, c_ctx, ada_w, ada_b, w_in, q_norm_g, w_uq, kv_norm_g, w_ukv, pool_w, pool_scale, conv_dw, conv_b, conv_ln_g, conv_ln_b, conv_pw, w_out, ln1_g, ln1_b, router_w, router_bias, exp_wg, exp_wu, exp_wd, sh_wg, sh_wu, sh_wd, ln2_g, ln2_b):
    b, s, d = x.shape
    n_ctx = ctx.shape[1]
    depth = ada_w.shape[0]
    alpha = (2 * depth) ** 0.25
    assert b + 1 <= 8 and s % GRID_W == 0

    tm_l = min(512, s)
    tm_c = min(256, n_ctx)
    tm_moe_l = min(1024, s)
    tm_moe_c = n_ctx
    bq, bk = min(512, s), min(512, s)

    cond = jnp.concatenate([c, c_ctx[None], jnp.zeros((8 - b - 1, d), F32)], axis=0)
    mods = _ada(cond, ada_w, ada_b)
    tab_l, tab_c = _rope_tables(s, n_ctx)

    zl, zc = x, ctx
    for l in range(depth):
        last = l == depth - 1
        w_l, w_c = _prep_layer(l, w_in, q_norm_g, w_uq, kv_norm_g, w_ukv, pool_w, pool_scale,
                               conv_dw, conv_b, conv_ln_g, conv_ln_b, conv_pw, w_out, ln1_g, ln1_b,
                               router_w, router_bias, exp_wg, exp_wu, exp_wd, sh_wg, sh_wu, sh_wd,
                               ln2_g, ln2_b)
        mod_l = mods[l, :b].reshape(b, 6, d)
        mod_c = jnp.broadcast_to(mods[l, b].reshape(1, 6, d), (b, 6, d))

        q_l, k_l, v_l, up_l, uc_l = _proj(zl, mod_l, w_l, tab_l, tm_l)
        q_c, k_c, v_c, up_c, uc_c = _proj(zc, mod_c, w_c, tab_c, tm_c)
        att_l = _attn(q_l, k_c, v_c, k_l, v_l, bq, bk)
        zl = _sublayers(zl, mod_l, w_l, att_l, up_l, uc_l, tm_l, tm_moe_l, alpha)
        if not last:
            att_c = _attn(q_c, k_c, v_c, None, None, tm_c, tm_c)
            zc = _sublayers(zc, mod_c, w_c, att_c, up_c, uc_c, tm_c, tm_moe_c, alpha)
    return zl
```

```python
import functools
import math

import jax
import jax.numpy as jnp
from jax import lax
from jax.experimental import pallas as pl
from jax.experimental.pallas import tpu as pltpu

GRID_W = 64
MLA_HEADS = 8
QK_NOPE = 64
QK_ROPE = 32
V_DIM = 64
Q_LORA = 256
KV_LORA = 128
ROPE_AXIS = QK_ROPE // 2
ROPE_BASE = 10000.0
ATTN_SCALE = (QK_NOPE + QK_ROPE) ** -0.5
POOL_WINDOWS = (2, 4, 8, 16)
POOL_GROUP = 64
POOL_WIDTH = POOL_GROUP * len(POOL_WINDOWS)
CONV_WIDTH = 256
CONV_K = 31
MLA_WIDTH = MLA_HEADS * V_DIM
OFF_KV = Q_LORA
OFF_KR = OFF_KV + KV_LORA
OFF_POOL = OFF_KR + QK_ROPE
OFF_CONV = OFF_POOL + POOL_WIDTH
N_EXPERTS = 32
TOP_K = 4
N_GROUPS = 4
TOPK_GROUPS = 2
GROUP_SIZE = N_EXPERTS // N_GROUPS
ROUTED_SCALE = 2.5
EPS = 1e-6

LANES = 128
HEAD_PAD = LANES
QK_WIDTH = MLA_HEADS * HEAD_PAD
HALO = 16
P_COLS = 1280
PC_Q, PC_KV, PC_KR, PC_POOL, PC_CA, PC_CG = 0, 256, 384, 512, 768, 1024
VMEM_LIMIT = 48 * 1024 * 1024
LOG2E = math.log2(math.e)

F32 = jnp.float32
BF16 = jnp.bfloat16


def _cparams(sem, flags=None):
    return pltpu.CompilerParams(dimension_semantics=sem, vmem_limit_bytes=VMEM_LIMIT, flags=flags)


def _ln(x):
    mu = jnp.mean(x, axis=-1, keepdims=True)
    xc = x - mu
    var = jnp.mean(xc * xc, axis=-1, keepdims=True)
    return xc * lax.rsqrt(var + EPS)


def _rms(x):
    return x * lax.rsqrt(jnp.mean(x * x, axis=-1, keepdims=True) + EPS)


def _sigmoid(x):
    return 1.0 / (1.0 + jnp.exp(-x))


def _dot(a, b):
    return jnp.dot(a, b, preferred_element_type=F32)


def _dot_nt(a, b):
    return lax.dot_general(a, b, (((1,), (1,)), ((), ())), preferred_element_type=F32)


def _ada_kernel(c_ref, w_ref, b_ref, o_ref):
    x = c_ref[...]
    x = x * _sigmoid(x)
    o_ref[0] = _dot(x.astype(BF16), w_ref[0].astype(BF16)) + b_ref[0]


def _ada(cond, ada_w, ada_b):
    depth, d, n = ada_w.shape
    tn = 1536
    return pl.pallas_call(
        _ada_kernel,
        grid=(depth, n // tn),
        in_specs=[
            pl.BlockSpec((8, d), lambda l, j: (0, 0)),
            pl.BlockSpec((1, d, tn), lambda l, j: (l, 0, j)),
            pl.BlockSpec((1, 1, tn), lambda l, j: (l, 0, j)),
        ],
        out_specs=pl.BlockSpec((1, 8, tn), lambda l, j: (l, 0, j)),
        out_shape=jax.ShapeDtypeStruct((depth, 8, n), F32),
        compiler_params=_cparams(("parallel", "parallel")),
        name="ada",
    )(cond, ada_w, ada_b.reshape(depth, 1, n))


def _proj_kernel(z_ref, mod_ref, win_ref, qg_ref, kvg_ref, wqat_ref, wqbt_ref, wka_ref, wkb_ref,
                 wvt_ref, vbt_ref, cqt_ref, sqt_ref, ck_ref, sk_ref,
                 qt_ref, k_ref, vt_ref, up_ref, uc_ref):
    z = z_ref[0]
    shift = mod_ref[0, 0:1, :]
    scale = mod_ref[0, 1:2, :]
    h = _ln(z) * (1.0 + scale) + shift
    p = _dot(h.astype(BF16), win_ref[...])
    qn = (_rms(p[:, PC_Q:PC_KV]) * qg_ref[...]).astype(BF16)
    kvn = (_rms(p[:, PC_KV:PC_KR]) * kvg_ref[...]).astype(BF16)
    xk = jnp.concatenate([kvn, p[:, PC_KR:PC_POOL].astype(BF16)], axis=1)
    qat = _dot_nt(wqat_ref[...], qn)
    qbt = _dot_nt(wqbt_ref[...], qn)
    ka = _dot(xk, wka_ref[...])
    kb = _dot(xk, wkb_ref[...])
    cqt, sqt, ck, sk = cqt_ref[...], sqt_ref[...], ck_ref[...], sk_ref[...]
    for hd in range(MLA_HEADS):
        sl = slice(hd * HEAD_PAD, (hd + 1) * HEAD_PAD)
        qt_ref[0, sl, :] = (qat[sl, :] * cqt + qbt[sl, :] * sqt).astype(BF16)
        k_ref[0, :, sl] = (ka[:, sl] * ck + kb[:, sl] * sk).astype(BF16)
    vt_ref[0, 0] = (_dot_nt(wvt_ref[...], kvn) + vbt_ref[...]).astype(BF16)
    up_ref[0] = p[:, PC_POOL:PC_CA]
    uc_ref[0] = p[:, PC_CA:PC_CG] * _sigmoid(p[:, PC_CG:P_COLS])


def _proj(z, mod, w, tabs, tm):
    b, s, d = z.shape
    full = lambda shape: pl.BlockSpec(shape, lambda i, j: (0,) * len(shape))
    tok = lambda width: pl.BlockSpec((1, tm, width), lambda i, j: (i, j, 0))
    tab = pl.BlockSpec((tm, HEAD_PAD), lambda i, j: (j, 0))
    tab_t = pl.BlockSpec((HEAD_PAD, tm), lambda i, j: (0, j))
    return pl.pallas_call(
        _proj_kernel,
        grid=(b, s // tm),
        in_specs=[
            tok(d),
            pl.BlockSpec((1, 6, d), lambda i, j: (i, 0, 0)),
            full((d, P_COLS)), full((1, Q_LORA)), full((1, KV_LORA)),
            full((QK_WIDTH, Q_LORA)), full((QK_WIDTH, Q_LORA)),
            full((2 * KV_LORA, QK_WIDTH)), full((2 * KV_LORA, QK_WIDTH)),
            full((QK_WIDTH, KV_LORA)), full((QK_WIDTH, 1)),
            tab_t, tab_t, tab, tab,
        ],
        out_specs=[
            pl.BlockSpec((1, QK_WIDTH, tm), lambda i, j: (i, 0, j)),
            tok(QK_WIDTH),
            pl.BlockSpec((1, 1, QK_WIDTH, tm), lambda i, j: (i, j, 0, 0)),
            tok(POOL_WIDTH), tok(CONV_WIDTH)],
        out_shape=[
            jax.ShapeDtypeStruct((b, QK_WIDTH, s), BF16),
            jax.ShapeDtypeStruct((b, s, QK_WIDTH), BF16),
            jax.ShapeDtypeStruct((b, s // tm, QK_WIDTH, tm), BF16),
            jax.ShapeDtypeStruct((b, s, POOL_WIDTH), F32),
            jax.ShapeDtypeStruct((b, s, CONV_WIDTH), F32),
        ],
        compiler_params=_cparams(("parallel", "parallel")),
        name="proj",
    )(z, mod, w["w_in"], w["q_g"], w["kv_g"], w["wqat"], w["wqbt"], w["wka"], w["wkb"],
      w["wvt"], w["vbt"], *tabs)


HEADS_PER_STEP = 2


def _attn_body(qt_ref, kc_ref, vct_ref, kl_ref, vlt_ref, o_ref, s_refs, p_refs):
    heads = range(HEADS_PER_STEP)
    sls = [slice(hh * HEAD_PAD, (hh + 1) * HEAD_PAD) for hh in heads]
    qts = [qt_ref[0, sl, :] for sl in sls]
    ms, accs = [], []
    for qt, sl in zip(qts, sls):
        st = _dot(kc_ref[0, :, sl], qt)
        m = jnp.max(st, axis=0, keepdims=True)
        pt = jnp.exp2(st - m)
        ms.append(m)
        accs.append(_dot(vct_ref[0, 0, sl, :], pt.astype(BF16)))

    if kl_ref is not None:
        n, bk = vlt_ref.shape[1], vlt_ref.shape[3]
        assert n >= 2 and n % 2 == 0

        def score(c, slot):
            r0 = c * bk if isinstance(c, int) else pl.multiple_of(c * bk, bk)
            cms = []
            for hh in heads:
                st = _dot(kl_ref[0, pl.ds(r0, bk), sls[hh]], qts[hh])
                s_refs[slot][hh] = st
                cms.append(jnp.max(st, axis=0, keepdims=True))
            return cms

        def probs(slot, ms, cms):
            m_new = [jnp.maximum(ms[hh], cms[hh]) for hh in heads]
            alphas = [jnp.exp2(ms[hh] - m_new[hh]) for hh in heads]
            for hh in heads:
                p_refs[slot][hh] = jnp.exp2((s_refs[slot][hh] - m_new[hh]).astype(BF16))
            return m_new, alphas

        def accum(c, slot, accs, alphas):
            return [accs[hh] * alphas[hh] + _dot(vlt_ref[0, c, sls[hh], :], p_refs[slot][hh])
                    for hh in heads]

        def half(c, cur, nxt, ms, cms, accs, with_score=True):
            ms, alphas = probs(cur, ms, cms)
            if with_score:
                cms = score(c + 1, nxt)
            accs = accum(c, cur, accs, alphas)
            return ms, cms, accs

        cms = score(0, 0)

        def pair(t, carry):
            ms, cms, accs = [list(x) for x in carry]
            c = 2 * t
            ms, cms, accs = half(c, 0, 1, ms, cms, accs)
            ms, cms, accs = half(c + 1, 1, 0, ms, cms, accs)
            return tuple(tuple(x) for x in (ms, cms, accs))

        carry = tuple(tuple(x) for x in (ms, cms, accs))
        carry = lax.fori_loop(0, (n - 2) // 2, pair, carry)
        ms, cms, accs = [list(x) for x in carry]
        ms, cms, accs = half(n - 2, 0, 1, ms, cms, accs)
        ms, cms, accs = half(n - 1, 1, 0, ms, cms, accs, with_score=False)

    acc_even, acc_odd = accs
    row = lax.broadcasted_iota(jnp.int32, acc_even.shape, 0)
    ot = jnp.where(row < V_DIM, acc_even / acc_even[V_DIM:V_DIM + 1, :], acc_odd / acc_odd[0:1, :])
    o_ref[0] = ot.T.astype(o_ref.dtype)


def _attn_kernel_full(qt_ref, kc_ref, vct_ref, kl_ref, vlt_ref, o_ref, s0_ref, s1_ref, p0_ref, p1_ref):
    _attn_body(qt_ref, kc_ref, vct_ref, kl_ref, vlt_ref, o_ref, (s0_ref, s1_ref), (p0_ref, p1_ref))


def _attn_kernel_ctx(qt_ref, kc_ref, vct_ref, o_ref):
    _attn_body(qt_ref, kc_ref, vct_ref, None, None, o_ref, None, None)


def _attn(qt, kc, vct, kl, vlt, bq):
    b, _, s = qt.shape
    nc = kc.shape[1]
    w2 = HEADS_PER_STEP * HEAD_PAD
    in_specs = [
        pl.BlockSpec((1, w2, bq), lambda i, h, j: (i, h, j)),
        pl.BlockSpec((1, nc, w2), lambda i, h, j: (i, 0, h)),
        pl.BlockSpec((1, 1, w2, nc), lambda i, h, j: (i, 0, h, 0)),
    ]
    args = [qt, kc, vct]
    scratch = []
    if kl is None:
        body = _attn_kernel_ctx
    else:
        bk = vlt.shape[3]
        scratch = [pltpu.VMEM((HEADS_PER_STEP, bk, bq), F32)] * 2 + [pltpu.VMEM((HEADS_PER_STEP, bk, bq), BF16)] * 2
        in_specs += [
            pl.BlockSpec((1, kl.shape[1], w2), lambda i, h, j: (i, 0, h)),
            pl.BlockSpec((1, vlt.shape[1], w2, vlt.shape[3]), lambda i, h, j: (i, 0, h, 0)),
        ]
        args += [kl, vlt]
        body = _attn_kernel_full
    return pl.pallas_call(
        body,
        grid=(b, MLA_HEADS // HEADS_PER_STEP, s // bq),
        in_specs=in_specs,
        out_specs=pl.BlockSpec((1, bq, HEADS_PER_STEP * V_DIM), lambda i, h, j: (i, j, h)),
        out_shape=jax.ShapeDtypeStruct((b, s, MLA_WIDTH), BF16),
        scratch_shapes=scratch,
        compiler_params=_cparams(("parallel", "parallel", "arbitrary")),
        name="attn",
    )(*args)


def _fill_ext(ext_ref, main_ref, left_ref, right_ref, tm):
    j = pl.program_id(1)
    last = pl.num_programs(1) - 1
    ext_ref[0:HALO, :] = jnp.where(j > 0, left_ref[0], 0.0)
    ext_ref[HALO:HALO + tm, :] = main_ref[0]
    ext_ref[HALO + tm:HALO + tm + HALO, :] = jnp.where(j < last, right_ref[0], 0.0)


def _mix_kernel(att_ref, up_ref, upl_ref, upr_ref, uc_ref, ucl_ref, ucr_ref, z_ref, mod_ref,
                poolw_ref, pools_ref, dw_ref, cb_ref, clg_ref, clb_ref, cpw_ref, wout_ref,
                l1g_ref, l1b_ref, z1_ref, h_ref, ext_ref, *, tm, seq, alpha):
    j = pl.program_id(1)

    _fill_ext(ext_ref, up_ref, upl_ref, upr_ref, tm)
    x = up_ref[0]
    lane = lax.broadcasted_iota(jnp.int32, (tm, POOL_WIDTH), 1)
    t = lax.broadcasted_iota(jnp.int32, (tm, POOL_WIDTH), 0) + j * tm

    def shifted(d):
        return ext_ref[HALO + d:HALO + d + tm, :]

    run = x + shifted(-1)
    sums = [run]
    for lo in (2, 4, 8):
        for d in list(range(-lo, -lo // 2)) + list(range(lo // 2, lo)):
            run = run + shifted(d)
        sums.append(run)
    wsum = jnp.where(lane < POOL_GROUP, sums[0],
                     jnp.where(lane < 2 * POOL_GROUP, sums[1],
                               jnp.where(lane < 3 * POOL_GROUP, sums[2], sums[3])))
    lo = jnp.where(lane < POOL_GROUP, 1,
                   jnp.where(lane < 2 * POOL_GROUP, 2, jnp.where(lane < 3 * POOL_GROUP, 4, 8)))
    cnt = jnp.minimum(t + lo, seq) - jnp.maximum(t - lo, 0)
    diff = wsum / cnt.astype(F32) - x
    pooled = _dot(diff.astype(BF16), poolw_ref[...]) * pools_ref[...]

    _fill_ext(ext_ref, uc_ref, ucl_ref, ucr_ref, tm)
    acc = jnp.zeros((tm, CONV_WIDTH), F32) + cb_ref[...]
    for k in range(CONV_K):
        acc = acc + dw_ref[k:k + 1, :] * shifted(k - CONV_K // 2)
    y = _ln(acc) * clg_ref[...] + clb_ref[...]
    y = y * _sigmoid(y)
    conv = _dot(y.astype(BF16), cpw_ref[...])

    cat = jnp.concatenate([att_ref[0], pooled.astype(BF16), conv.astype(BF16)], axis=1)
    out = _dot(cat, wout_ref[...])
    gate = mod_ref[0, 2:3, :]
    z1 = _ln(alpha * z_ref[0] + gate * out) * l1g_ref[...] + l1b_ref[...]
    z1_ref[0] = z1
    h_ref[0] = (_ln(z1) * (1.0 + mod_ref[0, 4:5, :]) + mod_ref[0, 3:4, :]).astype(BF16)


def _mix(att, up, uc, z, mod, w, tm, alpha):
    b, s, d = z.shape
    hb = tm // HALO
    nhb = s // HALO
    full = lambda shape: pl.BlockSpec(shape, lambda i, j: (0,) * len(shape))
    tok = lambda width: pl.BlockSpec((1, tm, width), lambda i, j: (i, j, 0))
    left = lambda width: pl.BlockSpec((1, HALO, width),
                                      lambda i, j: (i, jnp.maximum(j * hb - 1, 0), 0))
    right = lambda width: pl.BlockSpec((1, HALO, width),
                                       lambda i, j: (i, jnp.minimum((j + 1) * hb, nhb - 1), 0))
    body = functools.partial(_mix_kernel, tm=tm, seq=s, alpha=alpha)
    return pl.pallas_call(
        body,
        grid=(b, s // tm),
        in_specs=[
            tok(MLA_WIDTH),
            tok(POOL_WIDTH), left(POOL_WIDTH), right(POOL_WIDTH),
            tok(CONV_WIDTH), left(CONV_WIDTH), right(CONV_WIDTH),
            tok(d),
            pl.BlockSpec((1, 6, d), lambda i, j: (i, 0, 0)),
            full((POOL_WIDTH, POOL_WIDTH)), full((1, POOL_WIDTH)),
            full((CONV_K + 1, CONV_WIDTH)), full((1, CONV_WIDTH)), full((1, CONV_WIDTH)),
            full((1, CONV_WIDTH)), full((CONV_WIDTH, CONV_WIDTH)),
            full((d, d)), full((1, d)), full((1, d)),
        ],
        out_specs=[tok(d), tok(d)],
        out_shape=[jax.ShapeDtypeStruct((b, s, d), F32), jax.ShapeDtypeStruct((b, s, d), BF16)],
        scratch_shapes=[pltpu.VMEM((tm + 2 * HALO, POOL_WIDTH), F32)],
        compiler_params=_cparams(("parallel", "parallel")),
        name="mix",
    )(att, up, up, up, uc, uc, uc, z, mod, w["pool_w"], w["pool_s"], w["conv_dw"], w["conv_b"],
      w["conv_ln_g"], w["conv_ln_b"], w["conv_pw"], w["w_out"], w["ln1_g"], w["ln1_b"])


def _route_kernel(h_ref, rw_ref, rb_ref, g_ref):
    tm = h_ref.shape[0]
    logits = _dot(h_ref[...], rw_ref[...])
    scores = _sigmoid(logits.T[0:N_EXPERTS, :])
    sel = scores + rb_ref[...]
    neg = -jnp.inf

    row8 = lax.broadcasted_iota(jnp.int32, (GROUP_SIZE, tm), 0)
    gscore = []
    for g in range(N_GROUPS):
        xg = sel[g * GROUP_SIZE:(g + 1) * GROUP_SIZE, :]
        m1 = jnp.max(xg, axis=0, keepdims=True)
        i1 = jnp.min(jnp.where(xg == m1, row8, GROUP_SIZE), axis=0, keepdims=True)
        m2 = jnp.max(jnp.where(row8 == i1, neg, xg), axis=0, keepdims=True)
        gscore.append(m1 + m2)
    masked = []
    for g in range(N_GROUPS):
        ahead = jnp.zeros((1, tm), F32)
        for g2 in range(N_GROUPS):
            if g2 < g:
                ahead = ahead + jnp.where(gscore[g2] >= gscore[g], 1.0, 0.0)
            elif g2 > g:
                ahead = ahead + jnp.where(gscore[g2] > gscore[g], 1.0, 0.0)
        keep = ahead < TOPK_GROUPS
        masked.append(jnp.where(keep, sel[g * GROUP_SIZE:(g + 1) * GROUP_SIZE, :], neg))
    masked = jnp.concatenate(masked, axis=0)

    row = lax.broadcasted_iota(jnp.int32, (N_EXPERTS, tm), 0)
    chosen = jnp.zeros((N_EXPERTS, tm), F32)
    for _ in range(TOP_K):
        m = jnp.max(masked, axis=0, keepdims=True)
        idx = jnp.min(jnp.where(masked == m, row, N_EXPERTS), axis=0, keepdims=True)
        hit = row == idx
        chosen = jnp.where(hit, 1.0, chosen)
        masked = jnp.where(hit, neg, masked)
    w = chosen * scores
    gates = w / jnp.sum(w, axis=0, keepdims=True) * ROUTED_SCALE
    pad_row = lax.broadcasted_iota(jnp.int32, (LANES - N_EXPERTS, tm), 0)
    pad = jnp.where(pad_row == 0, 1.0, 0.0)
    g_ref[...] = jnp.concatenate([gates, pad], axis=0).T


def _route(h, rw, rb, tm):
    t, d = h.shape
    return pl.pallas_call(
        _route_kernel,
        grid=(t // tm,),
        in_specs=[
            pl.BlockSpec((tm, d), lambda i: (i, 0)),
            pl.BlockSpec((d, LANES), lambda i: (0, 0)),
            pl.BlockSpec((N_EXPERTS, 1), lambda i: (0, 0)),
        ],
        out_specs=pl.BlockSpec((tm, LANES), lambda i: (i, 0)),
        out_shape=jax.ShapeDtypeStruct((t, LANES), F32),
        compiler_params=_cparams(("parallel",)),
        name="route",
    )(h, rw, rb)


def _moe_kernel(h_ref, g_ref, wg_ref, wu_ref, wd_ref, z1_ref, mod_ref, l2g_ref, l2b_ref,
                o_ref, acc_ref, *, alpha):
    e = pl.program_id(1)

    @pl.when(e == 0)
    def _():
        acc_ref[...] = jnp.zeros_like(acc_ref)

    x = h_ref[...]
    hg = _dot(x, wg_ref[0])
    hu = _dot(x, wu_ref[0])
    lane = lax.broadcasted_iota(jnp.int32, g_ref.shape, 1)
    gcol = jnp.sum(jnp.where(lane == e, g_ref[...], 0.0), axis=1, keepdims=True)
    a = hg * _sigmoid(hg) * hu * gcol
    acc_ref[...] += _dot(a.astype(BF16), wd_ref[0])

    @pl.when(e == pl.num_programs(1) - 1)
    def _():
        gate = mod_ref[0, 5:6, :]
        o_ref[...] = _ln(alpha * z1_ref[...] + gate * acc_ref[...]) * l2g_ref[...] + l2b_ref[...]


def _moe(h, gates, wg, wu, wd, z1, mod, l2g, l2b, tm, tiles_per_batch, alpha):
    t, d = h.shape
    ne, _, de = wg.shape
    body = functools.partial(_moe_kernel, alpha=alpha)
    return pl.pallas_call(
        body,
        grid=(t // tm, ne),
        in_specs=[
            pl.BlockSpec((tm, d), lambda i, e: (i, 0)),
            pl.BlockSpec((tm, LANES), lambda i, e: (i, 0)),
            pl.BlockSpec((1, d, de), lambda i, e: (e, 0, 0)),
            pl.BlockSpec((1, d, de), lambda i, e: (e, 0, 0)),
            pl.BlockSpec((1, de, d), lambda i, e: (e, 0, 0)),
            pl.BlockSpec((tm, d), lambda i, e: (i, 0)),
            pl.BlockSpec((1, 6, d), lambda i, e: (i // tiles_per_batch, 0, 0)),
            pl.BlockSpec((1, d), lambda i, e: (0, 0)),
            pl.BlockSpec((1, d), lambda i, e: (0, 0)),
        ],
        out_specs=pl.BlockSpec((tm, d), lambda i, e: (i, 0)),
        out_shape=jax.ShapeDtypeStruct((t, d), F32),
        scratch_shapes=[pltpu.VMEM((tm, d), F32)],
        compiler_params=_cparams(("parallel", "arbitrary")),
        name="moe",
    )(h, gates, wg, wu, wd, z1, mod, l2g, l2b)


def _rope_swap(r):
    r4 = r.reshape(r.shape[:-1] + (2, 2, ROPE_AXIS // 2))
    return jnp.stack([-r4[..., 1, :], r4[..., 0, :]], axis=-2).reshape(r.shape)


def _prep_layer(l, w_in, q_norm_g, w_uq, kv_norm_g, w_ukv, pool_w, pool_scale, conv_dw, conv_b,
                conv_ln_g, conv_ln_b, conv_pw, w_out, ln1_g, ln1_b, router_w, router_bias,
                exp_wg, exp_wu, exp_wd, sh_wg, sh_wu, sh_wd, ln2_g, ln2_b):
    d = w_in.shape[1]
    wi = w_in[l]
    w_in_p = jnp.concatenate([
        wi[:, 0:OFF_KR], wi[:, OFF_KR:OFF_POOL], jnp.zeros((d, PC_POOL - PC_KR - QK_ROPE), F32),
        wi[:, OFF_POOL:]], axis=1).astype(BF16)

    uq = w_uq[l].reshape(Q_LORA, MLA_HEADS, QK_NOPE + QK_ROPE)
    nope, rope = uq[..., :QK_NOPE], uq[..., QK_NOPE:]
    wqat = jnp.concatenate([nope, rope, rope], axis=-1).reshape(Q_LORA, QK_WIDTH).T.astype(BF16)
    wqbt = jnp.concatenate([jnp.zeros_like(nope), jnp.zeros_like(rope), _rope_swap(rope)],
                           axis=-1).reshape(Q_LORA, QK_WIDTH).T.astype(BF16)

    ukv = w_ukv[l].reshape(KV_LORA, MLA_HEADS, QK_NOPE + V_DIM)
    kn, vv = ukv[..., :QK_NOPE], ukv[..., QK_NOPE:]
    z32 = jnp.zeros((KV_LORA, MLA_HEADS, QK_ROPE), F32)
    top = jnp.concatenate([kn, z32, z32], axis=-1)
    eye = jnp.broadcast_to(jnp.eye(QK_ROPE, dtype=F32)[:, None, :], (QK_ROPE, MLA_HEADS, QK_ROPE))
    zr = jnp.zeros((QK_ROPE, MLA_HEADS, QK_ROPE), F32)
    zn = jnp.zeros((QK_ROPE, MLA_HEADS, QK_NOPE), F32)
    padrows = jnp.zeros((KV_LORA - QK_ROPE, MLA_HEADS, HEAD_PAD), F32)

    def kmat(kr_rows):
        return jnp.concatenate([top, kr_rows, padrows], axis=0).reshape(2 * KV_LORA, QK_WIDTH).astype(BF16)

    wka_lat = kmat(jnp.concatenate([zn, zr, eye], axis=-1))
    wkb_lat = kmat(jnp.concatenate([zn, zr, _rope_swap(eye)], axis=-1))
    wka_ctx = kmat(jnp.concatenate([zn, eye, zr], axis=-1))
    wkb_ctx = jnp.zeros((2 * KV_LORA, QK_WIDTH), BF16)

    zv = jnp.zeros_like(vv)
    even = (jnp.arange(MLA_HEADS) % 2 == 0)[None, :, None]
    wv = jnp.where(even, jnp.concatenate([vv, zv], -1), jnp.concatenate([zv, vv], -1))
    wvt = wv.reshape(KV_LORA, QK_WIDTH).T.astype(BF16)
    lane = jnp.arange(QK_WIDTH) % (2 * HEAD_PAD)
    vbt = ((lane == V_DIM) | (lane == HEAD_PAD)).astype(F32)[:, None]

    pw = jnp.zeros((POOL_WIDTH, POOL_WIDTH), F32)
    for gi in range(len(POOL_WINDOWS)):
        pw = pw.at[gi * POOL_GROUP:(gi + 1) * POOL_GROUP, gi * POOL_GROUP:(gi + 1) * POOL_GROUP].set(pool_w[l, gi])

    common = dict(
        w_in=w_in_p, q_g=q_norm_g[l][None], kv_g=kv_norm_g[l][None], wqat=wqat, wqbt=wqbt, wvt=wvt, vbt=vbt,
        pool_w=pw.astype(BF16), pool_s=pool_scale[l][None],
        conv_dw=jnp.concatenate([conv_dw[l], jnp.zeros((1, CONV_WIDTH), F32)], axis=0),
        conv_b=conv_b[l][None], conv_ln_g=conv_ln_g[l][None], conv_ln_b=conv_ln_b[l][None],
        conv_pw=conv_pw[l].astype(BF16), w_out=w_out[l].astype(BF16),
        ln1_g=ln1_g[l][None], ln1_b=ln1_b[l][None],
        router_w=jnp.pad(router_w[l], ((0, 0), (0, LANES - N_EXPERTS))).astype(BF16),
        router_b=router_bias[l][:, None],
        wg=jnp.concatenate([exp_wg[l], sh_wg[l][None]], axis=0).astype(BF16),
        wu=jnp.concatenate([exp_wu[l], sh_wu[l][None]], axis=0).astype(BF16),
        wd=jnp.concatenate([exp_wd[l], sh_wd[l][None]], axis=0).astype(BF16),
        ln2_g=ln2_g[l][None], ln2_b=ln2_b[l][None],
    )
    return dict(common, wka=wka_lat, wkb=wkb_lat), dict(common, wka=wka_ctx, wkb=wkb_ctx)


def _rope_tables(n_lat, n_ctx):
    t = jnp.arange(n_lat)
    inv = ROPE_BASE ** (-jnp.arange(0, ROPE_AXIS, 2, dtype=F32) / ROPE_AXIS)
    ang_r = (t // GRID_W).astype(F32)[:, None] * inv
    ang_c = (t % GRID_W).astype(F32)[:, None] * inv
    cos = jnp.concatenate([jnp.cos(ang_r), jnp.cos(ang_r), jnp.cos(ang_c), jnp.cos(ang_c)], axis=1)
    sin = jnp.concatenate([jnp.sin(ang_r), jnp.sin(ang_r), jnp.sin(ang_c), jnp.sin(ang_c)], axis=1)
    ck = jnp.concatenate([jnp.ones((n_lat, HEAD_PAD - QK_ROPE), F32), cos], axis=1)
    sk = jnp.concatenate([jnp.zeros((n_lat, HEAD_PAD - QK_ROPE), F32), sin], axis=1)
    qs = ATTN_SCALE * LOG2E
    lat = ((ck * qs).T, (sk * qs).T, ck, sk)
    one = jnp.ones((n_ctx, HEAD_PAD), F32)
    zero = jnp.zeros((n_ctx, HEAD_PAD), F32)
    ctx = ((one * qs).T, zero.T, one, zero)
    return lat, ctx


def _sublayers(z, mod, w, att, up, uc, tm, tm_moe, alpha):
    b, s, d = z.shape
    z1, h = _mix(att, up, uc, z, mod, w, tm, alpha)
    h2 = h.reshape(b * s, d)
    gates = _route(h2, w["router_w"], w["router_b"], tm)
    out = _moe(h2, gates, w["wg"], w["wu"], w["wd"], z1.reshape(b * s, d), mod,
               w["ln2_g"], w["ln2_b"], tm_moe, s // tm_moe, alpha)
    return out.reshape(b, s, d)


def kernel(x, c, ctx, c_ctx, ada_w, ada_b, w_in, q_norm_g, w_uq, kv_norm_g, w_ukv, pool_w, pool_scale, conv_dw, conv_b, conv_ln_g, conv_ln_b, conv_pw, w_out, ln1_g, ln1_b, router_w, router_bias, exp_wg, exp_wu, exp_wd, sh_wg, sh_wu, sh_wd, ln2_g, ln2_b):
    b, s, d = x.shape
    n_ctx = ctx.shape[1]
    depth = ada_w.shape[0]
    alpha = (2 * depth) ** 0.25
    assert b + 1 <= 8 and s % GRID_W == 0

    tm_l = min(512, s)
    tm_c = min(256, n_ctx)
    tm_moe_l = min(1024, s)
    tm_moe_c = n_ctx
    bq = min(512, s)

    cond = jnp.concatenate([c, c_ctx[None], jnp.zeros((8 - b - 1, d), F32)], axis=0)
    mods = _ada(cond, ada_w, ada_b)
    tab_l, tab_c = _rope_tables(s, n_ctx)

    zl, zc = x, ctx
    for l in range(depth):
        last = l == depth - 1
        w_l, w_c = _prep_layer(l, w_in, q_norm_g, w_uq, kv_norm_g, w_ukv, pool_w, pool_scale,
                               conv_dw, conv_b, conv_ln_g, conv_ln_b, conv_pw, w_out, ln1_g, ln1_b,
                               router_w, router_bias, exp_wg, exp_wu, exp_wd, sh_wg, sh_wu, sh_wd,
                               ln2_g, ln2_b)
        mod_l = mods[l, :b].reshape(b, 6, d)
        mod_c = jnp.broadcast_to(mods[l, b].reshape(1, 6, d), (b, 6, d))

        q_l, k_l, v_l, up_l, uc_l = _proj(zl, mod_l, w_l, tab_l, tm_l)
        q_c, k_c, v_c, up_c, uc_c = _proj(zc, mod_c, w_c, tab_c, tm_c)
        att_l = _attn(q_l, k_c, v_c, k_l, v_l, bq)
        zl = _sublayers(zl, mod_l, w_l, att_l, up_l, uc_l, tm_l, tm_moe_l, alpha)
        if not last:
            att_c = _attn(q_c, k_c, v_c, None, None, tm_c)
            zc = _sublayers(zc, mod_c, w_c, att_c, up_c, uc_c, tm_c, tm_moe_c, alpha)
    return zl
```

```python
import functools
import math

import jax
import jax.numpy as jnp
from jax import lax
from jax.experimental import pallas as pl
from jax.experimental.pallas import tpu as pltpu

GRID_W = 64
MLA_HEADS = 8
QK_NOPE = 64
QK_ROPE = 32
V_DIM = 64
Q_LORA = 256
KV_LORA = 128
ROPE_AXIS = QK_ROPE // 2
ROPE_BASE = 10000.0
ATTN_SCALE = (QK_NOPE + QK_ROPE) ** -0.5
POOL_WINDOWS = (2, 4, 8, 16)
POOL_GROUP = 64
POOL_WIDTH = POOL_GROUP * len(POOL_WINDOWS)
CONV_WIDTH = 256
CONV_K = 31
MLA_WIDTH = MLA_HEADS * V_DIM
OFF_KV = Q_LORA
OFF_KR = OFF_KV + KV_LORA
OFF_POOL = OFF_KR + QK_ROPE
OFF_CONV = OFF_POOL + POOL_WIDTH
N_EXPERTS = 32
TOP_K = 4
N_GROUPS = 4
TOPK_GROUPS = 2
GROUP_SIZE = N_EXPERTS // N_GROUPS
ROUTED_SCALE = 2.5
EPS = 1e-6

LANES = 128
HEAD_PAD = LANES
QK_WIDTH = MLA_HEADS * HEAD_PAD
HALO = 16
P_COLS = 1280
PC_Q, PC_KV, PC_KR, PC_POOL, PC_CA, PC_CG = 0, 256, 384, 512, 768, 1024
VMEM_LIMIT = 48 * 1024 * 1024
LOG2E = math.log2(math.e)

F32 = jnp.float32
BF16 = jnp.bfloat16


def _cparams(sem, flags=None):
    return pltpu.CompilerParams(dimension_semantics=sem, vmem_limit_bytes=VMEM_LIMIT, flags=flags)


def _ln(x):
    mu = jnp.mean(x, axis=-1, keepdims=True)
    xc = x - mu
    var = jnp.mean(xc * xc, axis=-1, keepdims=True)
    return xc * lax.rsqrt(var + EPS)


def _rms(x):
    return x * lax.rsqrt(jnp.mean(x * x, axis=-1, keepdims=True) + EPS)


def _sigmoid(x):
    return 1.0 / (1.0 + jnp.exp(-x))


def _dot(a, b):
    return jnp.dot(a, b, preferred_element_type=F32)


def _dot_nt(a, b):
    return lax.dot_general(a, b, (((1,), (1,)), ((), ())), preferred_element_type=F32)


def _ada_kernel(c_ref, w_ref, b_ref, o_ref):
    x = c_ref[...]
    x = x * _sigmoid(x)
    o_ref[0] = _dot(x.astype(BF16), w_ref[0].astype(BF16)) + b_ref[0]


def _ada(cond, ada_w, ada_b):
    depth, d, n = ada_w.shape
    tn = 1536
    return pl.pallas_call(
        _ada_kernel,
        grid=(depth, n // tn),
        in_specs=[
            pl.BlockSpec((8, d), lambda l, j: (0, 0)),
            pl.BlockSpec((1, d, tn), lambda l, j: (l, 0, j)),
            pl.BlockSpec((1, 1, tn), lambda l, j: (l, 0, j)),
        ],
        out_specs=pl.BlockSpec((1, 8, tn), lambda l, j: (l, 0, j)),
        out_shape=jax.ShapeDtypeStruct((depth, 8, n), F32),
        compiler_params=_cparams(("parallel", "parallel")),
        name="ada",
    )(cond, ada_w, ada_b.reshape(depth, 1, n))


def _proj_kernel(z_ref, mod_ref, win_ref, qg_ref, kvg_ref, wqat_ref, wqbt_ref, wka_ref, wkb_ref,
                 wvt_ref, vbt_ref, cqt_ref, sqt_ref, ck_ref, sk_ref,
                 qt_ref, k_ref, vt_ref, up_ref, uc_ref):
    z = z_ref[0]
    shift = mod_ref[0, 0:1, :]
    scale = mod_ref[0, 1:2, :]
    h = _ln(z) * (1.0 + scale) + shift
    p = _dot(h.astype(BF16), win_ref[...])
    qn = (_rms(p[:, PC_Q:PC_KV]) * qg_ref[...]).astype(BF16)
    kvn = (_rms(p[:, PC_KV:PC_KR]) * kvg_ref[...]).astype(BF16)
    xk = jnp.concatenate([kvn, p[:, PC_KR:PC_POOL].astype(BF16)], axis=1)
    qat = _dot_nt(wqat_ref[...], qn)
    qbt = _dot_nt(wqbt_ref[...], qn)
    ka = _dot(xk, wka_ref[...])
    kb = _dot(xk, wkb_ref[...])
    cqt, sqt, ck, sk = cqt_ref[...], sqt_ref[...], ck_ref[...], sk_ref[...]
    for hd in range(MLA_HEADS):
        sl = slice(hd * HEAD_PAD, (hd + 1) * HEAD_PAD)
        qt_ref[0, sl, :] = (qat[sl, :] * cqt + qbt[sl, :] * sqt).astype(BF16)
        k_ref[0, :, sl] = (ka[:, sl] * ck + kb[:, sl] * sk).astype(BF16)
    vt_ref[0, 0] = (_dot_nt(wvt_ref[...], kvn) + vbt_ref[...]).astype(BF16)
    up_ref[0] = p[:, PC_POOL:PC_CA]
    uc_ref[0] = p[:, PC_CA:PC_CG] * _sigmoid(p[:, PC_CG:P_COLS])


def _proj(z, mod, w, tabs, tm):
    b, s, d = z.shape
    full = lambda shape: pl.BlockSpec(shape, lambda i, j: (0,) * len(shape))
    tok = lambda width: pl.BlockSpec((1, tm, width), lambda i, j: (i, j, 0))
    tab = pl.BlockSpec((tm, HEAD_PAD), lambda i, j: (j, 0))
    tab_t = pl.BlockSpec((HEAD_PAD, tm), lambda i, j: (0, j))
    return pl.pallas_call(
        _proj_kernel,
        grid=(b, s // tm),
        in_specs=[
            tok(d),
            pl.BlockSpec((1, 6, d), lambda i, j: (i, 0, 0)),
            full((d, P_COLS)), full((1, Q_LORA)), full((1, KV_LORA)),
            full((QK_WIDTH, Q_LORA)), full((QK_WIDTH, Q_LORA)),
            full((2 * KV_LORA, QK_WIDTH)), full((2 * KV_LORA, QK_WIDTH)),
            full((QK_WIDTH, KV_LORA)), full((QK_WIDTH, 1)),
            tab_t, tab_t, tab, tab,
        ],
        out_specs=[
            pl.BlockSpec((1, QK_WIDTH, tm), lambda i, j: (i, 0, j)),
            tok(QK_WIDTH),
            pl.BlockSpec((1, 1, QK_WIDTH, tm), lambda i, j: (i, j, 0, 0)),
            tok(POOL_WIDTH), tok(CONV_WIDTH)],
        out_shape=[
            jax.ShapeDtypeStruct((b, QK_WIDTH, s), BF16),
            jax.ShapeDtypeStruct((b, s, QK_WIDTH), BF16),
            jax.ShapeDtypeStruct((b, s // tm, QK_WIDTH, tm), BF16),
            jax.ShapeDtypeStruct((b, s, POOL_WIDTH), F32),
            jax.ShapeDtypeStruct((b, s, CONV_WIDTH), F32),
        ],
        compiler_params=_cparams(("parallel", "parallel")),
        name="proj",
    )(z, mod, w["w_in"], w["q_g"], w["kv_g"], w["wqat"], w["wqbt"], w["wka"], w["wkb"],
      w["wvt"], w["vbt"], *tabs)


HEADS_PER_STEP = 2


def _attn_body(qt_ref, kc_ref, vct_ref, kl_ref, vlt_ref, o_ref, s_refs, p_refs):
    heads = range(HEADS_PER_STEP)
    sls = [slice(hh * HEAD_PAD, (hh + 1) * HEAD_PAD) for hh in heads]
    qts = [qt_ref[0, sl, :] for sl in sls]
    ms, accs = [], []
    for qt, sl in zip(qts, sls):
        st = _dot(kc_ref[0, :, sl], qt)
        m = jnp.max(st, axis=0, keepdims=True)
        pt = jnp.exp2(st - m)
        ms.append(m)
        accs.append(_dot(vct_ref[0, 0, sl, :], pt.astype(BF16)))

    if kl_ref is not None:
        n, bk = vlt_ref.shape[1], vlt_ref.shape[3]
        assert n >= 2 and n % 2 == 0

        def score(c, slot):
            r0 = c * bk if isinstance(c, int) else pl.multiple_of(c * bk, bk)
            cms = []
            for hh in heads:
                st = _dot(kl_ref[0, pl.ds(r0, bk), sls[hh]], qts[hh])
                s_refs[slot][hh] = st
                cms.append(jnp.max(st, axis=0, keepdims=True))
            return cms

        def probs(slot, ms, cms):
            m_new = [jnp.maximum(ms[hh], cms[hh]) for hh in heads]
            alphas = [jnp.exp2(ms[hh] - m_new[hh]) for hh in heads]
            for hh in heads:
                p_refs[slot][hh] = jnp.exp2((s_refs[slot][hh] - m_new[hh]).astype(BF16))
            return m_new, alphas

        def accum(c, slot, accs, alphas):
            return [accs[hh] * alphas[hh] + _dot(vlt_ref[0, c, sls[hh], :], p_refs[slot][hh])
                    for hh in heads]

        def half(c, cur, nxt, ms, cms, accs, with_score=True):
            ms, alphas = probs(cur, ms, cms)
            if with_score:
                cms = score(c + 1, nxt)
            accs = accum(c, cur, accs, alphas)
            return ms, cms, accs

        cms = score(0, 0)

        def pair(t, carry):
            ms, cms, accs = [list(x) for x in carry]
            c = 2 * t
            ms, cms, accs = half(c, 0, 1, ms, cms, accs)
            ms, cms, accs = half(c + 1, 1, 0, ms, cms, accs)
            return tuple(tuple(x) for x in (ms, cms, accs))

        carry = tuple(tuple(x) for x in (ms, cms, accs))
        carry = lax.fori_loop(0, (n - 2) // 2, pair, carry)
        ms, cms, accs = [list(x) for x in carry]
        ms, cms, accs = half(n - 2, 0, 1, ms, cms, accs)
        ms, cms, accs = half(n - 1, 1, 0, ms, cms, accs, with_score=False)

    acc_even, acc_odd = accs
    row = lax.broadcasted_iota(jnp.int32, acc_even.shape, 0)
    ot = jnp.where(row < V_DIM, acc_even / acc_even[V_DIM:V_DIM + 1, :], acc_odd / acc_odd[0:1, :])
    o_ref[0] = ot.T.astype(o_ref.dtype)


def _attn_kernel_full(qt_ref, kc_ref, vct_ref, kl_ref, vlt_ref, o_ref, s0_ref, s1_ref, p0_ref, p1_ref):
    _attn_body(qt_ref, kc_ref, vct_ref, kl_ref, vlt_ref, o_ref, (s0_ref, s1_ref), (p0_ref, p1_ref))


def _attn_kernel_ctx(qt_ref, kc_ref, vct_ref, o_ref):
    _attn_body(qt_ref, kc_ref, vct_ref, None, None, o_ref, None, None)


def _attn(qt, kc, vct, kl, vlt, bq):
    b, _, s = qt.shape
    nc = kc.shape[1]
    w2 = HEADS_PER_STEP * HEAD_PAD
    in_specs = [
        pl.BlockSpec((1, w2, bq), lambda i, h, j: (i, h, j)),
        pl.BlockSpec((1, nc, w2), lambda i, h, j: (i, 0, h)),
        pl.BlockSpec((1, 1, w2, nc), lambda i, h, j: (i, 0, h, 0)),
    ]
    args = [qt, kc, vct]
    scratch = []
    if kl is None:
        body = _attn_kernel_ctx
    else:
        bk = vlt.shape[3]
        scratch = [pltpu.VMEM((HEADS_PER_STEP, bk, bq), F32)] * 2 + [pltpu.VMEM((HEADS_PER_STEP, bk, bq), BF16)] * 2
        in_specs += [
            pl.BlockSpec((1, kl.shape[1], w2), lambda i, h, j: (i, 0, h)),
            pl.BlockSpec((1, vlt.shape[1], w2, vlt.shape[3]), lambda i, h, j: (i, 0, h, 0)),
        ]
        args += [kl, vlt]
        body = _attn_kernel_full
    return pl.pallas_call(
        body,
        grid=(b, MLA_HEADS // HEADS_PER_STEP, s // bq),
        in_specs=in_specs,
        out_specs=pl.BlockSpec((1, bq, HEADS_PER_STEP * V_DIM), lambda i, h, j: (i, j, h)),
        out_shape=jax.ShapeDtypeStruct((b, s, MLA_WIDTH), BF16),
        scratch_shapes=scratch,
        compiler_params=_cparams(("parallel", "parallel", "arbitrary")),
        name="attn",
    )(*args)


SUBLANES = 8


def _fill_ext(ext_ref, sh_ref, main_ref, left_ref, right_ref, tm):
    j = pl.program_id(1)
    last = pl.num_programs(1) - 1
    ext_ref[0:HALO, :] = jnp.where(j > 0, left_ref[0], 0.0)
    ext_ref[HALO:HALO + tm, :] = main_ref[0]
    ext_ref[HALO + tm:HALO + tm + HALO, :] = jnp.where(j < last, right_ref[0], 0.0)
    n = sh_ref.shape[1]
    for b in range(1, SUBLANES):
        sh_ref[b - 1, :, :] = ext_ref[b:b + n, :]


def _mix_kernel(att_ref, up_ref, upl_ref, upr_ref, uc_ref, ucl_ref, ucr_ref, z_ref, mod_ref,
                poolw_ref, pools_ref, dw_ref, cb_ref, clg_ref, clb_ref, cpw_ref, wout_ref,
                l1g_ref, l1b_ref, z1_ref, h_ref, ext_ref, sh_ref, *, tm, seq, alpha):
    j = pl.program_id(1)

    def shifted(d):
        a, b = divmod(HALO + d, SUBLANES)
        if b == 0:
            return ext_ref[a * SUBLANES:a * SUBLANES + tm, :]
        return sh_ref[b - 1, a * SUBLANES:a * SUBLANES + tm, :]

    _fill_ext(ext_ref, sh_ref, up_ref, upl_ref, upr_ref, tm)
    x = up_ref[0]
    lane = lax.broadcasted_iota(jnp.int32, (tm, POOL_WIDTH), 1)
    t = lax.broadcasted_iota(jnp.int32, (tm, POOL_WIDTH), 0) + j * tm

    run = x + shifted(-1)
    sums = [run]
    for lo in (2, 4, 8):
        for d in list(range(-lo, -lo // 2)) + list(range(lo // 2, lo)):
            run = run + shifted(d)
        sums.append(run)
    wsum = jnp.where(lane < POOL_GROUP, sums[0],
                     jnp.where(lane < 2 * POOL_GROUP, sums[1],
                               jnp.where(lane < 3 * POOL_GROUP, sums[2], sums[3])))
    lo = jnp.where(lane < POOL_GROUP, 1,
                   jnp.where(lane < 2 * POOL_GROUP, 2, jnp.where(lane < 3 * POOL_GROUP, 4, 8)))
    cnt = jnp.minimum(t + lo, seq) - jnp.maximum(t - lo, 0)
    diff = wsum / cnt.astype(F32) - x
    pooled = _dot(diff.astype(BF16), poolw_ref[...]) * pools_ref[...]

    _fill_ext(ext_ref, sh_ref, uc_ref, ucl_ref, ucr_ref, tm)
    acc = jnp.zeros((tm, CONV_WIDTH), F32) + cb_ref[...]
    for k in range(CONV_K):
        acc = acc + dw_ref[k:k + 1, :] * shifted(k - CONV_K // 2)
    y = _ln(acc) * clg_ref[...] + clb_ref[...]
    y = y * _sigmoid(y)
    conv = _dot(y.astype(BF16), cpw_ref[...])

    cat = jnp.concatenate([att_ref[0], pooled.astype(BF16), conv.astype(BF16)], axis=1)
    out = _dot(cat, wout_ref[...])
    gate = mod_ref[0, 2:3, :]
    z1 = _ln(alpha * z_ref[0] + gate * out) * l1g_ref[...] + l1b_ref[...]
    z1_ref[0] = z1
    h_ref[0] = (_ln(z1) * (1.0 + mod_ref[0, 4:5, :]) + mod_ref[0, 3:4, :]).astype(BF16)


def _mix(att, up, uc, z, mod, w, tm, alpha):
    b, s, d = z.shape
    hb = tm // HALO
    nhb = s // HALO
    full = lambda shape: pl.BlockSpec(shape, lambda i, j: (0,) * len(shape))
    tok = lambda width: pl.BlockSpec((1, tm, width), lambda i, j: (i, j, 0))
    left = lambda width: pl.BlockSpec((1, HALO, width),
                                      lambda i, j: (i, jnp.maximum(j * hb - 1, 0), 0))
    right = lambda width: pl.BlockSpec((1, HALO, width),
                                       lambda i, j: (i, jnp.minimum((j + 1) * hb, nhb - 1), 0))
    body = functools.partial(_mix_kernel, tm=tm, seq=s, alpha=alpha)
    return pl.pallas_call(
        body,
        grid=(b, s // tm),
        in_specs=[
            tok(MLA_WIDTH),
            tok(POOL_WIDTH), left(POOL_WIDTH), right(POOL_WIDTH),
            tok(CONV_WIDTH), left(CONV_WIDTH), right(CONV_WIDTH),
            tok(d),
            pl.BlockSpec((1, 6, d), lambda i, j: (i, 0, 0)),
            full((POOL_WIDTH, POOL_WIDTH)), full((1, POOL_WIDTH)),
            full((CONV_K + 1, CONV_WIDTH)), full((1, CONV_WIDTH)), full((1, CONV_WIDTH)),
            full((1, CONV_WIDTH)), full((CONV_WIDTH, CONV_WIDTH)),
            full((d, d)), full((1, d)), full((1, d)),
        ],
        out_specs=[tok(d), tok(d)],
        out_shape=[jax.ShapeDtypeStruct((b, s, d), F32), jax.ShapeDtypeStruct((b, s, d), BF16)],
        scratch_shapes=[pltpu.VMEM((tm + 2 * HALO, POOL_WIDTH), F32),
                        pltpu.VMEM((SUBLANES - 1, tm + 2 * HALO - SUBLANES, POOL_WIDTH), F32)],
        compiler_params=_cparams(("parallel", "parallel")),
        name="mix",
    )(att, up, up, up, uc, uc, uc, z, mod, w["pool_w"], w["pool_s"], w["conv_dw"], w["conv_b"],
      w["conv_ln_g"], w["conv_ln_b"], w["conv_pw"], w["w_out"], w["ln1_g"], w["ln1_b"])


def _route_kernel(h_ref, rw_ref, rb_ref, g_ref):
    tm = h_ref.shape[0]
    logits = _dot(h_ref[...], rw_ref[...])
    scores = _sigmoid(logits.T[0:N_EXPERTS, :])
    sel = scores + rb_ref[...]
    neg = -jnp.inf

    row8 = lax.broadcasted_iota(jnp.int32, (GROUP_SIZE, tm), 0)
    gscore = []
    for g in range(N_GROUPS):
        xg = sel[g * GROUP_SIZE:(g + 1) * GROUP_SIZE, :]
        m1 = jnp.max(xg, axis=0, keepdims=True)
        i1 = jnp.min(jnp.where(xg == m1, row8, GROUP_SIZE), axis=0, keepdims=True)
        m2 = jnp.max(jnp.where(row8 == i1, neg, xg), axis=0, keepdims=True)
        gscore.append(m1 + m2)
    masked = []
    for g in range(N_GROUPS):
        ahead = jnp.zeros((1, tm), F32)
        for g2 in range(N_GROUPS):
            if g2 < g:
                ahead = ahead + jnp.where(gscore[g2] >= gscore[g], 1.0, 0.0)
            elif g2 > g:
                ahead = ahead + jnp.where(gscore[g2] > gscore[g], 1.0, 0.0)
        keep = ahead < TOPK_GROUPS
        masked.append(jnp.where(keep, sel[g * GROUP_SIZE:(g + 1) * GROUP_SIZE, :], neg))
    masked = jnp.concatenate(masked, axis=0)

    row = lax.broadcasted_iota(jnp.int32, (N_EXPERTS, tm), 0)
    chosen = jnp.zeros((N_EXPERTS, tm), F32)
    for _ in range(TOP_K):
        m = jnp.max(masked, axis=0, keepdims=True)
        idx = jnp.min(jnp.where(masked == m, row, N_EXPERTS), axis=0, keepdims=True)
        hit = row == idx
        chosen = jnp.where(hit, 1.0, chosen)
        masked = jnp.where(hit, neg, masked)
    w = chosen * scores
    gates = w / jnp.sum(w, axis=0, keepdims=True) * ROUTED_SCALE
    pad = jnp.zeros((LANES - N_EXPERTS, tm), F32)
    g_ref[...] = jnp.concatenate([gates, pad], axis=0).T


def _route(h, rw, rb, tm):
    t, d = h.shape
    return pl.pallas_call(
        _route_kernel,
        grid=(t // tm,),
        in_specs=[
            pl.BlockSpec((tm, d), lambda i: (i, 0)),
            pl.BlockSpec((d, LANES), lambda i: (0, 0)),
            pl.BlockSpec((N_EXPERTS, 1), lambda i: (0, 0)),
        ],
        out_specs=pl.BlockSpec((tm, LANES), lambda i: (i, 0)),
        out_shape=jax.ShapeDtypeStruct((t, LANES), F32),
        compiler_params=_cparams(("parallel",)),
        name="route",
    )(h, rw, rb)


def _swiglu_hidden(x, wg, wu):
    hg = _dot(x, wg)
    return hg * _sigmoid(hg) * _dot(x, wu)


def _moe_kernel(h_ref, g_ref, wg_ref, wu_ref, wd_ref, swg_ref, swu_ref, swd_ref, z1_ref, mod_ref,
                l2g_ref, l2b_ref, o_ref, acc_ref, *, alpha):
    e = pl.program_id(1)
    x = h_ref[...]

    @pl.when(e == 0)
    def _():
        a = _swiglu_hidden(x, swg_ref[...], swu_ref[...])
        acc_ref[...] = _dot(a.astype(BF16), swd_ref[...])

    lane = lax.broadcasted_iota(jnp.int32, g_ref.shape, 1)
    gcol = jnp.sum(jnp.where(lane == e, g_ref[...], 0.0), axis=1, keepdims=True)
    a = _swiglu_hidden(x, wg_ref[0, 0], wu_ref[0, 0]) * gcol
    acc_ref[...] += _dot(a.astype(BF16), wd_ref[0, 0])

    @pl.when(e == pl.num_programs(1) - 1)
    def _():
        gate = mod_ref[0, 5:6, :]
        o_ref[...] = _ln(alpha * z1_ref[...] + gate * acc_ref[...]) * l2g_ref[...] + l2b_ref[...]


def _moe(h, gates, l, ew, w, z1, mod, tm, tiles_per_batch, alpha):
    t, d = h.shape
    wg, wu, wd = ew
    ne, de = wg.shape[1], wg.shape[3]
    body = functools.partial(_moe_kernel, alpha=alpha)
    return pl.pallas_call(
        body,
        grid=(t // tm, ne),
        in_specs=[
            pl.BlockSpec((tm, d), lambda i, e: (i, 0)),
            pl.BlockSpec((tm, LANES), lambda i, e: (i, 0)),
            pl.BlockSpec((1, 1, d, de), lambda i, e: (l, e, 0, 0)),
            pl.BlockSpec((1, 1, d, de), lambda i, e: (l, e, 0, 0)),
            pl.BlockSpec((1, 1, de, d), lambda i, e: (l, e, 0, 0)),
            pl.BlockSpec((d, de), lambda i, e: (0, 0)),
            pl.BlockSpec((d, de), lambda i, e: (0, 0)),
            pl.BlockSpec((de, d), lambda i, e: (0, 0)),
            pl.BlockSpec((tm, d), lambda i, e: (i, 0)),
            pl.BlockSpec((1, 6, d), lambda i, e: (i // tiles_per_batch, 0, 0)),
            pl.BlockSpec((1, d), lambda i, e: (0, 0)),
            pl.BlockSpec((1, d), lambda i, e: (0, 0)),
        ],
        out_specs=pl.BlockSpec((tm, d), lambda i, e: (i, 0)),
        out_shape=jax.ShapeDtypeStruct((t, d), F32),
        scratch_shapes=[pltpu.VMEM((tm, d), F32)],
        compiler_params=_cparams(("parallel", "arbitrary")),
        name="moe",
    )(h, gates, wg, wu, wd, w["sh_wg"], w["sh_wu"], w["sh_wd"], z1, mod, w["ln2_g"], w["ln2_b"])


def _rope_swap(r):
    r4 = r.reshape(r.shape[:-1] + (2, 2, ROPE_AXIS // 2))
    return jnp.stack([-r4[..., 1, :], r4[..., 0, :]], axis=-2).reshape(r.shape)


def _prep_layer(l, w_in, q_norm_g, w_uq, kv_norm_g, w_ukv, pool_w, pool_scale, conv_dw, conv_b,
                conv_ln_g, conv_ln_b, conv_pw, w_out, ln1_g, ln1_b, router_w, router_bias,
                exp_wg, exp_wu, exp_wd, sh_wg, sh_wu, sh_wd, ln2_g, ln2_b):
    d = w_in.shape[1]
    wi = w_in[l]
    w_in_p = jnp.concatenate([
        wi[:, 0:OFF_KR], wi[:, OFF_KR:OFF_POOL], jnp.zeros((d, PC_POOL - PC_KR - QK_ROPE), F32),
        wi[:, OFF_POOL:]], axis=1).astype(BF16)

    uq = w_uq[l].reshape(Q_LORA, MLA_HEADS, QK_NOPE + QK_ROPE)
    nope, rope = uq[..., :QK_NOPE], uq[..., QK_NOPE:]
    wqat = jnp.concatenate([nope, rope, rope], axis=-1).reshape(Q_LORA, QK_WIDTH).T.astype(BF16)
    wqbt = jnp.concatenate([jnp.zeros_like(nope), jnp.zeros_like(rope), _rope_swap(rope)],
                           axis=-1).reshape(Q_LORA, QK_WIDTH).T.astype(BF16)

    ukv = w_ukv[l].reshape(KV_LORA, MLA_HEADS, QK_NOPE + V_DIM)
    kn, vv = ukv[..., :QK_NOPE], ukv[..., QK_NOPE:]
    z32 = jnp.zeros((KV_LORA, MLA_HEADS, QK_ROPE), F32)
    top = jnp.concatenate([kn, z32, z32], axis=-1)
    eye = jnp.broadcast_to(jnp.eye(QK_ROPE, dtype=F32)[:, None, :], (QK_ROPE, MLA_HEADS, QK_ROPE))
    zr = jnp.zeros((QK_ROPE, MLA_HEADS, QK_ROPE), F32)
    zn = jnp.zeros((QK_ROPE, MLA_HEADS, QK_NOPE), F32)
    padrows = jnp.zeros((KV_LORA - QK_ROPE, MLA_HEADS, HEAD_PAD), F32)

    def kmat(kr_rows):
        return jnp.concatenate([top, kr_rows, padrows], axis=0).reshape(2 * KV_LORA, QK_WIDTH).astype(BF16)

    wka_lat = kmat(jnp.concatenate([zn, zr, eye], axis=-1))
    wkb_lat = kmat(jnp.concatenate([zn, zr, _rope_swap(eye)], axis=-1))
    wka_ctx = kmat(jnp.concatenate([zn, eye, zr], axis=-1))
    wkb_ctx = jnp.zeros((2 * KV_LORA, QK_WIDTH), BF16)

    zv = jnp.zeros_like(vv)
    even = (jnp.arange(MLA_HEADS) % 2 == 0)[None, :, None]
    wv = jnp.where(even, jnp.concatenate([vv, zv], -1), jnp.concatenate([zv, vv], -1))
    wvt = wv.reshape(KV_LORA, QK_WIDTH).T.astype(BF16)
    lane = jnp.arange(QK_WIDTH) % (2 * HEAD_PAD)
    vbt = ((lane == V_DIM) | (lane == HEAD_PAD)).astype(F32)[:, None]

    pw = jnp.zeros((POOL_WIDTH, POOL_WIDTH), F32)
    for gi in range(len(POOL_WINDOWS)):
        pw = pw.at[gi * POOL_GROUP:(gi + 1) * POOL_GROUP, gi * POOL_GROUP:(gi + 1) * POOL_GROUP].set(pool_w[l, gi])

    common = dict(
        w_in=w_in_p, q_g=q_norm_g[l][None], kv_g=kv_norm_g[l][None], wqat=wqat, wqbt=wqbt, wvt=wvt, vbt=vbt,
        pool_w=pw.astype(BF16), pool_s=pool_scale[l][None],
        conv_dw=jnp.concatenate([conv_dw[l], jnp.zeros((1, CONV_WIDTH), F32)], axis=0),
        conv_b=conv_b[l][None], conv_ln_g=conv_ln_g[l][None], conv_ln_b=conv_ln_b[l][None],
        conv_pw=conv_pw[l].astype(BF16), w_out=w_out[l].astype(BF16),
        ln1_g=ln1_g[l][None], ln1_b=ln1_b[l][None],
        router_w=jnp.pad(router_w[l], ((0, 0), (0, LANES - N_EXPERTS))).astype(BF16),
        router_b=router_bias[l][:, None],
        sh_wg=sh_wg[l].astype(BF16), sh_wu=sh_wu[l].astype(BF16), sh_wd=sh_wd[l].astype(BF16),
        ln2_g=ln2_g[l][None], ln2_b=ln2_b[l][None],
    )
    return dict(common, wka=wka_lat, wkb=wkb_lat), dict(common, wka=wka_ctx, wkb=wkb_ctx)


def _rope_tables(n_lat, n_ctx):
    t = jnp.arange(n_lat)
    inv = ROPE_BASE ** (-jnp.arange(0, ROPE_AXIS, 2, dtype=F32) / ROPE_AXIS)
    ang_r = (t // GRID_W).astype(F32)[:, None] * inv
    ang_c = (t % GRID_W).astype(F32)[:, None] * inv
    cos = jnp.concatenate([jnp.cos(ang_r), jnp.cos(ang_r), jnp.cos(ang_c), jnp.cos(ang_c)], axis=1)
    sin = jnp.concatenate([jnp.sin(ang_r), jnp.sin(ang_r), jnp.sin(ang_c), jnp.sin(ang_c)], axis=1)
    ck = jnp.concatenate([jnp.ones((n_lat, HEAD_PAD - QK_ROPE), F32), cos], axis=1)
    sk = jnp.concatenate([jnp.zeros((n_lat, HEAD_PAD - QK_ROPE), F32), sin], axis=1)
    qs = ATTN_SCALE * LOG2E
    lat = ((ck * qs).T, (sk * qs).T, ck, sk)
    one = jnp.ones((n_ctx, HEAD_PAD), F32)
    zero = jnp.zeros((n_ctx, HEAD_PAD), F32)
    ctx = ((one * qs).T, zero.T, one, zero)
    return lat, ctx


def _sublayers(z, mod, l, ew, w, att, up, uc, tm, tm_moe, alpha):
    b, s, d = z.shape
    z1, h = _mix(att, up, uc, z, mod, w, tm, alpha)
    h2 = h.reshape(b * s, d)
    gates = _route(h2, w["router_w"], w["router_b"], tm)
    out = _moe(h2, gates, l, ew, w, z1.reshape(b * s, d), mod, tm_moe, s // tm_moe, alpha)
    return out.reshape(b, s, d)


def kernel(x, c, ctx, c_ctx, ada_w, ada_b, w_in, q_norm_g, w_uq, kv_norm_g, w_ukv, pool_w, pool_scale, conv_dw, conv_b, conv_ln_g, conv_ln_b, conv_pw, w_out, ln1_g, ln1_b, router_w, router_bias, exp_wg, exp_wu, exp_wd, sh_wg, sh_wu, sh_wd, ln2_g, ln2_b):
    b, s, d = x.shape
    n_ctx = ctx.shape[1]
    depth = ada_w.shape[0]
    alpha = (2 * depth) ** 0.25
    assert b + 1 <= 8 and s % GRID_W == 0

    tm_l = min(512, s)
    tm_c = min(256, n_ctx)
    tm_moe_l = min(1024, s)
    tm_moe_c = n_ctx
    bq = min(1024, s)

    cond = jnp.concatenate([c, c_ctx[None], jnp.zeros((8 - b - 1, d), F32)], axis=0)
    mods = _ada(cond, ada_w, ada_b)
    tab_l, tab_c = _rope_tables(s, n_ctx)
    ew = (exp_wg.astype(BF16), exp_wu.astype(BF16), exp_wd.astype(BF16))

    zl, zc = x, ctx
    for l in range(depth):
        last = l == depth - 1
        w_l, w_c = _prep_layer(l, w_in, q_norm_g, w_uq, kv_norm_g, w_ukv, pool_w, pool_scale,
                               conv_dw, conv_b, conv_ln_g, conv_ln_b, conv_pw, w_out, ln1_g, ln1_b,
                               router_w, router_bias, exp_wg, exp_wu, exp_wd, sh_wg, sh_wu, sh_wd,
                               ln2_g, ln2_b)
        mod_l = mods[l, :b].reshape(b, 6, d)
        mod_c = jnp.broadcast_to(mods[l, b].reshape(1, 6, d), (b, 6, d))

        q_l, k_l, v_l, up_l, uc_l = _proj(zl, mod_l, w_l, tab_l, tm_l)
        q_c, k_c, v_c, up_c, uc_c = _proj(zc, mod_c, w_c, tab_c, tm_c)
        att_l = _attn(q_l, k_c, v_c, k_l, v_l, bq)
        zl = _sublayers(zl, mod_l, l, ew, w_l, att_l, up_l, uc_l, tm_l, tm_moe_l, alpha)
        if not last:
            att_c = _attn(q_c, k_c, v_c, None, None, tm_c)
            zc = _sublayers(zc, mod_c, l, ew, w_c, att_c, up_c, uc_c, tm_c, tm_moe_c, alpha)
    return zl
```

```python
import functools
import math

import jax
import jax.numpy as jnp
from jax import lax
from jax.experimental import pallas as pl
from jax.experimental.pallas import tpu as pltpu

GRID_W = 64
MLA_HEADS = 8
QK_NOPE = 64
QK_ROPE = 32
V_DIM = 64
Q_LORA = 256
KV_LORA = 128
ROPE_AXIS = QK_ROPE // 2
ROPE_BASE = 10000.0
ATTN_SCALE = (QK_NOPE + QK_ROPE) ** -0.5
POOL_WINDOWS = (2, 4, 8, 16)
POOL_GROUP = 64
POOL_WIDTH = POOL_GROUP * len(POOL_WINDOWS)
CONV_WIDTH = 256
CONV_K = 31
MLA_WIDTH = MLA_HEADS * V_DIM
OFF_KV = Q_LORA
OFF_KR = OFF_KV + KV_LORA
OFF_POOL = OFF_KR + QK_ROPE
OFF_CONV = OFF_POOL + POOL_WIDTH
N_EXPERTS = 32
TOP_K = 4
N_GROUPS = 4
TOPK_GROUPS = 2
GROUP_SIZE = N_EXPERTS // N_GROUPS
ROUTED_SCALE = 2.5
EPS = 1e-6

LANES = 128
HEAD_PAD = LANES
QK_WIDTH = MLA_HEADS * HEAD_PAD
HALO = 16
P_COLS = 1280
PC_Q, PC_KV, PC_KR, PC_POOL, PC_CA, PC_CG = 0, 256, 384, 512, 768, 1024
VMEM_LIMIT = 48 * 1024 * 1024
LOG2E = math.log2(math.e)

F32 = jnp.float32
BF16 = jnp.bfloat16


def _cparams(sem, flags=None):
    return pltpu.CompilerParams(dimension_semantics=sem, vmem_limit_bytes=VMEM_LIMIT, flags=flags)


def _ln(x):
    mu = jnp.mean(x, axis=-1, keepdims=True)
    xc = x - mu
    var = jnp.mean(xc * xc, axis=-1, keepdims=True)
    return xc * lax.rsqrt(var + EPS)


def _rms(x):
    return x * lax.rsqrt(jnp.mean(x * x, axis=-1, keepdims=True) + EPS)


def _sigmoid(x):
    return 1.0 / (1.0 + jnp.exp(-x))


def _dot(a, b):
    return jnp.dot(a, b, preferred_element_type=F32)


def _dot_nt(a, b):
    return lax.dot_general(a, b, (((1,), (1,)), ((), ())), preferred_element_type=F32)


def _ada_kernel(c_ref, w_ref, b_ref, o_ref):
    x = c_ref[...]
    x = x * _sigmoid(x)
    o_ref[0] = _dot(x.astype(BF16), w_ref[0].astype(BF16)) + b_ref[0]


def _ada(cond, ada_w, ada_b):
    depth, d, n = ada_w.shape
    tn = 1536
    return pl.pallas_call(
        _ada_kernel,
        grid=(depth, n // tn),
        in_specs=[
            pl.BlockSpec((8, d), lambda l, j: (0, 0)),
            pl.BlockSpec((1, d, tn), lambda l, j: (l, 0, j)),
            pl.BlockSpec((1, 1, tn), lambda l, j: (l, 0, j)),
        ],
        out_specs=pl.BlockSpec((1, 8, tn), lambda l, j: (l, 0, j)),
        out_shape=jax.ShapeDtypeStruct((depth, 8, n), F32),
        compiler_params=_cparams(("parallel", "parallel")),
        name="ada",
    )(cond, ada_w, ada_b.reshape(depth, 1, n))


def _proj_kernel(z_ref, mod_ref, win_ref, qg_ref, kvg_ref, wqat_ref, wqbt_ref, wka_ref, wkb_ref,
                 wvt_ref, vbt_ref, cqt_ref, sqt_ref, ck_ref, sk_ref,
                 qt_ref, k_ref, vt_ref, up_ref, uc_ref):
    z = z_ref[0]
    shift = mod_ref[0, 0:1, :]
    scale = mod_ref[0, 1:2, :]
    h = _ln(z) * (1.0 + scale) + shift
    p = _dot(h.astype(BF16), win_ref[...])
    qn = (_rms(p[:, PC_Q:PC_KV]) * qg_ref[...]).astype(BF16)
    kvn = (_rms(p[:, PC_KV:PC_KR]) * kvg_ref[...]).astype(BF16)
    xk = jnp.concatenate([kvn, p[:, PC_KR:PC_POOL].astype(BF16)], axis=1)
    qat = _dot_nt(wqat_ref[...], qn)
    qbt = _dot_nt(wqbt_ref[...], qn)
    ka = _dot(xk, wka_ref[...])
    kb = _dot(xk, wkb_ref[...])
    cqt, sqt, ck, sk = cqt_ref[...], sqt_ref[...], ck_ref[...], sk_ref[...]
    for hd in range(MLA_HEADS):
        sl = slice(hd * HEAD_PAD, (hd + 1) * HEAD_PAD)
        qt_ref[0, sl, :] = (qat[sl, :] * cqt + qbt[sl, :] * sqt).astype(BF16)
        k_ref[0, :, sl] = (ka[:, sl] * ck + kb[:, sl] * sk).astype(BF16)
    vt_ref[0, 0] = (_dot_nt(wvt_ref[...], kvn) + vbt_ref[...]).astype(BF16)
    up_ref[0] = p[:, PC_POOL:PC_CA]
    uc_ref[0] = p[:, PC_CA:PC_CG] * _sigmoid(p[:, PC_CG:P_COLS])


def _proj(z, mod, w, tabs, tm):
    b, s, d = z.shape
    full = lambda shape: pl.BlockSpec(shape, lambda i, j: (0,) * len(shape))
    tok = lambda width: pl.BlockSpec((1, tm, width), lambda i, j: (i, j, 0))
    tab = pl.BlockSpec((tm, HEAD_PAD), lambda i, j: (j, 0))
    tab_t = pl.BlockSpec((HEAD_PAD, tm), lambda i, j: (0, j))
    return pl.pallas_call(
        _proj_kernel,
        grid=(b, s // tm),
        in_specs=[
            tok(d),
            pl.BlockSpec((1, 6, d), lambda i, j: (i, 0, 0)),
            full((d, P_COLS)), full((1, Q_LORA)), full((1, KV_LORA)),
            full((QK_WIDTH, Q_LORA)), full((QK_WIDTH, Q_LORA)),
            full((2 * KV_LORA, QK_WIDTH)), full((2 * KV_LORA, QK_WIDTH)),
            full((QK_WIDTH, KV_LORA)), full((QK_WIDTH, 1)),
            tab_t, tab_t, tab, tab,
        ],
        out_specs=[
            pl.BlockSpec((1, QK_WIDTH, tm), lambda i, j: (i, 0, j)),
            tok(QK_WIDTH),
            pl.BlockSpec((1, 1, QK_WIDTH, tm), lambda i, j: (i, j, 0, 0)),
            tok(POOL_WIDTH), tok(CONV_WIDTH)],
        out_shape=[
            jax.ShapeDtypeStruct((b, QK_WIDTH, s), BF16),
            jax.ShapeDtypeStruct((b, s, QK_WIDTH), BF16),
            jax.ShapeDtypeStruct((b, s // tm, QK_WIDTH, tm), BF16),
            jax.ShapeDtypeStruct((b, s, POOL_WIDTH), F32),
            jax.ShapeDtypeStruct((b, s, CONV_WIDTH), F32),
        ],
        compiler_params=_cparams(("parallel", "parallel")),
        name="proj",
    )(z, mod, w["w_in"], w["q_g"], w["kv_g"], w["wqat"], w["wqbt"], w["wka"], w["wkb"],
      w["wvt"], w["vbt"], *tabs)


HEADS_PER_STEP = 2


def _attn_body(qt_ref, kc_ref, vct_ref, kl_ref, vlt_ref, o_ref, s_refs, p_refs):
    heads = range(HEADS_PER_STEP)
    sls = [slice(hh * HEAD_PAD, (hh + 1) * HEAD_PAD) for hh in heads]
    qts = [qt_ref[0, sl, :] for sl in sls]
    ms, accs = [], []
    for qt, sl in zip(qts, sls):
        st = _dot(kc_ref[0, :, sl], qt)
        m = jnp.max(st, axis=0, keepdims=True)
        pt = jnp.exp2(st - m)
        ms.append(m)
        accs.append(_dot(vct_ref[0, 0, sl, :], pt.astype(BF16)))

    if kl_ref is not None:
        n, bk = vlt_ref.shape[1], vlt_ref.shape[3]
        assert n >= 2 and n % 2 == 0

        def score(c, slot):
            r0 = c * bk if isinstance(c, int) else pl.multiple_of(c * bk, bk)
            cms = []
            for hh in heads:
                st = _dot(kl_ref[0, pl.ds(r0, bk), sls[hh]], qts[hh])
                s_refs[slot][hh] = st
                cms.append(jnp.max(st, axis=0, keepdims=True))
            return cms

        def probs(slot, ms, cms):
            m_new = [jnp.maximum(ms[hh], cms[hh]) for hh in heads]
            alphas = [jnp.exp2(ms[hh] - m_new[hh]) for hh in heads]
            for hh in heads:
                p_refs[slot][hh] = jnp.exp2((s_refs[slot][hh] - m_new[hh]).astype(BF16))
            return m_new, alphas

        def accum(c, slot, accs, alphas):
            return [accs[hh] * alphas[hh] + _dot(vlt_ref[0, c, sls[hh], :], p_refs[slot][hh])
                    for hh in heads]

        def half(c, cur, nxt, ms, cms, accs, with_score=True):
            ms, alphas = probs(cur, ms, cms)
            if with_score:
                cms = score(c + 1, nxt)
            accs = accum(c, cur, accs, alphas)
            return ms, cms, accs

        cms = score(0, 0)

        def pair(t, carry):
            ms, cms, accs = [list(x) for x in carry]
            c = 2 * t
            ms, cms, accs = half(c, 0, 1, ms, cms, accs)
            ms, cms, accs = half(c + 1, 1, 0, ms, cms, accs)
            return tuple(tuple(x) for x in (ms, cms, accs))

        carry = tuple(tuple(x) for x in (ms, cms, accs))
        carry = lax.fori_loop(0, (n - 2) // 2, pair, carry)
        ms, cms, accs = [list(x) for x in carry]
        ms, cms, accs = half(n - 2, 0, 1, ms, cms, accs)
        ms, cms, accs = half(n - 1, 1, 0, ms, cms, accs, with_score=False)

    acc_even, acc_odd = accs
    row = lax.broadcasted_iota(jnp.int32, acc_even.shape, 0)
    ot = jnp.where(row < V_DIM, acc_even / acc_even[V_DIM:V_DIM + 1, :], acc_odd / acc_odd[0:1, :])
    o_ref[0] = ot.T.astype(o_ref.dtype)


def _attn_kernel_full(qt_ref, kc_ref, vct_ref, kl_ref, vlt_ref, o_ref, s0_ref, s1_ref, p0_ref, p1_ref):
    _attn_body(qt_ref, kc_ref, vct_ref, kl_ref, vlt_ref, o_ref, (s0_ref, s1_ref), (p0_ref, p1_ref))


def _attn_kernel_ctx(qt_ref, kc_ref, vct_ref, o_ref):
    _attn_body(qt_ref, kc_ref, vct_ref, None, None, o_ref, None, None)


def _attn(qt, kc, vct, kl, vlt, bq):
    b, _, s = qt.shape
    nc = kc.shape[1]
    w2 = HEADS_PER_STEP * HEAD_PAD
    in_specs = [
        pl.BlockSpec((1, w2, bq), lambda i, h, j: (i, h, j)),
        pl.BlockSpec((1, nc, w2), lambda i, h, j: (i, 0, h)),
        pl.BlockSpec((1, 1, w2, nc), lambda i, h, j: (i, 0, h, 0)),
    ]
    args = [qt, kc, vct]
    scratch = []
    if kl is None:
        body = _attn_kernel_ctx
    else:
        bk = vlt.shape[3]
        scratch = [pltpu.VMEM((HEADS_PER_STEP, bk, bq), F32)] * 2 + [pltpu.VMEM((HEADS_PER_STEP, bk, bq), BF16)] * 2
        in_specs += [
            pl.BlockSpec((1, kl.shape[1], w2), lambda i, h, j: (i, 0, h)),
            pl.BlockSpec((1, vlt.shape[1], w2, vlt.shape[3]), lambda i, h, j: (i, 0, h, 0)),
        ]
        args += [kl, vlt]
        body = _attn_kernel_full
    return pl.pallas_call(
        body,
        grid=(b, MLA_HEADS // HEADS_PER_STEP, s // bq),
        in_specs=in_specs,
        out_specs=pl.BlockSpec((1, bq, HEADS_PER_STEP * V_DIM), lambda i, h, j: (i, j, h)),
        out_shape=jax.ShapeDtypeStruct((b, s, MLA_WIDTH), BF16),
        scratch_shapes=scratch,
        compiler_params=_cparams(("parallel", "parallel", "arbitrary")),
        name="attn",
    )(*args)


SUBLANES = 8


def _fill_ext(ext_ref, sh_ref, main_ref, left_ref, right_ref, tm):
    j = pl.program_id(1)
    last = pl.num_programs(1) - 1
    ext_ref[0:HALO, :] = jnp.where(j > 0, left_ref[0], 0.0)
    ext_ref[HALO:HALO + tm, :] = main_ref[0]
    ext_ref[HALO + tm:HALO + tm + HALO, :] = jnp.where(j < last, right_ref[0], 0.0)
    n = sh_ref.shape[1]
    for b in range(1, SUBLANES):
        sh_ref[b - 1, :, :] = ext_ref[b:b + n, :]


def _mix_kernel(att_ref, up_ref, upl_ref, upr_ref, uc_ref, ucl_ref, ucr_ref, z_ref, mod_ref,
                poolw_ref, pools_ref, dw_ref, cb_ref, clg_ref, clb_ref, cpw_ref, wout_ref,
                l1g_ref, l1b_ref, z1_ref, h_ref, hp_ref, ext_ref, sh_ref, *, tm, seq, alpha):
    j = pl.program_id(1)

    def shifted(d):
        a, b = divmod(HALO + d, SUBLANES)
        if b == 0:
            return ext_ref[a * SUBLANES:a * SUBLANES + tm, :]
        return sh_ref[b - 1, a * SUBLANES:a * SUBLANES + tm, :]

    _fill_ext(ext_ref, sh_ref, up_ref, upl_ref, upr_ref, tm)
    x = up_ref[0]
    lane = lax.broadcasted_iota(jnp.int32, (tm, POOL_WIDTH), 1)
    t = lax.broadcasted_iota(jnp.int32, (tm, POOL_WIDTH), 0) + j * tm

    run = x + shifted(-1)
    sums = [run]
    for lo in (2, 4, 8):
        for d in list(range(-lo, -lo // 2)) + list(range(lo // 2, lo)):
            run = run + shifted(d)
        sums.append(run)
    wsum = jnp.where(lane < POOL_GROUP, sums[0],
                     jnp.where(lane < 2 * POOL_GROUP, sums[1],
                               jnp.where(lane < 3 * POOL_GROUP, sums[2], sums[3])))
    lo = jnp.where(lane < POOL_GROUP, 1,
                   jnp.where(lane < 2 * POOL_GROUP, 2, jnp.where(lane < 3 * POOL_GROUP, 4, 8)))
    cnt = jnp.minimum(t + lo, seq) - jnp.maximum(t - lo, 0)
    diff = wsum / cnt.astype(F32) - x
    pooled = _dot(diff.astype(BF16), poolw_ref[...]) * pools_ref[...]

    _fill_ext(ext_ref, sh_ref, uc_ref, ucl_ref, ucr_ref, tm)
    acc = jnp.zeros((tm, CONV_WIDTH), F32) + cb_ref[...]
    for k in range(CONV_K):
        acc = acc + dw_ref[k:k + 1, :] * shifted(k - CONV_K // 2)
    y = _ln(acc) * clg_ref[...] + clb_ref[...]
    y = y * _sigmoid(y)
    conv = _dot(y.astype(BF16), cpw_ref[...])

    cat = jnp.concatenate([att_ref[0], pooled.astype(BF16), conv.astype(BF16)], axis=1)
    out = _dot(cat, wout_ref[...])
    gate = mod_ref[0, 2:3, :]
    z1 = _ln(alpha * z_ref[0] + gate * out) * l1g_ref[...] + l1b_ref[...]
    z1_ref[0] = z1
    h = _ln(z1) * (1.0 + mod_ref[0, 4:5, :]) + mod_ref[0, 3:4, :]
    h_ref[0] = h.astype(BF16)
    hp_ref[0] = _pack_rows(h)


def _mix(att, up, uc, z, mod, w, tm, alpha):
    b, s, d = z.shape
    hb = tm // HALO
    nhb = s // HALO
    full = lambda shape: pl.BlockSpec(shape, lambda i, j: (0,) * len(shape))
    tok = lambda width: pl.BlockSpec((1, tm, width), lambda i, j: (i, j, 0))
    left = lambda width: pl.BlockSpec((1, HALO, width),
                                      lambda i, j: (i, jnp.maximum(j * hb - 1, 0), 0))
    right = lambda width: pl.BlockSpec((1, HALO, width),
                                       lambda i, j: (i, jnp.minimum((j + 1) * hb, nhb - 1), 0))
    body = functools.partial(_mix_kernel, tm=tm, seq=s, alpha=alpha)
    return pl.pallas_call(
        body,
        grid=(b, s // tm),
        in_specs=[
            tok(MLA_WIDTH),
            tok(POOL_WIDTH), left(POOL_WIDTH), right(POOL_WIDTH),
            tok(CONV_WIDTH), left(CONV_WIDTH), right(CONV_WIDTH),
            tok(d),
            pl.BlockSpec((1, 6, d), lambda i, j: (i, 0, 0)),
            full((POOL_WIDTH, POOL_WIDTH)), full((1, POOL_WIDTH)),
            full((CONV_K + 1, CONV_WIDTH)), full((1, CONV_WIDTH)), full((1, CONV_WIDTH)),
            full((1, CONV_WIDTH)), full((CONV_WIDTH, CONV_WIDTH)),
            full((d, d)), full((1, d)), full((1, d)),
        ],
        out_specs=[tok(d), tok(d), tok(d // 2)],
        out_shape=[jax.ShapeDtypeStruct((b, s, d), F32), jax.ShapeDtypeStruct((b, s, d), BF16),
                   jax.ShapeDtypeStruct((b, s, d // 2), jnp.uint32)],
        scratch_shapes=[pltpu.VMEM((tm + 2 * HALO, POOL_WIDTH), F32),
                        pltpu.VMEM((SUBLANES - 1, tm + 2 * HALO - SUBLANES, POOL_WIDTH), F32)],
        compiler_params=_cparams(("parallel", "parallel")),
        name="mix",
    )(att, up, up, up, uc, uc, uc, z, mod, w["pool_w"], w["pool_s"], w["conv_dw"], w["conv_b"],
      w["conv_ln_g"], w["conv_ln_b"], w["conv_pw"], w["w_out"], w["ln1_g"], w["ln1_b"])


def _select_experts(h, rw, rb):
    tm = h.shape[0]
    logits = _dot(h, rw)
    scores = _sigmoid(logits.T[0:N_EXPERTS, :])
    sel = scores + rb
    neg = -jnp.inf

    row8 = lax.broadcasted_iota(jnp.int32, (GROUP_SIZE, tm), 0)
    gscore = []
    for g in range(N_GROUPS):
        xg = sel[g * GROUP_SIZE:(g + 1) * GROUP_SIZE, :]
        m1 = jnp.max(xg, axis=0, keepdims=True)
        i1 = jnp.min(jnp.where(xg == m1, row8, GROUP_SIZE), axis=0, keepdims=True)
        m2 = jnp.max(jnp.where(row8 == i1, neg, xg), axis=0, keepdims=True)
        gscore.append(m1 + m2)
    masked = []
    for g in range(N_GROUPS):
        ahead = jnp.zeros((1, tm), F32)
        for g2 in range(N_GROUPS):
            if g2 < g:
                ahead = ahead + jnp.where(gscore[g2] >= gscore[g], 1.0, 0.0)
            elif g2 > g:
                ahead = ahead + jnp.where(gscore[g2] > gscore[g], 1.0, 0.0)
        keep = ahead < TOPK_GROUPS
        masked.append(jnp.where(keep, sel[g * GROUP_SIZE:(g + 1) * GROUP_SIZE, :], neg))
    masked = jnp.concatenate(masked, axis=0)

    row = lax.broadcasted_iota(jnp.int32, (N_EXPERTS, tm), 0)
    chosen = jnp.zeros((N_EXPERTS, tm), F32)
    hits = []
    for _ in range(TOP_K):
        m = jnp.max(masked, axis=0, keepdims=True)
        idx = jnp.min(jnp.where(masked == m, row, N_EXPERTS), axis=0, keepdims=True)
        hit = row == idx
        hits.append(hit)
        chosen = jnp.where(hit, 1.0, chosen)
        masked = jnp.where(hit, neg, masked)
    w = chosen * scores
    gates = w / jnp.sum(w, axis=0, keepdims=True) * ROUTED_SCALE
    return gates, chosen, hits


def _lanes_to_rows(rows):
    tm = rows[0].shape[1]
    pad = jnp.zeros((LANES - len(rows), tm), F32)
    return jnp.concatenate(rows + [pad], axis=0).T


def _route_kernel(h_ref, rw_ref, rb_ref, g_ref):
    gates, _, _ = _select_experts(h_ref[...], rw_ref[...], rb_ref[...])
    tm = gates.shape[1]
    pad = jnp.zeros((LANES - N_EXPERTS, tm), F32)
    g_ref[...] = jnp.concatenate([gates, pad], axis=0).T


def _route_sorted_kernel(h_ref, rw_ref, rb_ref, g4_ref, eidx_ref, rank_ref, cnt_ref, base_ref):
    @pl.when(pl.program_id(0) == 0)
    def _():
        base_ref[...] = jnp.zeros_like(base_ref)

    gates, chosen, hits = _select_experts(h_ref[...], rw_ref[...], rb_ref[...])
    tm = gates.shape[1]
    ii = lax.broadcasted_iota(jnp.int32, (tm, tm), 0)
    jj = lax.broadcasted_iota(jnp.int32, (tm, tm), 1)
    upper = jnp.where(ii <= jj, 1.0, 0.0).astype(BF16)
    prefix = _dot(chosen.astype(BF16), upper)
    rank_full = base_ref[...] + prefix - 1.0
    row = lax.broadcasted_iota(jnp.int32, (N_EXPERTS, tm), 0).astype(F32)
    pick = lambda hit, val: jnp.sum(jnp.where(hit, val, 0.0), axis=0, keepdims=True)
    g4_ref[...] = _lanes_to_rows([pick(hit, gates) for hit in hits])
    eidx_ref[...] = jnp.concatenate([pick(hit, row) for hit in hits], axis=0).astype(jnp.int32)
    rank_ref[...] = jnp.concatenate([pick(hit, rank_full) for hit in hits], axis=0).astype(jnp.int32)
    base_ref[...] += jnp.sum(chosen, axis=1, keepdims=True)
    cnt_ref[...] = jnp.broadcast_to(base_ref[...], cnt_ref.shape)


def _route(h, rw, rb, tm):
    t, d = h.shape
    return pl.pallas_call(
        _route_kernel,
        grid=(t // tm,),
        in_specs=[
            pl.BlockSpec((tm, d), lambda i: (i, 0)),
            pl.BlockSpec((d, LANES), lambda i: (0, 0)),
            pl.BlockSpec((N_EXPERTS, 1), lambda i: (0, 0)),
        ],
        out_specs=pl.BlockSpec((tm, LANES), lambda i: (i, 0)),
        out_shape=jax.ShapeDtypeStruct((t, LANES), F32),
        compiler_params=_cparams(("parallel",)),
        name="route",
    )(h, rw, rb)


def _route_sorted(h, rw, rb, tm):
    t, d = h.shape
    pick = pl.BlockSpec((TOP_K, tm), lambda i: (0, i))
    return pl.pallas_call(
        _route_sorted_kernel,
        grid=(t // tm,),
        in_specs=[
            pl.BlockSpec((tm, d), lambda i: (i, 0)),
            pl.BlockSpec((d, LANES), lambda i: (0, 0)),
            pl.BlockSpec((N_EXPERTS, 1), lambda i: (0, 0)),
        ],
        out_specs=[pl.BlockSpec((tm, LANES), lambda i: (i, 0)), pick, pick,
                   pl.BlockSpec((N_EXPERTS, LANES), lambda i: (0, 0))],
        out_shape=[jax.ShapeDtypeStruct((t, LANES), F32),
                   jax.ShapeDtypeStruct((TOP_K, t), jnp.int32),
                   jax.ShapeDtypeStruct((TOP_K, t), jnp.int32),
                   jax.ShapeDtypeStruct((N_EXPERTS, LANES), F32)],
        scratch_shapes=[pltpu.VMEM((N_EXPERTS, 1), F32)],
        compiler_params=_cparams(("arbitrary",)),
        name="route_sorted",
    )(h, rw, rb)


def _swiglu_hidden(x, wg, wu):
    hg = _dot(x, wg)
    return hg * _sigmoid(hg) * _dot(x, wu)


ROW_TILE = 512


def _pack_rows(x):
    n = x.shape[1] // 2
    bits = lax.bitcast_convert_type(x.astype(BF16).astype(F32), jnp.uint32)
    return bits[:, :n] | (bits[:, n:] >> 16)


def _unpack_rows(w):
    hi = lax.bitcast_convert_type(w & jnp.uint32(0xFFFF0000), F32)
    lo = lax.bitcast_convert_type(w << 16, F32)
    return jnp.concatenate([hi, lo], axis=1)


def _row_copies(pos_ref, tm, make_copy):
    def body(t, carry):
        for k in range(TOP_K):
            make_copy(k, t, pos_ref[0, 0, k * tm + t]).start()
        return carry
    lax.fori_loop(0, tm, body, 0, unroll=8)


def _dispatch_kernel(pos_ref, hp_ref, xs_in_ref, xs_ref, sem):
    del xs_in_ref
    tm = hp_ref.shape[0]
    _row_copies(pos_ref, tm, lambda k, t, p: pltpu.make_async_copy(
        hp_ref.at[pl.ds(t, 1), :], xs_ref.at[pl.ds(p, 1), :], sem))
    n = TOP_K * tm
    pltpu.make_async_copy(xs_ref.at[pl.ds(0, n), :], xs_ref.at[pl.ds(0, n), :], sem).wait()


def _dispatch(pos_tiles, hp, n_rows, tm):
    t, half = hp.shape
    xs0 = jnp.zeros((n_rows, half), jnp.uint32)
    return pl.pallas_call(
        _dispatch_kernel,
        grid=(t // tm,),
        in_specs=[
            pl.BlockSpec((1, 1, TOP_K * tm), lambda i: (i, 0, 0), memory_space=pltpu.SMEM),
            pl.BlockSpec((tm, half), lambda i: (i, 0)),
            pl.BlockSpec(memory_space=pl.ANY),
        ],
        out_specs=pl.BlockSpec(memory_space=pl.ANY),
        out_shape=jax.ShapeDtypeStruct((n_rows, half), jnp.uint32),
        scratch_shapes=[pltpu.SemaphoreType.DMA(())],
        input_output_aliases={2: 0},
        compiler_params=_cparams(("arbitrary",)),
        name="dispatch",
    )(pos_tiles, hp, xs0)


def _experts_kernel(te_ref, nu_ref, xs_ref, wg_ref, wu_ref, wd_ref, ys_ref):
    del te_ref
    i = pl.program_id(0)

    @pl.when(i < nu_ref[0])
    def _():
        x = _unpack_rows(xs_ref[...]).astype(BF16)
        a = _swiglu_hidden(x, wg_ref[0, 0], wu_ref[0, 0])
        ys_ref[...] = _pack_rows(_dot(a.astype(BF16), wd_ref[0, 0]))

    @pl.when(i >= nu_ref[0])
    def _():
        ys_ref[...] = jnp.zeros_like(ys_ref)


def _experts(tile_expert, n_used, xs, l, ew):
    wg, wu, wd = ew
    d, de = wg.shape[2], wg.shape[3]
    n_rows, half = xs.shape
    rows = pl.BlockSpec((ROW_TILE, half), lambda i, te, nu: (i, 0))
    return pl.pallas_call(
        _experts_kernel,
        grid_spec=pltpu.PrefetchScalarGridSpec(
            num_scalar_prefetch=2,
            grid=(n_rows // ROW_TILE,),
            in_specs=[
                rows,
                pl.BlockSpec((1, 1, d, de), lambda i, te, nu: (l, te[i], 0, 0)),
                pl.BlockSpec((1, 1, d, de), lambda i, te, nu: (l, te[i], 0, 0)),
                pl.BlockSpec((1, 1, de, d), lambda i, te, nu: (l, te[i], 0, 0)),
            ],
            out_specs=rows,
        ),
        out_shape=jax.ShapeDtypeStruct((n_rows, half), jnp.uint32),
        compiler_params=_cparams(("arbitrary",)),
        name="experts",
    )(tile_expert, n_used, xs, wg, wu, wd)


def _combine_kernel(pos_ref, g4_ref, h_ref, swg_ref, swu_ref, swd_ref, z1_ref, mod_ref, l2g_ref,
                    l2b_ref, ys_ref, o_ref, buf_ref, sem, *, alpha):
    tm = h_ref.shape[0]
    _row_copies(pos_ref, tm, lambda k, t, p: pltpu.make_async_copy(
        ys_ref.at[pl.ds(p, 1), :], buf_ref.at[k, pl.ds(t, 1), :], sem))
    a = _swiglu_hidden(h_ref[...], swg_ref[...], swu_ref[...])
    acc = _dot(a.astype(BF16), swd_ref[...])
    for k in range(TOP_K):
        pltpu.make_async_copy(ys_ref.at[pl.ds(0, tm), :], buf_ref.at[k], sem).wait()
    for k in range(TOP_K):
        acc = acc + g4_ref[:, k:k + 1] * _unpack_rows(buf_ref[k])
    gate = mod_ref[0, 5:6, :]
    o_ref[...] = _ln(alpha * z1_ref[...] + gate * acc) * l2g_ref[...] + l2b_ref[...]


def _combine(pos_tiles, g4, h, ys, w, z1, mod, tm, tiles_per_batch, alpha):
    t, d = h.shape
    de = w["sh_wg"].shape[1]
    half = ys.shape[1]
    full = lambda shape: pl.BlockSpec(shape, lambda i: (0,) * len(shape))
    body = functools.partial(_combine_kernel, alpha=alpha)
    return pl.pallas_call(
        body,
        grid=(t // tm,),
        in_specs=[
            pl.BlockSpec((1, 1, TOP_K * tm), lambda i: (i, 0, 0), memory_space=pltpu.SMEM),
            pl.BlockSpec((tm, LANES), lambda i: (i, 0)),
            pl.BlockSpec((tm, d), lambda i: (i, 0)),
            full((d, de)), full((d, de)), full((de, d)),
            pl.BlockSpec((tm, d), lambda i: (i, 0)),
            pl.BlockSpec((1, 6, d), lambda i: (i // tiles_per_batch, 0, 0)),
            full((1, d)), full((1, d)),
            pl.BlockSpec(memory_space=pl.ANY),
        ],
        out_specs=pl.BlockSpec((tm, d), lambda i: (i, 0)),
        out_shape=jax.ShapeDtypeStruct((t, d), F32),
        scratch_shapes=[pltpu.VMEM((TOP_K, tm, half), jnp.uint32), pltpu.SemaphoreType.DMA(())],
        compiler_params=_cparams(("arbitrary",)),
        name="combine",
    )(pos_tiles, g4, h, w["sh_wg"], w["sh_wu"], w["sh_wd"], z1, mod, w["ln2_g"], w["ln2_b"], ys)


def _moe_sorted(h, hp, l, ew, w, z1, mod, tm, tiles_per_batch, alpha):
    t = h.shape[0]
    g4, eidx, rank, cnt = _route_sorted(h, w["router_w"], w["router_b"], tm)
    counts = cnt[:, 0].astype(jnp.int32)
    sizes = (counts + ROW_TILE - 1) // ROW_TILE * ROW_TILE
    ends = jnp.cumsum(sizes)
    starts = ends - sizes
    n_tiles = t * TOP_K // ROW_TILE + N_EXPERTS
    n_used = (ends[-1] // ROW_TILE).reshape(1)
    tile_expert = jnp.minimum(
        jnp.sum(jnp.arange(n_tiles)[:, None] >= (ends // ROW_TILE)[None, :], axis=1), N_EXPERTS - 1
    ).astype(jnp.int32)
    onehot = eidx[:, :, None] == jnp.arange(N_EXPERTS)[None, None, :]
    pos = rank + jnp.sum(jnp.where(onehot, starts[None, None, :], 0), axis=-1)
    pos_tiles = pos.reshape(TOP_K, t // tm, tm).transpose(1, 0, 2).reshape(t // tm, 1, TOP_K * tm)
    xs = _dispatch(pos_tiles, hp, n_tiles * ROW_TILE, tm)
    ys = _experts(tile_expert, n_used, xs, l, ew)
    return _combine(pos_tiles, g4, h, ys, w, z1, mod, tm, tiles_per_batch, alpha)


def _moe_kernel(h_ref, g_ref, wg_ref, wu_ref, wd_ref, swg_ref, swu_ref, swd_ref, z1_ref, mod_ref,
                l2g_ref, l2b_ref, o_ref, acc_ref, *, alpha):
    e = pl.program_id(1)
    x = h_ref[...]

    @pl.when(e == 0)
    def _():
        a = _swiglu_hidden(x, swg_ref[...], swu_ref[...])
        acc_ref[...] = _dot(a.astype(BF16), swd_ref[...])

    lane = lax.broadcasted_iota(jnp.int32, g_ref.shape, 1)
    gcol = jnp.sum(jnp.where(lane == e, g_ref[...], 0.0), axis=1, keepdims=True)
    a = _swiglu_hidden(x, wg_ref[0, 0], wu_ref[0, 0]) * gcol
    acc_ref[...] += _dot(a.astype(BF16), wd_ref[0, 0])

    @pl.when(e == pl.num_programs(1) - 1)
    def _():
        gate = mod_ref[0, 5:6, :]
        o_ref[...] = _ln(alpha * z1_ref[...] + gate * acc_ref[...]) * l2g_ref[...] + l2b_ref[...]


def _moe(h, gates, l, ew, w, z1, mod, tm, tiles_per_batch, alpha):
    t, d = h.shape
    wg, wu, wd = ew
    ne, de = wg.shape[1], wg.shape[3]
    body = functools.partial(_moe_kernel, alpha=alpha)
    return pl.pallas_call(
        body,
        grid=(t // tm, ne),
        in_specs=[
            pl.BlockSpec((tm, d), lambda i, e: (i, 0)),
            pl.BlockSpec((tm, LANES), lambda i, e: (i, 0)),
            pl.BlockSpec((1, 1, d, de), lambda i, e: (l, e, 0, 0)),
            pl.BlockSpec((1, 1, d, de), lambda i, e: (l, e, 0, 0)),
            pl.BlockSpec((1, 1, de, d), lambda i, e: (l, e, 0, 0)),
            pl.BlockSpec((d, de), lambda i, e: (0, 0)),
            pl.BlockSpec((d, de), lambda i, e: (0, 0)),
            pl.BlockSpec((de, d), lambda i, e: (0, 0)),
            pl.BlockSpec((tm, d), lambda i, e: (i, 0)),
            pl.BlockSpec((1, 6, d), lambda i, e: (i // tiles_per_batch, 0, 0)),
            pl.BlockSpec((1, d), lambda i, e: (0, 0)),
            pl.BlockSpec((1, d), lambda i, e: (0, 0)),
        ],
        out_specs=pl.BlockSpec((tm, d), lambda i, e: (i, 0)),
        out_shape=jax.ShapeDtypeStruct((t, d), F32),
        scratch_shapes=[pltpu.VMEM((tm, d), F32)],
        compiler_params=_cparams(("parallel", "arbitrary")),
        name="moe",
    )(h, gates, wg, wu, wd, w["sh_wg"], w["sh_wu"], w["sh_wd"], z1, mod, w["ln2_g"], w["ln2_b"])


def _rope_swap(r):
    r4 = r.reshape(r.shape[:-1] + (2, 2, ROPE_AXIS // 2))
    return jnp.stack([-r4[..., 1, :], r4[..., 0, :]], axis=-2).reshape(r.shape)


def _prep_layer(l, w_in, q_norm_g, w_uq, kv_norm_g, w_ukv, pool_w, pool_scale, conv_dw, conv_b,
                conv_ln_g, conv_ln_b, conv_pw, w_out, ln1_g, ln1_b, router_w, router_bias,
                exp_wg, exp_wu, exp_wd, sh_wg, sh_wu, sh_wd, ln2_g, ln2_b):
    d = w_in.shape[1]
    wi = w_in[l]
    w_in_p = jnp.concatenate([
        wi[:, 0:OFF_KR], wi[:, OFF_KR:OFF_POOL], jnp.zeros((d, PC_POOL - PC_KR - QK_ROPE), F32),
        wi[:, OFF_POOL:]], axis=1).astype(BF16)

    uq = w_uq[l].reshape(Q_LORA, MLA_HEADS, QK_NOPE + QK_ROPE)
    nope, rope = uq[..., :QK_NOPE], uq[..., QK_NOPE:]
    wqat = jnp.concatenate([nope, rope, rope], axis=-1).reshape(Q_LORA, QK_WIDTH).T.astype(BF16)
    wqbt = jnp.concatenate([jnp.zeros_like(nope), jnp.zeros_like(rope), _rope_swap(rope)],
                           axis=-1).reshape(Q_LORA, QK_WIDTH).T.astype(BF16)

    ukv = w_ukv[l].reshape(KV_LORA, MLA_HEADS, QK_NOPE + V_DIM)
    kn, vv = ukv[..., :QK_NOPE], ukv[..., QK_NOPE:]
    z32 = jnp.zeros((KV_LORA, MLA_HEADS, QK_ROPE), F32)
    top = jnp.concatenate([kn, z32, z32], axis=-1)
    eye = jnp.broadcast_to(jnp.eye(QK_ROPE, dtype=F32)[:, None, :], (QK_ROPE, MLA_HEADS, QK_ROPE))
    zr = jnp.zeros((QK_ROPE, MLA_HEADS, QK_ROPE), F32)
    zn = jnp.zeros((QK_ROPE, MLA_HEADS, QK_NOPE), F32)
    padrows = jnp.zeros((KV_LORA - QK_ROPE, MLA_HEADS, HEAD_PAD), F32)

    def kmat(kr_rows):
        return jnp.concatenate([top, kr_rows, padrows], axis=0).reshape(2 * KV_LORA, QK_WIDTH).astype(BF16)

    wka_lat = kmat(jnp.concatenate([zn, zr, eye], axis=-1))
    wkb_lat = kmat(jnp.concatenate([zn, zr, _rope_swap(eye)], axis=-1))
    wka_ctx = kmat(jnp.concatenate([zn, eye, zr], axis=-1))
    wkb_ctx = jnp.zeros((2 * KV_LORA, QK_WIDTH), BF16)

    zv = jnp.zeros_like(vv)
    even = (jnp.arange(MLA_HEADS) % 2 == 0)[None, :, None]
    wv = jnp.where(even, jnp.concatenate([vv, zv], -1), jnp.concatenate([zv, vv], -1))
    wvt = wv.reshape(KV_LORA, QK_WIDTH).T.astype(BF16)
    lane = jnp.arange(QK_WIDTH) % (2 * HEAD_PAD)
    vbt = ((lane == V_DIM) | (lane == HEAD_PAD)).astype(F32)[:, None]

    pw = jnp.zeros((POOL_WIDTH, POOL_WIDTH), F32)
    for gi in range(len(POOL_WINDOWS)):
        pw = pw.at[gi * POOL_GROUP:(gi + 1) * POOL_GROUP, gi * POOL_GROUP:(gi + 1) * POOL_GROUP].set(pool_w[l, gi])

    common = dict(
        w_in=w_in_p, q_g=q_norm_g[l][None], kv_g=kv_norm_g[l][None], wqat=wqat, wqbt=wqbt, wvt=wvt, vbt=vbt,
        pool_w=pw.astype(BF16), pool_s=pool_scale[l][None],
        conv_dw=jnp.concatenate([conv_dw[l], jnp.zeros((1, CONV_WIDTH), F32)], axis=0),
        conv_b=conv_b[l][None], conv_ln_g=conv_ln_g[l][None], conv_ln_b=conv_ln_b[l][None],
        conv_pw=conv_pw[l].astype(BF16), w_out=w_out[l].astype(BF16),
        ln1_g=ln1_g[l][None], ln1_b=ln1_b[l][None],
        router_w=jnp.pad(router_w[l], ((0, 0), (0, LANES - N_EXPERTS))).astype(BF16),
        router_b=router_bias[l][:, None],
        sh_wg=sh_wg[l].astype(BF16), sh_wu=sh_wu[l].astype(BF16), sh_wd=sh_wd[l].astype(BF16),
        ln2_g=ln2_g[l][None], ln2_b=ln2_b[l][None],
    )
    return dict(common, wka=wka_lat, wkb=wkb_lat), dict(common, wka=wka_ctx, wkb=wkb_ctx)


def _rope_tables(n_lat, n_ctx):
    t = jnp.arange(n_lat)
    inv = ROPE_BASE ** (-jnp.arange(0, ROPE_AXIS, 2, dtype=F32) / ROPE_AXIS)
    ang_r = (t // GRID_W).astype(F32)[:, None] * inv
    ang_c = (t % GRID_W).astype(F32)[:, None] * inv
    cos = jnp.concatenate([jnp.cos(ang_r), jnp.cos(ang_r), jnp.cos(ang_c), jnp.cos(ang_c)], axis=1)
    sin = jnp.concatenate([jnp.sin(ang_r), jnp.sin(ang_r), jnp.sin(ang_c), jnp.sin(ang_c)], axis=1)
    ck = jnp.concatenate([jnp.ones((n_lat, HEAD_PAD - QK_ROPE), F32), cos], axis=1)
    sk = jnp.concatenate([jnp.zeros((n_lat, HEAD_PAD - QK_ROPE), F32), sin], axis=1)
    qs = ATTN_SCALE * LOG2E
    lat = ((ck * qs).T, (sk * qs).T, ck, sk)
    one = jnp.ones((n_ctx, HEAD_PAD), F32)
    zero = jnp.zeros((n_ctx, HEAD_PAD), F32)
    ctx = ((one * qs).T, zero.T, one, zero)
    return lat, ctx


def _sublayers(z, mod, l, ew, w, att, up, uc, tm, tm_moe, alpha, sorted_moe):
    b, s, d = z.shape
    z1, h, hp = _mix(att, up, uc, z, mod, w, tm, alpha)
    h2, z2 = h.reshape(b * s, d), z1.reshape(b * s, d)
    if sorted_moe:
        out = _moe_sorted(h2, hp.reshape(b * s, d // 2), l, ew, w, z2, mod, tm, s // tm, alpha)
    else:
        gates = _route(h2, w["router_w"], w["router_b"], tm)
        out = _moe(h2, gates, l, ew, w, z2, mod, tm_moe, s // tm_moe, alpha)
    return out.reshape(b, s, d)


def kernel(x, c, ctx, c_ctx, ada_w, ada_b, w_in, q_norm_g, w_uq, kv_norm_g, w_ukv, pool_w, pool_scale, conv_dw, conv_b, conv_ln_g, conv_ln_b, conv_pw, w_out, ln1_g, ln1_b, router_w, router_bias, exp_wg, exp_wu, exp_wd, sh_wg, sh_wu, sh_wd, ln2_g, ln2_b):
    b, s, d = x.shape
    n_ctx = ctx.shape[1]
    depth = ada_w.shape[0]
    alpha = (2 * depth) ** 0.25
    assert b + 1 <= 8 and s % GRID_W == 0

    tm_l = min(512, s)
    tm_c = min(256, n_ctx)
    tm_moe_l = min(1024, s)
    tm_moe_c = n_ctx
    bq = min(1024, s)

    cond = jnp.concatenate([c, c_ctx[None], jnp.zeros((8 - b - 1, d), F32)], axis=0)
    mods = _ada(cond, ada_w, ada_b)
    tab_l, tab_c = _rope_tables(s, n_ctx)
    ew = (exp_wg.astype(BF16), exp_wu.astype(BF16), exp_wd.astype(BF16))

    zl, zc = x, ctx
    for l in range(depth):
        last = l == depth - 1
        w_l, w_c = _prep_layer(l, w_in, q_norm_g, w_uq, kv_norm_g, w_ukv, pool_w, pool_scale,
                               conv_dw, conv_b, conv_ln_g, conv_ln_b, conv_pw, w_out, ln1_g, ln1_b,
                               router_w, router_bias, exp_wg, exp_wu, exp_wd, sh_wg, sh_wu, sh_wd,
                               ln2_g, ln2_b)
        mod_l = mods[l, :b].reshape(b, 6, d)
        mod_c = jnp.broadcast_to(mods[l, b].reshape(1, 6, d), (b, 6, d))

        q_l, k_l, v_l, up_l, uc_l = _proj(zl, mod_l, w_l, tab_l, tm_l)
        q_c, k_c, v_c, up_c, uc_c = _proj(zc, mod_c, w_c, tab_c, tm_c)
        att_l = _attn(q_l, k_c, v_c, k_l, v_l, bq)
        zl = _sublayers(zl, mod_l, l, ew, w_l, att_l, up_l, uc_l, tm_l, tm_moe_l, alpha, True)
        if not last:
            att_c = _attn(q_c, k_c, v_c, None, None, tm_c)
            zc = _sublayers(zc, mod_c, l, ew, w_c, att_c, up_c, uc_c, tm_c, tm_moe_c, alpha, False)
    return zl
```

```python
import functools
import math

import jax
import jax.numpy as jnp
from jax import lax
from jax.experimental import pallas as pl
from jax.experimental.pallas import tpu as pltpu

GRID_W = 64
MLA_HEADS = 8
QK_NOPE = 64
QK_ROPE = 32
V_DIM = 64
Q_LORA = 256
KV_LORA = 128
ROPE_AXIS = QK_ROPE // 2
ROPE_BASE = 10000.0
ATTN_SCALE = (QK_NOPE + QK_ROPE) ** -0.5
POOL_WINDOWS = (2, 4, 8, 16)
POOL_GROUP = 64
POOL_WIDTH = POOL_GROUP * len(POOL_WINDOWS)
CONV_WIDTH = 256
CONV_K = 31
MLA_WIDTH = MLA_HEADS * V_DIM
OFF_KV = Q_LORA
OFF_KR = OFF_KV + KV_LORA
OFF_POOL = OFF_KR + QK_ROPE
OFF_CONV = OFF_POOL + POOL_WIDTH
N_EXPERTS = 32
TOP_K = 4
N_GROUPS = 4
TOPK_GROUPS = 2
GROUP_SIZE = N_EXPERTS // N_GROUPS
ROUTED_SCALE = 2.5
EPS = 1e-6

LANES = 128
HEAD_PAD = LANES
QK_WIDTH = MLA_HEADS * HEAD_PAD
HALO = 16
P_COLS = 1280
PC_Q, PC_KV, PC_KR, PC_POOL, PC_CA, PC_CG = 0, 256, 384, 512, 768, 1024
VMEM_LIMIT = 48 * 1024 * 1024
LOG2E = math.log2(math.e)

F32 = jnp.float32
BF16 = jnp.bfloat16


def _cparams(sem, flags=None):
    return pltpu.CompilerParams(dimension_semantics=sem, vmem_limit_bytes=VMEM_LIMIT, flags=flags)


def _ln(x):
    mu = jnp.mean(x, axis=-1, keepdims=True)
    xc = x - mu
    var = jnp.mean(xc * xc, axis=-1, keepdims=True)
    return xc * lax.rsqrt(var + EPS)


def _rms(x):
    return x * lax.rsqrt(jnp.mean(x * x, axis=-1, keepdims=True) + EPS)


def _sigmoid(x):
    return 1.0 / (1.0 + jnp.exp(-x))


def _dot(a, b):
    return jnp.dot(a, b, preferred_element_type=F32)


def _dot_nt(a, b):
    return lax.dot_general(a, b, (((1,), (1,)), ((), ())), preferred_element_type=F32)


def _ada_kernel(c_ref, w_ref, b_ref, o_ref):
    x = c_ref[...]
    x = x * _sigmoid(x)
    o_ref[0] = _dot(x.astype(BF16), w_ref[0].astype(BF16)) + b_ref[0]


def _ada(cond, ada_w, ada_b):
    depth, d, n = ada_w.shape
    tn = 1536
    return pl.pallas_call(
        _ada_kernel,
        grid=(depth, n // tn),
        in_specs=[
            pl.BlockSpec((8, d), lambda l, j: (0, 0)),
            pl.BlockSpec((1, d, tn), lambda l, j: (l, 0, j)),
            pl.BlockSpec((1, 1, tn), lambda l, j: (l, 0, j)),
        ],
        out_specs=pl.BlockSpec((1, 8, tn), lambda l, j: (l, 0, j)),
        out_shape=jax.ShapeDtypeStruct((depth, 8, n), F32),
        compiler_params=_cparams(("parallel", "parallel")),
        name="ada",
    )(cond, ada_w, ada_b.reshape(depth, 1, n))


def _proj_kernel(z_ref, mod_ref, win_ref, qg_ref, kvg_ref, wqat_ref, wqbt_ref, wka_ref, wkb_ref,
                 wvt_ref, vbt_ref, cqt_ref, sqt_ref, ck_ref, sk_ref,
                 qt_ref, k_ref, vt_ref, up_ref, uc_ref):
    z = z_ref[0]
    shift = mod_ref[0, 0:1, :]
    scale = mod_ref[0, 1:2, :]
    h = _ln(z) * (1.0 + scale) + shift
    p = _dot(h.astype(BF16), win_ref[...])
    qn = (_rms(p[:, PC_Q:PC_KV]) * qg_ref[...]).astype(BF16)
    kvn = (_rms(p[:, PC_KV:PC_KR]) * kvg_ref[...]).astype(BF16)
    xk = jnp.concatenate([kvn, p[:, PC_KR:PC_POOL].astype(BF16)], axis=1)
    qat = _dot_nt(wqat_ref[...], qn)
    qbt = _dot_nt(wqbt_ref[...], qn)
    ka = _dot(xk, wka_ref[...])
    kb = _dot(xk, wkb_ref[...])
    cqt, sqt, ck, sk = cqt_ref[...], sqt_ref[...], ck_ref[...], sk_ref[...]
    for hd in range(MLA_HEADS):
        sl = slice(hd * HEAD_PAD, (hd + 1) * HEAD_PAD)
        qt_ref[0, sl, :] = (qat[sl, :] * cqt + qbt[sl, :] * sqt).astype(BF16)
        k_ref[0, :, sl] = (ka[:, sl] * ck + kb[:, sl] * sk).astype(BF16)
    vt_ref[0, 0] = (_dot_nt(wvt_ref[...], kvn) + vbt_ref[...]).astype(BF16)
    up_ref[0] = p[:, PC_POOL:PC_CA]
    uc_ref[0] = p[:, PC_CA:PC_CG] * _sigmoid(p[:, PC_CG:P_COLS])


def _proj(z, mod, w, tabs, tm):
    b, s, d = z.shape
    full = lambda shape: pl.BlockSpec(shape, lambda i, j: (0,) * len(shape))
    tok = lambda width: pl.BlockSpec((1, tm, width), lambda i, j: (i, j, 0))
    tab = pl.BlockSpec((tm, HEAD_PAD), lambda i, j: (j, 0))
    tab_t = pl.BlockSpec((HEAD_PAD, tm), lambda i, j: (0, j))
    return pl.pallas_call(
        _proj_kernel,
        grid=(b, s // tm),
        in_specs=[
            tok(d),
            pl.BlockSpec((1, 6, d), lambda i, j: (i, 0, 0)),
            full((d, P_COLS)), full((1, Q_LORA)), full((1, KV_LORA)),
            full((QK_WIDTH, Q_LORA)), full((QK_WIDTH, Q_LORA)),
            full((2 * KV_LORA, QK_WIDTH)), full((2 * KV_LORA, QK_WIDTH)),
            full((QK_WIDTH, KV_LORA)), full((QK_WIDTH, 1)),
            tab_t, tab_t, tab, tab,
        ],
        out_specs=[
            pl.BlockSpec((1, QK_WIDTH, tm), lambda i, j: (i, 0, j)),
            tok(QK_WIDTH),
            pl.BlockSpec((1, 1, QK_WIDTH, tm), lambda i, j: (i, j, 0, 0)),
            tok(POOL_WIDTH), tok(CONV_WIDTH)],
        out_shape=[
            jax.ShapeDtypeStruct((b, QK_WIDTH, s), BF16),
            jax.ShapeDtypeStruct((b, s, QK_WIDTH), BF16),
            jax.ShapeDtypeStruct((b, s // tm, QK_WIDTH, tm), BF16),
            jax.ShapeDtypeStruct((b, s, POOL_WIDTH), F32),
            jax.ShapeDtypeStruct((b, s, CONV_WIDTH), F32),
        ],
        compiler_params=_cparams(("parallel", "parallel")),
        name="proj",
    )(z, mod, w["w_in"], w["q_g"], w["kv_g"], w["wqat"], w["wqbt"], w["wka"], w["wkb"],
      w["wvt"], w["vbt"], *tabs)


HEADS_PER_STEP = 2


def _attn_body(qt_ref, kc_ref, vct_ref, kl_ref, vlt_ref, o_ref, s_refs, p_refs):
    heads = range(HEADS_PER_STEP)
    sls = [slice(hh * HEAD_PAD, (hh + 1) * HEAD_PAD) for hh in heads]
    qts = [qt_ref[0, sl, :] for sl in sls]
    ms, accs = [], []
    for qt, sl in zip(qts, sls):
        st = _dot(kc_ref[0, :, sl], qt)
        m = jnp.max(st, axis=0, keepdims=True)
        pt = jnp.exp2(st - m)
        ms.append(m)
        accs.append(_dot(vct_ref[0, 0, sl, :], pt.astype(BF16)))

    if kl_ref is not None:
        n, bk = vlt_ref.shape[1], vlt_ref.shape[3]
        assert n >= 2 and n % 2 == 0

        def score(c, slot):
            r0 = c * bk if isinstance(c, int) else pl.multiple_of(c * bk, bk)
            cms = []
            for hh in heads:
                st = _dot(kl_ref[0, pl.ds(r0, bk), sls[hh]], qts[hh])
                s_refs[slot][hh] = st
                cms.append(jnp.max(st, axis=0, keepdims=True))
            return cms

        def probs(slot, ms, cms):
            m_new = [jnp.maximum(ms[hh], cms[hh]) for hh in heads]
            alphas = [jnp.exp2(ms[hh] - m_new[hh]) for hh in heads]
            for hh in heads:
                p_refs[slot][hh] = jnp.exp2((s_refs[slot][hh] - m_new[hh]).astype(BF16))
            return m_new, alphas

        def accum(c, slot, accs, alphas):
            return [accs[hh] * alphas[hh] + _dot(vlt_ref[0, c, sls[hh], :], p_refs[slot][hh])
                    for hh in heads]

        def half(c, cur, nxt, ms, cms, accs, with_score=True):
            ms, alphas = probs(cur, ms, cms)
            if with_score:
                cms = score(c + 1, nxt)
            accs = accum(c, cur, accs, alphas)
            return ms, cms, accs

        cms = score(0, 0)

        def pair(t, carry):
            ms, cms, accs = [list(x) for x in carry]
            c = 2 * t
            ms, cms, accs = half(c, 0, 1, ms, cms, accs)
            ms, cms, accs = half(c + 1, 1, 0, ms, cms, accs)
            return tuple(tuple(x) for x in (ms, cms, accs))

        carry = tuple(tuple(x) for x in (ms, cms, accs))
        carry = lax.fori_loop(0, (n - 2) // 2, pair, carry)
        ms, cms, accs = [list(x) for x in carry]
        ms, cms, accs = half(n - 2, 0, 1, ms, cms, accs)
        ms, cms, accs = half(n - 1, 1, 0, ms, cms, accs, with_score=False)

    acc_even, acc_odd = accs
    row = lax.broadcasted_iota(jnp.int32, acc_even.shape, 0)
    ot = jnp.where(row < V_DIM, acc_even / acc_even[V_DIM:V_DIM + 1, :], acc_odd / acc_odd[0:1, :])
    o_ref[0] = ot.T.astype(o_ref.dtype)


def _attn_kernel_full(qt_ref, kc_ref, vct_ref, kl_ref, vlt_ref, o_ref, s0_ref, s1_ref, p0_ref, p1_ref):
    _attn_body(qt_ref, kc_ref, vct_ref, kl_ref, vlt_ref, o_ref, (s0_ref, s1_ref), (p0_ref, p1_ref))


def _attn_kernel_ctx(qt_ref, kc_ref, vct_ref, o_ref):
    _attn_body(qt_ref, kc_ref, vct_ref, None, None, o_ref, None, None)


def _attn(qt, kc, vct, kl, vlt, bq):
    b, _, s = qt.shape
    nc = kc.shape[1]
    w2 = HEADS_PER_STEP * HEAD_PAD
    in_specs = [
        pl.BlockSpec((1, w2, bq), lambda i, h, j: (i, h, j)),
        pl.BlockSpec((1, nc, w2), lambda i, h, j: (i, 0, h)),
        pl.BlockSpec((1, 1, w2, nc), lambda i, h, j: (i, 0, h, 0)),
    ]
    args = [qt, kc, vct]
    scratch = []
    if kl is None:
        body = _attn_kernel_ctx
    else:
        bk = vlt.shape[3]
        scratch = [pltpu.VMEM((HEADS_PER_STEP, bk, bq), F32)] * 2 + [pltpu.VMEM((HEADS_PER_STEP, bk, bq), BF16)] * 2
        in_specs += [
            pl.BlockSpec((1, kl.shape[1], w2), lambda i, h, j: (i, 0, h)),
            pl.BlockSpec((1, vlt.shape[1], w2, vlt.shape[3]), lambda i, h, j: (i, 0, h, 0)),
        ]
        args += [kl, vlt]
        body = _attn_kernel_full
    return pl.pallas_call(
        body,
        grid=(b, MLA_HEADS // HEADS_PER_STEP, s // bq),
        in_specs=in_specs,
        out_specs=pl.BlockSpec((1, bq, HEADS_PER_STEP * V_DIM), lambda i, h, j: (i, j, h)),
        out_shape=jax.ShapeDtypeStruct((b, s, MLA_WIDTH), BF16),
        scratch_shapes=scratch,
        compiler_params=_cparams(("parallel", "parallel", "arbitrary")),
        name="attn",
    )(*args)


SUBLANES = 8


def _fill_ext(ext_ref, sh_ref, main_ref, left_ref, right_ref, tm):
    j = pl.program_id(1)
    last = pl.num_programs(1) - 1
    ext_ref[0:HALO, :] = jnp.where(j > 0, left_ref[0], 0.0)
    ext_ref[HALO:HALO + tm, :] = main_ref[0]
    ext_ref[HALO + tm:HALO + tm + HALO, :] = jnp.where(j < last, right_ref[0], 0.0)
    n = sh_ref.shape[1]
    for b in range(1, SUBLANES):
        sh_ref[b - 1, :, :] = ext_ref[b:b + n, :]


def _mix_kernel(att_ref, up_ref, upl_ref, upr_ref, uc_ref, ucl_ref, ucr_ref, z_ref, mod_ref,
                poolw_ref, pools_ref, dw_ref, cb_ref, clg_ref, clb_ref, cpw_ref, wout_ref,
                l1g_ref, l1b_ref, z1_ref, h_ref, hp_ref, ext_ref, sh_ref, *, tm, seq, alpha):
    j = pl.program_id(1)

    def shifted(d):
        a, b = divmod(HALO + d, SUBLANES)
        if b == 0:
            return ext_ref[a * SUBLANES:a * SUBLANES + tm, :]
        return sh_ref[b - 1, a * SUBLANES:a * SUBLANES + tm, :]

    _fill_ext(ext_ref, sh_ref, up_ref, upl_ref, upr_ref, tm)
    x = up_ref[0]
    lane = lax.broadcasted_iota(jnp.int32, (tm, POOL_WIDTH), 1)
    t = lax.broadcasted_iota(jnp.int32, (tm, POOL_WIDTH), 0) + j * tm

    run = x + shifted(-1)
    sums = [run]
    for lo in (2, 4, 8):
        for d in list(range(-lo, -lo // 2)) + list(range(lo // 2, lo)):
            run = run + shifted(d)
        sums.append(run)
    wsum = jnp.where(lane < POOL_GROUP, sums[0],
                     jnp.where(lane < 2 * POOL_GROUP, sums[1],
                               jnp.where(lane < 3 * POOL_GROUP, sums[2], sums[3])))
    lo = jnp.where(lane < POOL_GROUP, 1,
                   jnp.where(lane < 2 * POOL_GROUP, 2, jnp.where(lane < 3 * POOL_GROUP, 4, 8)))
    cnt = jnp.minimum(t + lo, seq) - jnp.maximum(t - lo, 0)
    diff = wsum / cnt.astype(F32) - x
    pooled = _dot(diff.astype(BF16), poolw_ref[...]) * pools_ref[...]

    _fill_ext(ext_ref, sh_ref, uc_ref, ucl_ref, ucr_ref, tm)
    acc = jnp.zeros((tm, CONV_WIDTH), F32) + cb_ref[...]
    for k in range(CONV_K):
        acc = acc + dw_ref[k:k + 1, :] * shifted(k - CONV_K // 2)
    y = _ln(acc) * clg_ref[...] + clb_ref[...]
    y = y * _sigmoid(y)
    conv = _dot(y.astype(BF16), cpw_ref[...])

    cat = jnp.concatenate([att_ref[0], pooled.astype(BF16), conv.astype(BF16)], axis=1)
    out = _dot(cat, wout_ref[...])
    gate = mod_ref[0, 2:3, :]
    z1 = _ln(alpha * z_ref[0] + gate * out) * l1g_ref[...] + l1b_ref[...]
    z1_ref[0] = z1
    h = _ln(z1) * (1.0 + mod_ref[0, 4:5, :]) + mod_ref[0, 3:4, :]
    h_ref[0] = h.astype(BF16)
    hp_ref[0] = _pack_rows(h)


def _mix(att, up, uc, z, mod, w, tm, alpha):
    b, s, d = z.shape
    hb = tm // HALO
    nhb = s // HALO
    full = lambda shape: pl.BlockSpec(shape, lambda i, j: (0,) * len(shape))
    tok = lambda width: pl.BlockSpec((1, tm, width), lambda i, j: (i, j, 0))
    left = lambda width: pl.BlockSpec((1, HALO, width),
                                      lambda i, j: (i, jnp.maximum(j * hb - 1, 0), 0))
    right = lambda width: pl.BlockSpec((1, HALO, width),
                                       lambda i, j: (i, jnp.minimum((j + 1) * hb, nhb - 1), 0))
    body = functools.partial(_mix_kernel, tm=tm, seq=s, alpha=alpha)
    return pl.pallas_call(
        body,
        grid=(b, s // tm),
        in_specs=[
            tok(MLA_WIDTH),
            tok(POOL_WIDTH), left(POOL_WIDTH), right(POOL_WIDTH),
            tok(CONV_WIDTH), left(CONV_WIDTH), right(CONV_WIDTH),
            tok(d),
            pl.BlockSpec((1, 6, d), lambda i, j: (i, 0, 0)),
            full((POOL_WIDTH, POOL_WIDTH)), full((1, POOL_WIDTH)),
            full((CONV_K + 1, CONV_WIDTH)), full((1, CONV_WIDTH)), full((1, CONV_WIDTH)),
            full((1, CONV_WIDTH)), full((CONV_WIDTH, CONV_WIDTH)),
            full((d, d)), full((1, d)), full((1, d)),
        ],
        out_specs=[tok(d), tok(d), tok(d // 2)],
        out_shape=[jax.ShapeDtypeStruct((b, s, d), F32), jax.ShapeDtypeStruct((b, s, d), BF16),
                   jax.ShapeDtypeStruct((b, s, d // 2), jnp.uint32)],
        scratch_shapes=[pltpu.VMEM((tm + 2 * HALO, POOL_WIDTH), F32),
                        pltpu.VMEM((SUBLANES - 1, tm + 2 * HALO - SUBLANES, POOL_WIDTH), F32)],
        compiler_params=_cparams(("parallel", "parallel")),
        name="mix",
    )(att, up, up, up, uc, uc, uc, z, mod, w["pool_w"], w["pool_s"], w["conv_dw"], w["conv_b"],
      w["conv_ln_g"], w["conv_ln_b"], w["conv_pw"], w["w_out"], w["ln1_g"], w["ln1_b"])


def _select_experts(h, rw, rb):
    tm = h.shape[0]
    logits = _dot(h, rw)
    scores = _sigmoid(logits.T[0:N_EXPERTS, :])
    sel = scores + rb
    neg = -jnp.inf

    row8 = lax.broadcasted_iota(jnp.int32, (GROUP_SIZE, tm), 0)
    gscore = []
    for g in range(N_GROUPS):
        xg = sel[g * GROUP_SIZE:(g + 1) * GROUP_SIZE, :]
        m1 = jnp.max(xg, axis=0, keepdims=True)
        i1 = jnp.min(jnp.where(xg == m1, row8, GROUP_SIZE), axis=0, keepdims=True)
        m2 = jnp.max(jnp.where(row8 == i1, neg, xg), axis=0, keepdims=True)
        gscore.append(m1 + m2)
    masked = []
    for g in range(N_GROUPS):
        ahead = jnp.zeros((1, tm), F32)
        for g2 in range(N_GROUPS):
            if g2 < g:
                ahead = ahead + jnp.where(gscore[g2] >= gscore[g], 1.0, 0.0)
            elif g2 > g:
                ahead = ahead + jnp.where(gscore[g2] > gscore[g], 1.0, 0.0)
        keep = ahead < TOPK_GROUPS
        masked.append(jnp.where(keep, sel[g * GROUP_SIZE:(g + 1) * GROUP_SIZE, :], neg))
    masked = jnp.concatenate(masked, axis=0)

    row = lax.broadcasted_iota(jnp.int32, (N_EXPERTS, tm), 0)
    chosen = jnp.zeros((N_EXPERTS, tm), F32)
    hits = []
    for _ in range(TOP_K):
        m = jnp.max(masked, axis=0, keepdims=True)
        idx = jnp.min(jnp.where(masked == m, row, N_EXPERTS), axis=0, keepdims=True)
        hit = row == idx
        hits.append(hit)
        chosen = jnp.where(hit, 1.0, chosen)
        masked = jnp.where(hit, neg, masked)
    w = chosen * scores
    gates = w / jnp.sum(w, axis=0, keepdims=True) * ROUTED_SCALE
    return gates, chosen, hits


def _lanes_to_rows(rows):
    tm = rows[0].shape[1]
    pad = jnp.zeros((LANES - len(rows), tm), F32)
    return jnp.concatenate(rows + [pad], axis=0).T


def _route_kernel(h_ref, rw_ref, rb_ref, g_ref):
    gates, _, _ = _select_experts(h_ref[...], rw_ref[...], rb_ref[...])
    tm = gates.shape[1]
    pad = jnp.zeros((LANES - N_EXPERTS, tm), F32)
    g_ref[...] = jnp.concatenate([gates, pad], axis=0).T


def _route_sorted_kernel(h_ref, rw_ref, rb_ref, g4_ref, eidx_ref, rank_ref, cnt_ref, base_ref):
    @pl.when(pl.program_id(0) == 0)
    def _():
        base_ref[...] = jnp.zeros_like(base_ref)

    gates, chosen, hits = _select_experts(h_ref[...], rw_ref[...], rb_ref[...])
    tm = gates.shape[1]
    ii = lax.broadcasted_iota(jnp.int32, (tm, tm), 0)
    jj = lax.broadcasted_iota(jnp.int32, (tm, tm), 1)
    upper = jnp.where(ii <= jj, 1.0, 0.0).astype(BF16)
    prefix = _dot(chosen.astype(BF16), upper)
    rank_full = base_ref[...] + prefix - 1.0
    row = lax.broadcasted_iota(jnp.int32, (N_EXPERTS, tm), 0).astype(F32)
    pick = lambda hit, val: jnp.sum(jnp.where(hit, val, 0.0), axis=0, keepdims=True)
    g4_ref[...] = _lanes_to_rows([pick(hit, gates) for hit in hits])
    eidx_ref[...] = jnp.concatenate([pick(hit, row) for hit in hits], axis=0).astype(jnp.int32)
    rank_ref[...] = jnp.concatenate([pick(hit, rank_full) for hit in hits], axis=0).astype(jnp.int32)
    base_ref[...] += jnp.sum(chosen, axis=1, keepdims=True)
    cnt_ref[...] = jnp.broadcast_to(base_ref[...], cnt_ref.shape)


def _route(h, rw, rb, tm):
    t, d = h.shape
    return pl.pallas_call(
        _route_kernel,
        grid=(t // tm,),
        in_specs=[
            pl.BlockSpec((tm, d), lambda i: (i, 0)),
            pl.BlockSpec((d, LANES), lambda i: (0, 0)),
            pl.BlockSpec((N_EXPERTS, 1), lambda i: (0, 0)),
        ],
        out_specs=pl.BlockSpec((tm, LANES), lambda i: (i, 0)),
        out_shape=jax.ShapeDtypeStruct((t, LANES), F32),
        compiler_params=_cparams(("parallel",)),
        name="route",
    )(h, rw, rb)


def _route_sorted(h, rw, rb, tm):
    t, d = h.shape
    pick = pl.BlockSpec((TOP_K, tm), lambda i: (0, i))
    return pl.pallas_call(
        _route_sorted_kernel,
        grid=(t // tm,),
        in_specs=[
            pl.BlockSpec((tm, d), lambda i: (i, 0)),
            pl.BlockSpec((d, LANES), lambda i: (0, 0)),
            pl.BlockSpec((N_EXPERTS, 1), lambda i: (0, 0)),
        ],
        out_specs=[pl.BlockSpec((tm, LANES), lambda i: (i, 0)), pick, pick,
                   pl.BlockSpec((N_EXPERTS, LANES), lambda i: (0, 0))],
        out_shape=[jax.ShapeDtypeStruct((t, LANES), F32),
                   jax.ShapeDtypeStruct((TOP_K, t), jnp.int32),
                   jax.ShapeDtypeStruct((TOP_K, t), jnp.int32),
                   jax.ShapeDtypeStruct((N_EXPERTS, LANES), F32)],
        scratch_shapes=[pltpu.VMEM((N_EXPERTS, 1), F32)],
        compiler_params=_cparams(("arbitrary",)),
        name="route_sorted",
    )(h, rw, rb)


def _swiglu_hidden(x, wg, wu):
    hg = _dot(x, wg)
    return hg * _sigmoid(hg) * _dot(x, wu)


ROW_TILE = 512


def _pack_rows(x):
    n = x.shape[1] // 2
    bits = lax.bitcast_convert_type(x.astype(BF16).astype(F32), jnp.uint32)
    return bits[:, :n] | (bits[:, n:] >> 16)


def _unpack_rows(w):
    hi = lax.bitcast_convert_type(w & jnp.uint32(0xFFFF0000), F32)
    lo = lax.bitcast_convert_type(w << 16, F32)
    return jnp.concatenate([hi, lo], axis=1)


def _experts_kernel(te_ref, src0_ref, src_ref, dst_ref, hp_ref, wg0_ref, wu0_ref, wd0_ref,
                    wg1_ref, wu1_ref, wd1_ref, ys_ref, x0_ref, x1_ref, y0_ref, y1_ref, gsem, ssem):
    del te_ref
    j = pl.program_id(0)
    xbufs, ybufs = (x0_ref, x1_ref), (y0_ref, y1_ref)

    def gather(idx_ref, off, slot):
        for r in range(ROW_TILE):
            pltpu.make_async_copy(hp_ref.at[pl.ds(idx_ref[0, 0, off + r], 1), :],
                                  xbufs[slot].at[pl.ds(r, 1), :], gsem.at[slot]).start()

    def wait_rows(sem):
        pltpu.make_async_copy(x0_ref, x1_ref, sem).wait()

    @pl.when(j == 0)
    def _():
        y0_ref[...] = jnp.zeros_like(y0_ref)
        y1_ref[...] = jnp.zeros_like(y1_ref)
        gather(src0_ref, 0, 0)

    for slot, (wg_ref, wu_ref, wd_ref) in enumerate(((wg0_ref, wu0_ref, wd0_ref),
                                                     (wg1_ref, wu1_ref, wd1_ref))):
        other = 1 - slot
        wait_rows(gsem.at[slot])
        gather(src_ref, slot * ROW_TILE, other)
        for r in range(ROW_TILE):
            pltpu.make_async_copy(ybufs[other].at[pl.ds(r, 1), :],
                                  ys_ref.at[pl.ds(dst_ref[0, 0, slot * ROW_TILE + r], 1), :],
                                  ssem.at[other]).start()
        x = _unpack_rows(xbufs[slot][...]).astype(BF16)
        a = _swiglu_hidden(x, wg_ref[0, 0], wu_ref[0, 0])
        y = _pack_rows(_dot(a.astype(BF16), wd_ref[0, 0]))
        if slot == 0:
            @pl.when(j >= 1)
            def _():
                wait_rows(ssem.at[0])
        else:
            wait_rows(ssem.at[1])
        ybufs[slot][...] = y

    @pl.when(j == pl.num_programs(0) - 1)
    def _():
        wait_rows(ssem.at[0])
        wait_rows(gsem.at[0])


def _experts(tile_expert, src_first, src_tiles, dst_tiles, hp, n_out_rows, l, ew):
    wg, wu, wd = ew
    d, de = wg.shape[2], wg.shape[3]
    steps = src_tiles.shape[0]
    half = hp.shape[1]
    pair = lambda f: pl.BlockSpec((1, 1, 2 * ROW_TILE), f, memory_space=pltpu.SMEM)
    wspec = lambda shape, o: pl.BlockSpec((1, 1) + shape, lambda j, te: (l, te[2 * j + o], 0, 0))
    buf = pltpu.VMEM((ROW_TILE, half), jnp.uint32)
    return pl.pallas_call(
        _experts_kernel,
        grid_spec=pltpu.PrefetchScalarGridSpec(
            num_scalar_prefetch=1,
            grid=(steps,),
            in_specs=[
                pair(lambda j, te: (0, 0, 0)),
                pair(lambda j, te: (j, 0, 0)),
                pair(lambda j, te: (j, 0, 0)),
                pl.BlockSpec(memory_space=pl.ANY),
                wspec((d, de), 0), wspec((d, de), 0), wspec((de, d), 0),
                wspec((d, de), 1), wspec((d, de), 1), wspec((de, d), 1),
            ],
            out_specs=pl.BlockSpec(memory_space=pl.ANY),
            scratch_shapes=[buf, buf, buf, buf,
                            pltpu.SemaphoreType.DMA((2,)), pltpu.SemaphoreType.DMA((2,))],
        ),
        out_shape=jax.ShapeDtypeStruct((n_out_rows, half), jnp.uint32),
        compiler_params=_cparams(("arbitrary",)),
        name="experts",
    )(tile_expert, src_first, src_tiles, dst_tiles, hp, wg, wu, wd, wg, wu, wd)


def _combine_kernel(g4_ref, h_ref, swg_ref, swu_ref, swd_ref, z1_ref, mod_ref, l2g_ref, l2b_ref,
                    y0_ref, y1_ref, y2_ref, y3_ref, o_ref, *, alpha):
    a = _swiglu_hidden(h_ref[...], swg_ref[...], swu_ref[...])
    acc = _dot(a.astype(BF16), swd_ref[...])
    for k, y_ref in enumerate((y0_ref, y1_ref, y2_ref, y3_ref)):
        acc = acc + g4_ref[:, k:k + 1] * _unpack_rows(y_ref[...])
    gate = mod_ref[0, 5:6, :]
    o_ref[...] = _ln(alpha * z1_ref[...] + gate * acc) * l2g_ref[...] + l2b_ref[...]


def _combine(g4, h, ys, w, z1, mod, tm, tiles_per_batch, alpha):
    t, d = h.shape
    de = w["sh_wg"].shape[1]
    half = ys.shape[1]
    nt = t // tm
    full = lambda shape: pl.BlockSpec(shape, lambda i: (0,) * len(shape))
    pick = lambda k: pl.BlockSpec((tm, half), lambda i: (k * nt + i, 0))
    body = functools.partial(_combine_kernel, alpha=alpha)
    return pl.pallas_call(
        body,
        grid=(nt,),
        in_specs=[
            pl.BlockSpec((tm, LANES), lambda i: (i, 0)),
            pl.BlockSpec((tm, d), lambda i: (i, 0)),
            full((d, de)), full((d, de)), full((de, d)),
            pl.BlockSpec((tm, d), lambda i: (i, 0)),
            pl.BlockSpec((1, 6, d), lambda i: (i // tiles_per_batch, 0, 0)),
            full((1, d)), full((1, d)),
            pick(0), pick(1), pick(2), pick(3),
        ],
        out_specs=pl.BlockSpec((tm, d), lambda i: (i, 0)),
        out_shape=jax.ShapeDtypeStruct((t, d), F32),
        compiler_params=_cparams(("parallel",)),
        name="combine",
    )(g4, h, w["sh_wg"], w["sh_wu"], w["sh_wd"], z1, mod, w["ln2_g"], w["ln2_b"], ys, ys, ys, ys)


def _moe_sorted(h, hp, l, ew, w, z1, mod, tm, tiles_per_batch, alpha):
    t = h.shape[0]
    n_pairs = t * TOP_K
    g4, eidx, rank, cnt = _route_sorted(h, w["router_w"], w["router_b"], tm)
    counts = cnt[:, 0].astype(jnp.int32)
    sizes = (counts + ROW_TILE - 1) // ROW_TILE * ROW_TILE
    ends = jnp.cumsum(sizes)
    starts = ends - sizes
    n_tiles = n_pairs // ROW_TILE + N_EXPERTS
    tile_expert = jnp.minimum(
        jnp.sum(jnp.arange(n_tiles)[:, None] >= (ends // ROW_TILE)[None, :], axis=1), N_EXPERTS - 1
    ).astype(jnp.int32)
    onehot = eidx[:, :, None] == jnp.arange(N_EXPERTS)[None, None, :]
    pos = rank + jnp.sum(jnp.where(onehot, starts[None, None, :], 0), axis=-1)
    def scratch_rows(tile):
        return n_pairs + (tile % 2)[:, None] * ROW_TILE + jnp.arange(ROW_TILE)[None, :]

    tiles = jnp.arange(n_tiles)
    inv = scratch_rows(tiles).reshape(-1).at[pos.reshape(-1)].set(jnp.arange(n_pairs), unique_indices=True)
    inv = inv.reshape(n_tiles, ROW_TILE)
    src = jnp.where(inv < n_pairs, inv % t, 0)
    steps = (n_tiles + 2) // 2
    c = jnp.arange(2 * steps)
    src_next = src[jnp.minimum(c + 1, n_tiles - 1)]
    has_prev = ((c >= 1) & (c <= n_tiles))[:, None]
    dst_prev = jnp.where(has_prev, inv[jnp.clip(c - 1, 0, n_tiles - 1)], scratch_rows(c - 1))
    as_steps = lambda a: a.astype(jnp.int32).reshape(steps, 1, 2 * ROW_TILE)
    src_first = src[0:2].astype(jnp.int32).reshape(1, 1, 2 * ROW_TILE)
    ys = _experts(tile_expert[jnp.minimum(c, n_tiles - 1)], src_first, as_steps(src_next),
                  as_steps(dst_prev), hp, n_pairs + 2 * ROW_TILE, l, ew)
    return _combine(g4, h, ys, w, z1, mod, tm, tiles_per_batch, alpha)


def _moe_kernel(h_ref, g_ref, wg_ref, wu_ref, wd_ref, swg_ref, swu_ref, swd_ref, z1_ref, mod_ref,
                l2g_ref, l2b_ref, o_ref, acc_ref, *, alpha):
    e = pl.program_id(1)
    x = h_ref[...]

    @pl.when(e == 0)
    def _():
        a = _swiglu_hidden(x, swg_ref[...], swu_ref[...])
        acc_ref[...] = _dot(a.astype(BF16), swd_ref[...])

    lane = lax.broadcasted_iota(jnp.int32, g_ref.shape, 1)
    gcol = jnp.sum(jnp.where(lane == e, g_ref[...], 0.0), axis=1, keepdims=True)
    a = _swiglu_hidden(x, wg_ref[0, 0], wu_ref[0, 0]) * gcol
    acc_ref[...] += _dot(a.astype(BF16), wd_ref[0, 0])

    @pl.when(e == pl.num_programs(1) - 1)
    def _():
        gate = mod_ref[0, 5:6, :]
        o_ref[...] = _ln(alpha * z1_ref[...] + gate * acc_ref[...]) * l2g_ref[...] + l2b_ref[...]


def _moe(h, gates, l, ew, w, z1, mod, tm, tiles_per_batch, alpha):
    t, d = h.shape
    wg, wu, wd = ew
    ne, de = wg.shape[1], wg.shape[3]
    body = functools.partial(_moe_kernel, alpha=alpha)
    return pl.pallas_call(
        body,
        grid=(t // tm, ne),
        in_specs=[
            pl.BlockSpec((tm, d), lambda i, e: (i, 0)),
            pl.BlockSpec((tm, LANES), lambda i, e: (i, 0)),
            pl.BlockSpec((1, 1, d, de), lambda i, e: (l, e, 0, 0)),
            pl.BlockSpec((1, 1, d, de), lambda i, e: (l, e, 0, 0)),
            pl.BlockSpec((1, 1, de, d), lambda i, e: (l, e, 0, 0)),
            pl.BlockSpec((d, de), lambda i, e: (0, 0)),
            pl.BlockSpec((d, de), lambda i, e: (0, 0)),
            pl.BlockSpec((de, d), lambda i, e: (0, 0)),
            pl.BlockSpec((tm, d), lambda i, e: (i, 0)),
            pl.BlockSpec((1, 6, d), lambda i, e: (i // tiles_per_batch, 0, 0)),
            pl.BlockSpec((1, d), lambda i, e: (0, 0)),
            pl.BlockSpec((1, d), lambda i, e: (0, 0)),
        ],
        out_specs=pl.BlockSpec((tm, d), lambda i, e: (i, 0)),
        out_shape=jax.ShapeDtypeStruct((t, d), F32),
        scratch_shapes=[pltpu.VMEM((tm, d), F32)],
        compiler_params=_cparams(("parallel", "arbitrary")),
        name="moe",
    )(h, gates, wg, wu, wd, w["sh_wg"], w["sh_wu"], w["sh_wd"], z1, mod, w["ln2_g"], w["ln2_b"])


def _rope_swap(r):
    r4 = r.reshape(r.shape[:-1] + (2, 2, ROPE_AXIS // 2))
    return jnp.stack([-r4[..., 1, :], r4[..., 0, :]], axis=-2).reshape(r.shape)


def _prep_layer(l, w_in, q_norm_g, w_uq, kv_norm_g, w_ukv, pool_w, pool_scale, conv_dw, conv_b,
                conv_ln_g, conv_ln_b, conv_pw, w_out, ln1_g, ln1_b, router_w, router_bias,
                exp_wg, exp_wu, exp_wd, sh_wg, sh_wu, sh_wd, ln2_g, ln2_b):
    d = w_in.shape[1]
    wi = w_in[l]
    w_in_p = jnp.concatenate([
        wi[:, 0:OFF_KR], wi[:, OFF_KR:OFF_POOL], jnp.zeros((d, PC_POOL - PC_KR - QK_ROPE), F32),
        wi[:, OFF_POOL:]], axis=1).astype(BF16)

    uq = w_uq[l].reshape(Q_LORA, MLA_HEADS, QK_NOPE + QK_ROPE)
    nope, rope = uq[..., :QK_NOPE], uq[..., QK_NOPE:]
    wqat = jnp.concatenate([nope, rope, rope], axis=-1).reshape(Q_LORA, QK_WIDTH).T.astype(BF16)
    wqbt = jnp.concatenate([jnp.zeros_like(nope), jnp.zeros_like(rope), _rope_swap(rope)],
                           axis=-1).reshape(Q_LORA, QK_WIDTH).T.astype(BF16)

    ukv = w_ukv[l].reshape(KV_LORA, MLA_HEADS, QK_NOPE + V_DIM)
    kn, vv = ukv[..., :QK_NOPE], ukv[..., QK_NOPE:]
    z32 = jnp.zeros((KV_LORA, MLA_HEADS, QK_ROPE), F32)
    top = jnp.concatenate([kn, z32, z32], axis=-1)
    eye = jnp.broadcast_to(jnp.eye(QK_ROPE, dtype=F32)[:, None, :], (QK_ROPE, MLA_HEADS, QK_ROPE))
    zr = jnp.zeros((QK_ROPE, MLA_HEADS, QK_ROPE), F32)
    zn = jnp.zeros((QK_ROPE, MLA_HEADS, QK_NOPE), F32)
    padrows = jnp.zeros((KV_LORA - QK_ROPE, MLA_HEADS, HEAD_PAD), F32)

    def kmat(kr_rows):
        return jnp.concatenate([top, kr_rows, padrows], axis=0).reshape(2 * KV_LORA, QK_WIDTH).astype(BF16)

    wka_lat = kmat(jnp.concatenate([zn, zr, eye], axis=-1))
    wkb_lat = kmat(jnp.concatenate([zn, zr, _rope_swap(eye)], axis=-1))
    wka_ctx = kmat(jnp.concatenate([zn, eye, zr], axis=-1))
    wkb_ctx = jnp.zeros((2 * KV_LORA, QK_WIDTH), BF16)

    zv = jnp.zeros_like(vv)
    even = (jnp.arange(MLA_HEADS) % 2 == 0)[None, :, None]
    wv = jnp.where(even, jnp.concatenate([vv, zv], -1), jnp.concatenate([zv, vv], -1))
    wvt = wv.reshape(KV_LORA, QK_WIDTH).T.astype(BF16)
    lane = jnp.arange(QK_WIDTH) % (2 * HEAD_PAD)
    vbt = ((lane == V_DIM) | (lane == HEAD_PAD)).astype(F32)[:, None]

    pw = jnp.zeros((POOL_WIDTH, POOL_WIDTH), F32)
    for gi in range(len(POOL_WINDOWS)):
        pw = pw.at[gi * POOL_GROUP:(gi + 1) * POOL_GROUP, gi * POOL_GROUP:(gi + 1) * POOL_GROUP].set(pool_w[l, gi])

    common = dict(
        w_in=w_in_p, q_g=q_norm_g[l][None], kv_g=kv_norm_g[l][None], wqat=wqat, wqbt=wqbt, wvt=wvt, vbt=vbt,
        pool_w=pw.astype(BF16), pool_s=pool_scale[l][None],
        conv_dw=jnp.concatenate([conv_dw[l], jnp.zeros((1, CONV_WIDTH), F32)], axis=0),
        conv_b=conv_b[l][None], conv_ln_g=conv_ln_g[l][None], conv_ln_b=conv_ln_b[l][None],
        conv_pw=conv_pw[l].astype(BF16), w_out=w_out[l].astype(BF16),
        ln1_g=ln1_g[l][None], ln1_b=ln1_b[l][None],
        router_w=jnp.pad(router_w[l], ((0, 0), (0, LANES - N_EXPERTS))).astype(BF16),
        router_b=router_bias[l][:, None],
        sh_wg=sh_wg[l].astype(BF16), sh_wu=sh_wu[l].astype(BF16), sh_wd=sh_wd[l].astype(BF16),
        ln2_g=ln2_g[l][None], ln2_b=ln2_b[l][None],
    )
    return dict(common, wka=wka_lat, wkb=wkb_lat), dict(common, wka=wka_ctx, wkb=wkb_ctx)


def _rope_tables(n_lat, n_ctx):
    t = jnp.arange(n_lat)
    inv = ROPE_BASE ** (-jnp.arange(0, ROPE_AXIS, 2, dtype=F32) / ROPE_AXIS)
    ang_r = (t // GRID_W).astype(F32)[:, None] * inv
    ang_c = (t % GRID_W).astype(F32)[:, None] * inv
    cos = jnp.concatenate([jnp.cos(ang_r), jnp.cos(ang_r), jnp.cos(ang_c), jnp.cos(ang_c)], axis=1)
    sin = jnp.concatenate([jnp.sin(ang_r), jnp.sin(ang_r), jnp.sin(ang_c), jnp.sin(ang_c)], axis=1)
    ck = jnp.concatenate([jnp.ones((n_lat, HEAD_PAD - QK_ROPE), F32), cos], axis=1)
    sk = jnp.concatenate([jnp.zeros((n_lat, HEAD_PAD - QK_ROPE), F32), sin], axis=1)
    qs = ATTN_SCALE * LOG2E
    lat = ((ck * qs).T, (sk * qs).T, ck, sk)
    one = jnp.ones((n_ctx, HEAD_PAD), F32)
    zero = jnp.zeros((n_ctx, HEAD_PAD), F32)
    ctx = ((one * qs).T, zero.T, one, zero)
    return lat, ctx


def _sublayers(z, mod, l, ew, w, att, up, uc, tm, tm_moe, alpha, sorted_moe):
    b, s, d = z.shape
    z1, h, hp = _mix(att, up, uc, z, mod, w, tm, alpha)
    h2, z2 = h.reshape(b * s, d), z1.reshape(b * s, d)
    if sorted_moe:
        out = _moe_sorted(h2, hp.reshape(b * s, d // 2), l, ew, w, z2, mod, tm, s // tm, alpha)
    else:
        gates = _route(h2, w["router_w"], w["router_b"], tm)
        out = _moe(h2, gates, l, ew, w, z2, mod, tm_moe, s // tm_moe, alpha)
    return out.reshape(b, s, d)


def kernel(x, c, ctx, c_ctx, ada_w, ada_b, w_in, q_norm_g, w_uq, kv_norm_g, w_ukv, pool_w, pool_scale, conv_dw, conv_b, conv_ln_g, conv_ln_b, conv_pw, w_out, ln1_g, ln1_b, router_w, router_bias, exp_wg, exp_wu, exp_wd, sh_wg, sh_wu, sh_wd, ln2_g, ln2_b):
    b, s, d = x.shape
    n_ctx = ctx.shape[1]
    depth = ada_w.shape[0]
    alpha = (2 * depth) ** 0.25
    assert b + 1 <= 8 and s % GRID_W == 0

    tm_l = min(512, s)
    tm_c = min(256, n_ctx)
    tm_moe_l = min(1024, s)
    tm_moe_c = n_ctx
    bq = min(1024, s)

    cond = jnp.concatenate([c, c_ctx[None], jnp.zeros((8 - b - 1, d), F32)], axis=0)
    mods = _ada(cond, ada_w, ada_b)
    tab_l, tab_c = _rope_tables(s, n_ctx)
    ew = (exp_wg.astype(BF16), exp_wu.astype(BF16), exp_wd.astype(BF16))

    zl, zc = x, ctx
    for l in range(depth):
        last = l == depth - 1
        w_l, w_c = _prep_layer(l, w_in, q_norm_g, w_uq, kv_norm_g, w_ukv, pool_w, pool_scale,
                               conv_dw, conv_b, conv_ln_g, conv_ln_b, conv_pw, w_out, ln1_g, ln1_b,
                               router_w, router_bias, exp_wg, exp_wu, exp_wd, sh_wg, sh_wu, sh_wd,
                               ln2_g, ln2_b)
        mod_l = mods[l, :b].reshape(b, 6, d)
        mod_c = jnp.broadcast_to(mods[l, b].reshape(1, 6, d), (b, 6, d))

        q_l, k_l, v_l, up_l, uc_l = _proj(zl, mod_l, w_l, tab_l, tm_l)
        q_c, k_c, v_c, up_c, uc_c = _proj(zc, mod_c, w_c, tab_c, tm_c)
        att_l = _attn(q_l, k_c, v_c, k_l, v_l, bq)
        zl = _sublayers(zl, mod_l, l, ew, w_l, att_l, up_l, uc_l, tm_l, tm_moe_l, alpha, True)
        if not last:
            att_c = _attn(q_c, k_c, v_c, None, None, tm_c)
            zc = _sublayers(zc, mod_c, l, ew, w_c, att_c, up_c, uc_c, tm_c, tm_moe_c, alpha, False)
    return zl
```

```python
import functools
import math

import jax
import jax.numpy as jnp
from jax import lax
from jax.experimental import pallas as pl
from jax.experimental.pallas import tpu as pltpu

GRID_W = 64
MLA_HEADS = 8
QK_NOPE = 64
QK_ROPE = 32
V_DIM = 64
Q_LORA = 256
KV_LORA = 128
ROPE_AXIS = QK_ROPE // 2
ROPE_BASE = 10000.0
ATTN_SCALE = (QK_NOPE + QK_ROPE) ** -0.5
POOL_WINDOWS = (2, 4, 8, 16)
POOL_GROUP = 64
POOL_WIDTH = POOL_GROUP * len(POOL_WINDOWS)
CONV_WIDTH = 256
CONV_K = 31
MLA_WIDTH = MLA_HEADS * V_DIM
OFF_KV = Q_LORA
OFF_KR = OFF_KV + KV_LORA
OFF_POOL = OFF_KR + QK_ROPE
OFF_CONV = OFF_POOL + POOL_WIDTH
N_EXPERTS = 32
TOP_K = 4
N_GROUPS = 4
TOPK_GROUPS = 2
GROUP_SIZE = N_EXPERTS // N_GROUPS
ROUTED_SCALE = 2.5
EPS = 1e-6

LANES = 128
HEAD_PAD = LANES
QK_WIDTH = MLA_HEADS * HEAD_PAD
HALO = 16
P_COLS = 1280
PC_Q, PC_KV, PC_KR, PC_POOL, PC_CA, PC_CG = 0, 256, 384, 512, 768, 1024
VMEM_LIMIT = 48 * 1024 * 1024
LOG2E = math.log2(math.e)

F32 = jnp.float32
BF16 = jnp.bfloat16


def _cparams(sem, flags=None):
    return pltpu.CompilerParams(dimension_semantics=sem, vmem_limit_bytes=VMEM_LIMIT, flags=flags)


def _ln(x):
    mu = jnp.mean(x, axis=-1, keepdims=True)
    xc = x - mu
    var = jnp.mean(xc * xc, axis=-1, keepdims=True)
    return xc * lax.rsqrt(var + EPS)


def _rms(x):
    return x * lax.rsqrt(jnp.mean(x * x, axis=-1, keepdims=True) + EPS)


def _sigmoid(x):
    return 1.0 / (1.0 + jnp.exp(-x))


def _dot(a, b):
    return jnp.dot(a, b, preferred_element_type=F32)


def _dot_nt(a, b):
    return lax.dot_general(a, b, (((1,), (1,)), ((), ())), preferred_element_type=F32)


def _ada_kernel(c_ref, w_ref, b_ref, o_ref):
    x = c_ref[...]
    x = x * _sigmoid(x)
    o_ref[0] = _dot(x.astype(BF16), w_ref[0].astype(BF16)) + b_ref[0]


def _ada(cond, ada_w, ada_b):
    depth, d, n = ada_w.shape
    tn = 1536
    return pl.pallas_call(
        _ada_kernel,
        grid=(depth, n // tn),
        in_specs=[
            pl.BlockSpec((8, d), lambda l, j: (0, 0)),
            pl.BlockSpec((1, d, tn), lambda l, j: (l, 0, j)),
            pl.BlockSpec((1, 1, tn), lambda l, j: (l, 0, j)),
        ],
        out_specs=pl.BlockSpec((1, 8, tn), lambda l, j: (l, 0, j)),
        out_shape=jax.ShapeDtypeStruct((depth, 8, n), F32),
        compiler_params=_cparams(("parallel", "parallel")),
        name="ada",
    )(cond, ada_w, ada_b.reshape(depth, 1, n))


def _proj_kernel(z_ref, mod_ref, win_ref, qg_ref, kvg_ref, wqat_ref, wqbt_ref, wka_ref, wkb_ref,
                 wvt_ref, vbt_ref, cqt_ref, sqt_ref, ck_ref, sk_ref,
                 qt_ref, k_ref, vt_ref, up_ref, uc_ref):
    z = z_ref[0]
    shift = mod_ref[0, 0:1, :]
    scale = mod_ref[0, 1:2, :]
    h = _ln(z) * (1.0 + scale) + shift
    p = _dot(h.astype(BF16), win_ref[...])
    qn = (_rms(p[:, PC_Q:PC_KV]) * qg_ref[...]).astype(BF16)
    kvn = (_rms(p[:, PC_KV:PC_KR]) * kvg_ref[...]).astype(BF16)
    xk = jnp.concatenate([kvn, p[:, PC_KR:PC_POOL].astype(BF16)], axis=1)
    qat = _dot_nt(wqat_ref[...], qn)
    qbt = _dot_nt(wqbt_ref[...], qn)
    ka = _dot(xk, wka_ref[...])
    kb = _dot(xk, wkb_ref[...])
    cqt, sqt, ck, sk = cqt_ref[...], sqt_ref[...], ck_ref[...], sk_ref[...]
    for hd in range(MLA_HEADS):
        sl = slice(hd * HEAD_PAD, (hd + 1) * HEAD_PAD)
        qt_ref[0, sl, :] = (qat[sl, :] * cqt + qbt[sl, :] * sqt).astype(BF16)
        k_ref[0, :, sl] = (ka[:, sl] * ck + kb[:, sl] * sk).astype(BF16)
    vt_ref[0, 0] = (_dot_nt(wvt_ref[...], kvn) + vbt_ref[...]).astype(BF16)
    up_ref[0] = p[:, PC_POOL:PC_CA]
    uc_ref[0] = p[:, PC_CA:PC_CG] * _sigmoid(p[:, PC_CG:P_COLS])


def _proj(z, mod, w, tabs, tm):
    b, s, d = z.shape
    full = lambda shape: pl.BlockSpec(shape, lambda i, j: (0,) * len(shape))
    tok = lambda width: pl.BlockSpec((1, tm, width), lambda i, j: (i, j, 0))
    tab = pl.BlockSpec((tm, HEAD_PAD), lambda i, j: (j, 0))
    tab_t = pl.BlockSpec((HEAD_PAD, tm), lambda i, j: (0, j))
    return pl.pallas_call(
        _proj_kernel,
        grid=(b, s // tm),
        in_specs=[
            tok(d),
            pl.BlockSpec((1, 6, d), lambda i, j: (i, 0, 0)),
            full((d, P_COLS)), full((1, Q_LORA)), full((1, KV_LORA)),
            full((QK_WIDTH, Q_LORA)), full((QK_WIDTH, Q_LORA)),
            full((2 * KV_LORA, QK_WIDTH)), full((2 * KV_LORA, QK_WIDTH)),
            full((QK_WIDTH, KV_LORA)), full((QK_WIDTH, 1)),
            tab_t, tab_t, tab, tab,
        ],
        out_specs=[
            pl.BlockSpec((1, QK_WIDTH, tm), lambda i, j: (i, 0, j)),
            tok(QK_WIDTH),
            pl.BlockSpec((1, 1, QK_WIDTH, tm), lambda i, j: (i, j, 0, 0)),
            tok(POOL_WIDTH), tok(CONV_WIDTH)],
        out_shape=[
            jax.ShapeDtypeStruct((b, QK_WIDTH, s), BF16),
            jax.ShapeDtypeStruct((b, s, QK_WIDTH), BF16),
            jax.ShapeDtypeStruct((b, s // tm, QK_WIDTH, tm), BF16),
            jax.ShapeDtypeStruct((b, s, POOL_WIDTH), F32),
            jax.ShapeDtypeStruct((b, s, CONV_WIDTH), F32),
        ],
        compiler_params=_cparams(("parallel", "parallel")),
        name="proj",
    )(z, mod, w["w_in"], w["q_g"], w["kv_g"], w["wqat"], w["wqbt"], w["wka"], w["wkb"],
      w["wvt"], w["vbt"], *tabs)


HEADS_PER_STEP = 2


def _attn_body(qt_ref, kc_ref, vct_ref, kl_ref, vlt_ref, o_ref, s_refs, p_refs):
    heads = range(HEADS_PER_STEP)
    sls = [slice(hh * HEAD_PAD, (hh + 1) * HEAD_PAD) for hh in heads]
    qts = [qt_ref[0, sl, :] for sl in sls]
    ms, accs = [], []
    for qt, sl in zip(qts, sls):
        st = _dot(kc_ref[0, :, sl], qt)
        m = jnp.max(st, axis=0, keepdims=True)
        pt = jnp.exp2(st - m)
        ms.append(m)
        accs.append(_dot(vct_ref[0, 0, sl, :], pt.astype(BF16)))

    if kl_ref is not None:
        n, bk = vlt_ref.shape[1], vlt_ref.shape[3]
        assert n >= 2 and n % 2 == 0

        def score(c, slot):
            r0 = c * bk if isinstance(c, int) else pl.multiple_of(c * bk, bk)
            cms = []
            for hh in heads:
                st = _dot(kl_ref[0, pl.ds(r0, bk), sls[hh]], qts[hh])
                s_refs[slot][hh] = st
                cms.append(jnp.max(st, axis=0, keepdims=True))
            return cms

        def probs(slot, ms, cms):
            m_new = [jnp.maximum(ms[hh], cms[hh]) for hh in heads]
            alphas = [jnp.exp2(ms[hh] - m_new[hh]) for hh in heads]
            for hh in heads:
                p_refs[slot][hh] = jnp.exp2(s_refs[slot][hh] - m_new[hh]).astype(BF16)
            return m_new, alphas

        def accum(c, slot, accs, alphas):
            return [accs[hh] * alphas[hh] + _dot(vlt_ref[0, c, sls[hh], :], p_refs[slot][hh])
                    for hh in heads]

        def half(c, cur, nxt, ms, cms, accs, with_score=True):
            ms, alphas = probs(cur, ms, cms)
            if with_score:
                cms = score(c + 1, nxt)
            accs = accum(c, cur, accs, alphas)
            return ms, cms, accs

        cms = score(0, 0)

        def pair(t, carry):
            ms, cms, accs = [list(x) for x in carry]
            c = 2 * t
            ms, cms, accs = half(c, 0, 1, ms, cms, accs)
            ms, cms, accs = half(c + 1, 1, 0, ms, cms, accs)
            return tuple(tuple(x) for x in (ms, cms, accs))

        carry = tuple(tuple(x) for x in (ms, cms, accs))
        carry = lax.fori_loop(0, (n - 2) // 2, pair, carry)
        ms, cms, accs = [list(x) for x in carry]
        ms, cms, accs = half(n - 2, 0, 1, ms, cms, accs)
        ms, cms, accs = half(n - 1, 1, 0, ms, cms, accs, with_score=False)

    acc_even, acc_odd = accs
    row = lax.broadcasted_iota(jnp.int32, acc_even.shape, 0)
    ot = jnp.where(row < V_DIM, acc_even / acc_even[V_DIM:V_DIM + 1, :], acc_odd / acc_odd[0:1, :])
    o_ref[0] = ot.T.astype(o_ref.dtype)


def _attn_kernel_full(qt_ref, kc_ref, vct_ref, kl_ref, vlt_ref, o_ref, s0_ref, s1_ref, p0_ref, p1_ref):
    _attn_body(qt_ref, kc_ref, vct_ref, kl_ref, vlt_ref, o_ref, (s0_ref, s1_ref), (p0_ref, p1_ref))


def _attn_kernel_ctx(qt_ref, kc_ref, vct_ref, o_ref):
    _attn_body(qt_ref, kc_ref, vct_ref, None, None, o_ref, None, None)


def _attn(qt, kc, vct, kl, vlt, bq):
    b, _, s = qt.shape
    nc = kc.shape[1]
    w2 = HEADS_PER_STEP * HEAD_PAD
    in_specs = [
        pl.BlockSpec((1, w2, bq), lambda i, h, j: (i, h, j)),
        pl.BlockSpec((1, nc, w2), lambda i, h, j: (i, 0, h)),
        pl.BlockSpec((1, 1, w2, nc), lambda i, h, j: (i, 0, h, 0)),
    ]
    args = [qt, kc, vct]
    scratch = []
    if kl is None:
        body = _attn_kernel_ctx
    else:
        bk = vlt.shape[3]
        scratch = [pltpu.VMEM((HEADS_PER_STEP, bk, bq), F32)] * 2 + [pltpu.VMEM((HEADS_PER_STEP, bk, bq), BF16)] * 2
        in_specs += [
            pl.BlockSpec((1, kl.shape[1], w2), lambda i, h, j: (i, 0, h)),
            pl.BlockSpec((1, vlt.shape[1], w2, vlt.shape[3]), lambda i, h, j: (i, 0, h, 0)),
        ]
        args += [kl, vlt]
        body = _attn_kernel_full
    return pl.pallas_call(
        body,
        grid=(b, MLA_HEADS // HEADS_PER_STEP, s // bq),
        in_specs=in_specs,
        out_specs=pl.BlockSpec((1, bq, HEADS_PER_STEP * V_DIM), lambda i, h, j: (i, j, h)),
        out_shape=jax.ShapeDtypeStruct((b, s, MLA_WIDTH), BF16),
        scratch_shapes=scratch,
        compiler_params=_cparams(("parallel", "parallel", "arbitrary")),
        name="attn",
    )(*args)


SUBLANES = 8


def _fill_ext(ext_ref, sh_ref, main_ref, left_ref, right_ref, tm):
    j = pl.program_id(1)
    last = pl.num_programs(1) - 1
    ext_ref[0:HALO, :] = jnp.where(j > 0, left_ref[0], 0.0)
    ext_ref[HALO:HALO + tm, :] = main_ref[0]
    ext_ref[HALO + tm:HALO + tm + HALO, :] = jnp.where(j < last, right_ref[0], 0.0)
    n = sh_ref.shape[1]
    for b in range(1, SUBLANES):
        sh_ref[b - 1, :, :] = ext_ref[b:b + n, :]


def _mix_kernel(att_ref, up_ref, upl_ref, upr_ref, uc_ref, ucl_ref, ucr_ref, z_ref, mod_ref,
                poolw_ref, pools_ref, dw_ref, cb_ref, clg_ref, clb_ref, cpw_ref, wout_ref,
                l1g_ref, l1b_ref, z1_ref, h_ref, hp_ref, ext_ref, sh_ref, *, tm, seq, alpha):
    j = pl.program_id(1)

    def shifted(d):
        a, b = divmod(HALO + d, SUBLANES)
        if b == 0:
            return ext_ref[a * SUBLANES:a * SUBLANES + tm, :]
        return sh_ref[b - 1, a * SUBLANES:a * SUBLANES + tm, :]

    _fill_ext(ext_ref, sh_ref, up_ref, upl_ref, upr_ref, tm)
    x = up_ref[0]
    lane = lax.broadcasted_iota(jnp.int32, (tm, POOL_WIDTH), 1)
    t = lax.broadcasted_iota(jnp.int32, (tm, POOL_WIDTH), 0) + j * tm

    run = x + shifted(-1)
    sums = [run]
    for lo in (2, 4, 8):
        for d in list(range(-lo, -lo // 2)) + list(range(lo // 2, lo)):
            run = run + shifted(d)
        sums.append(run)
    wsum = jnp.where(lane < POOL_GROUP, sums[0],
                     jnp.where(lane < 2 * POOL_GROUP, sums[1],
                               jnp.where(lane < 3 * POOL_GROUP, sums[2], sums[3])))
    lo = jnp.where(lane < POOL_GROUP, 1,
                   jnp.where(lane < 2 * POOL_GROUP, 2, jnp.where(lane < 3 * POOL_GROUP, 4, 8)))
    cnt = jnp.minimum(t + lo, seq) - jnp.maximum(t - lo, 0)
    diff = wsum / cnt.astype(F32) - x
    pooled = _dot(diff.astype(BF16), poolw_ref[...]) * pools_ref[...]

    _fill_ext(ext_ref, sh_ref, uc_ref, ucl_ref, ucr_ref, tm)
    acc = jnp.zeros((tm, CONV_WIDTH), F32) + cb_ref[...]
    for k in range(CONV_K):
        acc = acc + dw_ref[k:k + 1, :] * shifted(k - CONV_K // 2)
    y = _ln(acc) * clg_ref[...] + clb_ref[...]
    y = y * _sigmoid(y)
    conv = _dot(y.astype(BF16), cpw_ref[...])

    cat = jnp.concatenate([att_ref[0], pooled.astype(BF16), conv.astype(BF16)], axis=1)
    out = _dot(cat, wout_ref[...])
    gate = mod_ref[0, 2:3, :]
    z1 = _ln(alpha * z_ref[0] + gate * out) * l1g_ref[...] + l1b_ref[...]
    z1_ref[0] = z1
    h = _ln(z1) * (1.0 + mod_ref[0, 4:5, :]) + mod_ref[0, 3:4, :]
    h_ref[0] = h.astype(BF16)
    hp_ref[0] = _pack_rows(h)


def _mix(att, up, uc, z, mod, w, tm, alpha):
    b, s, d = z.shape
    hb = tm // HALO
    nhb = s // HALO
    full = lambda shape: pl.BlockSpec(shape, lambda i, j: (0,) * len(shape))
    tok = lambda width: pl.BlockSpec((1, tm, width), lambda i, j: (i, j, 0))
    left = lambda width: pl.BlockSpec((1, HALO, width),
                                      lambda i, j: (i, jnp.maximum(j * hb - 1, 0), 0))
    right = lambda width: pl.BlockSpec((1, HALO, width),
                                       lambda i, j: (i, jnp.minimum((j + 1) * hb, nhb - 1), 0))
    body = functools.partial(_mix_kernel, tm=tm, seq=s, alpha=alpha)
    return pl.pallas_call(
        body,
        grid=(b, s // tm),
        in_specs=[
            tok(MLA_WIDTH),
            tok(POOL_WIDTH), left(POOL_WIDTH), right(POOL_WIDTH),
            tok(CONV_WIDTH), left(CONV_WIDTH), right(CONV_WIDTH),
            tok(d),
            pl.BlockSpec((1, 6, d), lambda i, j: (i, 0, 0)),
            full((POOL_WIDTH, POOL_WIDTH)), full((1, POOL_WIDTH)),
            full((CONV_K + 1, CONV_WIDTH)), full((1, CONV_WIDTH)), full((1, CONV_WIDTH)),
            full((1, CONV_WIDTH)), full((CONV_WIDTH, CONV_WIDTH)),
            full((d, d)), full((1, d)), full((1, d)),
        ],
        out_specs=[tok(d), tok(d), tok(d // 2)],
        out_shape=[jax.ShapeDtypeStruct((b, s, d), F32), jax.ShapeDtypeStruct((b, s, d), BF16),
                   jax.ShapeDtypeStruct((b, s, d // 2), jnp.uint32)],
        scratch_shapes=[pltpu.VMEM((tm + 2 * HALO, POOL_WIDTH), F32),
                        pltpu.VMEM((SUBLANES - 1, tm + 2 * HALO - SUBLANES, POOL_WIDTH), F32)],
        compiler_params=_cparams(("parallel", "parallel")),
        name="mix",
    )(att, up, up, up, uc, uc, uc, z, mod, w["pool_w"], w["pool_s"], w["conv_dw"], w["conv_b"],
      w["conv_ln_g"], w["conv_ln_b"], w["conv_pw"], w["w_out"], w["ln1_g"], w["ln1_b"])


def _select_experts(h, rw, rb):
    tm = h.shape[0]
    logits = _dot(h, rw)
    scores = _sigmoid(logits.T[0:N_EXPERTS, :])
    sel = scores + rb
    neg = -jnp.inf

    row8 = lax.broadcasted_iota(jnp.int32, (GROUP_SIZE, tm), 0)
    gscore = []
    for g in range(N_GROUPS):
        xg = sel[g * GROUP_SIZE:(g + 1) * GROUP_SIZE, :]
        m1 = jnp.max(xg, axis=0, keepdims=True)
        i1 = jnp.min(jnp.where(xg == m1, row8, GROUP_SIZE), axis=0, keepdims=True)
        m2 = jnp.max(jnp.where(row8 == i1, neg, xg), axis=0, keepdims=True)
        gscore.append(m1 + m2)
    masked = []
    for g in range(N_GROUPS):
        ahead = jnp.zeros((1, tm), F32)
        for g2 in range(N_GROUPS):
            if g2 < g:
                ahead = ahead + jnp.where(gscore[g2] >= gscore[g], 1.0, 0.0)
            elif g2 > g:
                ahead = ahead + jnp.where(gscore[g2] > gscore[g], 1.0, 0.0)
        keep = ahead < TOPK_GROUPS
        masked.append(jnp.where(keep, sel[g * GROUP_SIZE:(g + 1) * GROUP_SIZE, :], neg))
    masked = jnp.concatenate(masked, axis=0)

    row = lax.broadcasted_iota(jnp.int32, (N_EXPERTS, tm), 0)
    chosen = jnp.zeros((N_EXPERTS, tm), F32)
    hits = []
    for _ in range(TOP_K):
        m = jnp.max(masked, axis=0, keepdims=True)
        idx = jnp.min(jnp.where(masked == m, row, N_EXPERTS), axis=0, keepdims=True)
        hit = row == idx
        hits.append(hit)
        chosen = jnp.where(hit, 1.0, chosen)
        masked = jnp.where(hit, neg, masked)
    w = chosen * scores
    gates = w / jnp.sum(w, axis=0, keepdims=True) * ROUTED_SCALE
    return gates, chosen, hits


def _lanes_to_rows(rows):
    tm = rows[0].shape[1]
    pad = jnp.zeros((LANES - len(rows), tm), F32)
    return jnp.concatenate(rows + [pad], axis=0).T


def _route_kernel(h_ref, rw_ref, rb_ref, g_ref):
    gates, _, _ = _select_experts(h_ref[...], rw_ref[...], rb_ref[...])
    tm = gates.shape[1]
    pad = jnp.zeros((LANES - N_EXPERTS, tm), F32)
    g_ref[...] = jnp.concatenate([gates, pad], axis=0).T


def _route_sorted_kernel(h_ref, rw_ref, rb_ref, g4_ref, eidx_ref, rank_ref, cnt_ref, base_ref):
    @pl.when(pl.program_id(0) == 0)
    def _():
        base_ref[...] = jnp.zeros_like(base_ref)

    gates, chosen, hits = _select_experts(h_ref[...], rw_ref[...], rb_ref[...])
    tm = gates.shape[1]
    ii = lax.broadcasted_iota(jnp.int32, (tm, tm), 0)
    jj = lax.broadcasted_iota(jnp.int32, (tm, tm), 1)
    upper = jnp.where(ii <= jj, 1.0, 0.0).astype(BF16)
    prefix = _dot(chosen.astype(BF16), upper)
    rank_full = base_ref[...] + prefix - 1.0
    row = lax.broadcasted_iota(jnp.int32, (N_EXPERTS, tm), 0).astype(F32)
    pick = lambda hit, val: jnp.sum(jnp.where(hit, val, 0.0), axis=0, keepdims=True)
    g4_ref[...] = _lanes_to_rows([pick(hit, gates) for hit in hits])
    eidx_ref[...] = jnp.concatenate([pick(hit, row) for hit in hits], axis=0).astype(jnp.int32)
    rank_ref[...] = jnp.concatenate([pick(hit, rank_full) for hit in hits], axis=0).astype(jnp.int32)
    base_ref[...] += jnp.sum(chosen, axis=1, keepdims=True)
    cnt_ref[...] = jnp.broadcast_to(base_ref[...], cnt_ref.shape)


def _route(h, rw, rb, tm):
    t, d = h.shape
    return pl.pallas_call(
        _route_kernel,
        grid=(t // tm,),
        in_specs=[
            pl.BlockSpec((tm, d), lambda i: (i, 0)),
            pl.BlockSpec((d, LANES), lambda i: (0, 0)),
            pl.BlockSpec((N_EXPERTS, 1), lambda i: (0, 0)),
        ],
        out_specs=pl.BlockSpec((tm, LANES), lambda i: (i, 0)),
        out_shape=jax.ShapeDtypeStruct((t, LANES), F32),
        compiler_params=_cparams(("parallel",)),
        name="route",
    )(h, rw, rb)


def _route_sorted(h, rw, rb, tm):
    t, d = h.shape
    pick = pl.BlockSpec((TOP_K, tm), lambda i: (0, i))
    return pl.pallas_call(
        _route_sorted_kernel,
        grid=(t // tm,),
        in_specs=[
            pl.BlockSpec((tm, d), lambda i: (i, 0)),
            pl.BlockSpec((d, LANES), lambda i: (0, 0)),
            pl.BlockSpec((N_EXPERTS, 1), lambda i: (0, 0)),
        ],
        out_specs=[pl.BlockSpec((tm, LANES), lambda i: (i, 0)), pick, pick,
                   pl.BlockSpec((N_EXPERTS, LANES), lambda i: (0, 0))],
        out_shape=[jax.ShapeDtypeStruct((t, LANES), F32),
                   jax.ShapeDtypeStruct((TOP_K, t), jnp.int32),
                   jax.ShapeDtypeStruct((TOP_K, t), jnp.int32),
                   jax.ShapeDtypeStruct((N_EXPERTS, LANES), F32)],
        scratch_shapes=[pltpu.VMEM((N_EXPERTS, 1), F32)],
        compiler_params=_cparams(("arbitrary",)),
        name="route_sorted",
    )(h, rw, rb)


def _swiglu_hidden(x, wg, wu):
    hg = _dot(x, wg)
    return hg * _sigmoid(hg) * _dot(x, wu)


ROW_TILE = 512


def _pack_rows(x):
    n = x.shape[1] // 2
    bits = lax.bitcast_convert_type(x.astype(BF16).astype(F32), jnp.uint32)
    return bits[:, :n] | (bits[:, n:] >> 16)


def _unpack_rows(w):
    hi = lax.bitcast_convert_type(w & jnp.uint32(0xFFFF0000), F32)
    lo = lax.bitcast_convert_type(w << 16, F32)
    return jnp.concatenate([hi, lo], axis=1)


def _row_copies(pos_ref, tm, make_copy):
    def body(t, carry):
        for k in range(TOP_K):
            make_copy(k, t, pos_ref[0, 0, k * tm + t]).start()
        return carry
    lax.fori_loop(0, tm, body, 0, unroll=8)


def _dispatch_kernel(pos_ref, hp_ref, xs_in_ref, xs_ref, sem):
    del xs_in_ref
    tm = hp_ref.shape[0]
    _row_copies(pos_ref, tm, lambda k, t, p: pltpu.make_async_copy(
        hp_ref.at[pl.ds(t, 1), :], xs_ref.at[pl.ds(p, 1), :], sem))
    n = TOP_K * tm
    pltpu.make_async_copy(xs_ref.at[pl.ds(0, n), :], xs_ref.at[pl.ds(0, n), :], sem).wait()


def _dispatch(pos_tiles, hp, n_rows, tm):
    t, half = hp.shape
    xs0 = jnp.zeros((n_rows, half), jnp.uint32)
    return pl.pallas_call(
        _dispatch_kernel,
        grid=(t // tm,),
        in_specs=[
            pl.BlockSpec((1, 1, TOP_K * tm), lambda i: (i, 0, 0), memory_space=pltpu.SMEM),
            pl.BlockSpec((tm, half), lambda i: (i, 0)),
            pl.BlockSpec(memory_space=pl.ANY),
        ],
        out_specs=pl.BlockSpec(memory_space=pl.ANY),
        out_shape=jax.ShapeDtypeStruct((n_rows, half), jnp.uint32),
        scratch_shapes=[pltpu.SemaphoreType.DMA(())],
        input_output_aliases={2: 0},
        compiler_params=_cparams(("arbitrary",)),
        name="dispatch",
    )(pos_tiles, hp, xs0)


def _experts_kernel(te_ref, nu_ref, xs_ref, wg_ref, wu_ref, wd_ref, ys_ref):
    del te_ref
    i = pl.program_id(0)

    @pl.when(i < nu_ref[0])
    def _():
        x = _unpack_rows(xs_ref[...]).astype(BF16)
        a = _swiglu_hidden(x, wg_ref[0, 0], wu_ref[0, 0])
        ys_ref[...] = _pack_rows(_dot(a.astype(BF16), wd_ref[0, 0]))

    @pl.when(i >= nu_ref[0])
    def _():
        ys_ref[...] = jnp.zeros_like(ys_ref)


def _experts(tile_expert, n_used, xs, l, ew):
    wg, wu, wd = ew
    d, de = wg.shape[2], wg.shape[3]
    n_rows, half = xs.shape
    rows = pl.BlockSpec((ROW_TILE, half), lambda i, te, nu: (i, 0))
    return pl.pallas_call(
        _experts_kernel,
        grid_spec=pltpu.PrefetchScalarGridSpec(
            num_scalar_prefetch=2,
            grid=(n_rows // ROW_TILE,),
            in_specs=[
                rows,
                pl.BlockSpec((1, 1, d, de), lambda i, te, nu: (l, te[i], 0, 0)),
                pl.BlockSpec((1, 1, d, de), lambda i, te, nu: (l, te[i], 0, 0)),
                pl.BlockSpec((1, 1, de, d), lambda i, te, nu: (l, te[i], 0, 0)),
            ],
            out_specs=rows,
        ),
        out_shape=jax.ShapeDtypeStruct((n_rows, half), jnp.uint32),
        compiler_params=_cparams(("arbitrary",)),
        name="experts",
    )(tile_expert, n_used, xs, wg, wu, wd)


def _combine_kernel(pos_ref, g4_ref, h_ref, swg_ref, swu_ref, swd_ref, z1_ref, mod_ref, l2g_ref,
                    l2b_ref, ys_ref, o_ref, buf_ref, sem, *, alpha):
    tm = h_ref.shape[0]
    _row_copies(pos_ref, tm, lambda k, t, p: pltpu.make_async_copy(
        ys_ref.at[pl.ds(p, 1), :], buf_ref.at[k, pl.ds(t, 1), :], sem))
    a = _swiglu_hidden(h_ref[...], swg_ref[...], swu_ref[...])
    acc = _dot(a.astype(BF16), swd_ref[...])
    for k in range(TOP_K):
        pltpu.make_async_copy(ys_ref.at[pl.ds(0, tm), :], buf_ref.at[k], sem).wait()
    for k in range(TOP_K):
        acc = acc + g4_ref[:, k:k + 1] * _unpack_rows(buf_ref[k])
    gate = mod_ref[0, 5:6, :]
    o_ref[...] = _ln(alpha * z1_ref[...] + gate * acc) * l2g_ref[...] + l2b_ref[...]


def _combine(pos_tiles, g4, h, ys, w, z1, mod, tm, tiles_per_batch, alpha):
    t, d = h.shape
    de = w["sh_wg"].shape[1]
    half = ys.shape[1]
    full = lambda shape: pl.BlockSpec(shape, lambda i: (0,) * len(shape))
    body = functools.partial(_combine_kernel, alpha=alpha)
    return pl.pallas_call(
        body,
        grid=(t // tm,),
        in_specs=[
            pl.BlockSpec((1, 1, TOP_K * tm), lambda i: (i, 0, 0), memory_space=pltpu.SMEM),
            pl.BlockSpec((tm, LANES), lambda i: (i, 0)),
            pl.BlockSpec((tm, d), lambda i: (i, 0)),
            full((d, de)), full((d, de)), full((de, d)),
            pl.BlockSpec((tm, d), lambda i: (i, 0)),
            pl.BlockSpec((1, 6, d), lambda i: (i // tiles_per_batch, 0, 0)),
            full((1, d)), full((1, d)),
            pl.BlockSpec(memory_space=pl.ANY),
        ],
        out_specs=pl.BlockSpec((tm, d), lambda i: (i, 0)),
        out_shape=jax.ShapeDtypeStruct((t, d), F32),
        scratch_shapes=[pltpu.VMEM((TOP_K, tm, half), jnp.uint32), pltpu.SemaphoreType.DMA(())],
        compiler_params=_cparams(("arbitrary",)),
        name="combine",
    )(pos_tiles, g4, h, w["sh_wg"], w["sh_wu"], w["sh_wd"], z1, mod, w["ln2_g"], w["ln2_b"], ys)


def _moe_sorted(h, hp, l, ew, w, z1, mod, tm, tiles_per_batch, alpha):
    t = h.shape[0]
    g4, eidx, rank, cnt = _route_sorted(h, w["router_w"], w["router_b"], tm)
    counts = cnt[:, 0].astype(jnp.int32)
    sizes = (counts + ROW_TILE - 1) // ROW_TILE * ROW_TILE
    ends = jnp.cumsum(sizes)
    starts = ends - sizes
    n_tiles = t * TOP_K // ROW_TILE + N_EXPERTS
    n_used = (ends[-1] // ROW_TILE).reshape(1)
    tile_expert = jnp.minimum(
        jnp.sum(jnp.arange(n_tiles)[:, None] >= (ends // ROW_TILE)[None, :], axis=1), N_EXPERTS - 1
    ).astype(jnp.int32)
    onehot = eidx[:, :, None] == jnp.arange(N_EXPERTS)[None, None, :]
    pos = rank + jnp.sum(jnp.where(onehot, starts[None, None, :], 0), axis=-1)
    pos_tiles = pos.reshape(TOP_K, t // tm, tm).transpose(1, 0, 2).reshape(t // tm, 1, TOP_K * tm)
    xs = _dispatch(pos_tiles, hp, n_tiles * ROW_TILE, tm)
    ys = _experts(tile_expert, n_used, xs, l, ew)
    return _combine(pos_tiles, g4, h, ys, w, z1, mod, tm, tiles_per_batch, alpha)


def _moe_kernel(h_ref, g_ref, wg_ref, wu_ref, wd_ref, swg_ref, swu_ref, swd_ref, z1_ref, mod_ref,
                l2g_ref, l2b_ref, o_ref, acc_ref, *, alpha):
    e = pl.program_id(1)
    x = h_ref[...]

    @pl.when(e == 0)
    def _():
        a = _swiglu_hidden(x, swg_ref[...], swu_ref[...])
        acc_ref[...] = _dot(a.astype(BF16), swd_ref[...])

    lane = lax.broadcasted_iota(jnp.int32, g_ref.shape, 1)
    gcol = jnp.sum(jnp.where(lane == e, g_ref[...], 0.0), axis=1, keepdims=True)
    a = _swiglu_hidden(x, wg_ref[0, 0], wu_ref[0, 0]) * gcol
    acc_ref[...] += _dot(a.astype(BF16), wd_ref[0, 0])

    @pl.when(e == pl.num_programs(1) - 1)
    def _():
        gate = mod_ref[0, 5:6, :]
        o_ref[...] = _ln(alpha * z1_ref[...] + gate * acc_ref[...]) * l2g_ref[...] + l2b_ref[...]


def _moe(h, gates, l, ew, w, z1, mod, tm, tiles_per_batch, alpha):
    t, d = h.shape
    wg, wu, wd = ew
    ne, de = wg.shape[1], wg.shape[3]
    body = functools.partial(_moe_kernel, alpha=alpha)
    return pl.pallas_call(
        body,
        grid=(t // tm, ne),
        in_specs=[
            pl.BlockSpec((tm, d), lambda i, e: (i, 0)),
            pl.BlockSpec((tm, LANES), lambda i, e: (i, 0)),
            pl.BlockSpec((1, 1, d, de), lambda i, e: (l, e, 0, 0)),
            pl.BlockSpec((1, 1, d, de), lambda i, e: (l, e, 0, 0)),
            pl.BlockSpec((1, 1, de, d), lambda i, e: (l, e, 0, 0)),
            pl.BlockSpec((d, de), lambda i, e: (0, 0)),
            pl.BlockSpec((d, de), lambda i, e: (0, 0)),
            pl.BlockSpec((de, d), lambda i, e: (0, 0)),
            pl.BlockSpec((tm, d), lambda i, e: (i, 0)),
            pl.BlockSpec((1, 6, d), lambda i, e: (i // tiles_per_batch, 0, 0)),
            pl.BlockSpec((1, d), lambda i, e: (0, 0)),
            pl.BlockSpec((1, d), lambda i, e: (0, 0)),
        ],
        out_specs=pl.BlockSpec((tm, d), lambda i, e: (i, 0)),
        out_shape=jax.ShapeDtypeStruct((t, d), F32),
        scratch_shapes=[pltpu.VMEM((tm, d), F32)],
        compiler_params=_cparams(("parallel", "arbitrary")),
        name="moe",
    )(h, gates, wg, wu, wd, w["sh_wg"], w["sh_wu"], w["sh_wd"], z1, mod, w["ln2_g"], w["ln2_b"])


def _rope_swap(r):
    r4 = r.reshape(r.shape[:-1] + (2, 2, ROPE_AXIS // 2))
    return jnp.stack([-r4[..., 1, :], r4[..., 0, :]], axis=-2).reshape(r.shape)


def _prep_layer(l, w_in, q_norm_g, w_uq, kv_norm_g, w_ukv, pool_w, pool_scale, conv_dw, conv_b,
                conv_ln_g, conv_ln_b, conv_pw, w_out, ln1_g, ln1_b, router_w, router_bias,
                exp_wg, exp_wu, exp_wd, sh_wg, sh_wu, sh_wd, ln2_g, ln2_b):
    d = w_in.shape[1]
    wi = w_in[l]
    w_in_p = jnp.concatenate([
        wi[:, 0:OFF_KR], wi[:, OFF_KR:OFF_POOL], jnp.zeros((d, PC_POOL - PC_KR - QK_ROPE), F32),
        wi[:, OFF_POOL:]], axis=1).astype(BF16)

    uq = w_uq[l].reshape(Q_LORA, MLA_HEADS, QK_NOPE + QK_ROPE)
    nope, rope = uq[..., :QK_NOPE], uq[..., QK_NOPE:]
    wqat = jnp.concatenate([nope, rope, rope], axis=-1).reshape(Q_LORA, QK_WIDTH).T.astype(BF16)
    wqbt = jnp.concatenate([jnp.zeros_like(nope), jnp.zeros_like(rope), _rope_swap(rope)],
                           axis=-1).reshape(Q_LORA, QK_WIDTH).T.astype(BF16)

    ukv = w_ukv[l].reshape(KV_LORA, MLA_HEADS, QK_NOPE + V_DIM)
    kn, vv = ukv[..., :QK_NOPE], ukv[..., QK_NOPE:]
    z32 = jnp.zeros((KV_LORA, MLA_HEADS, QK_ROPE), F32)
    top = jnp.concatenate([kn, z32, z32], axis=-1)
    eye = jnp.broadcast_to(jnp.eye(QK_ROPE, dtype=F32)[:, None, :], (QK_ROPE, MLA_HEADS, QK_ROPE))
    zr = jnp.zeros((QK_ROPE, MLA_HEADS, QK_ROPE), F32)
    zn = jnp.zeros((QK_ROPE, MLA_HEADS, QK_NOPE), F32)
    padrows = jnp.zeros((KV_LORA - QK_ROPE, MLA_HEADS, HEAD_PAD), F32)

    def kmat(kr_rows):
        return jnp.concatenate([top, kr_rows, padrows], axis=0).reshape(2 * KV_LORA, QK_WIDTH).astype(BF16)

    wka_lat = kmat(jnp.concatenate([zn, zr, eye], axis=-1))
    wkb_lat = kmat(jnp.concatenate([zn, zr, _rope_swap(eye)], axis=-1))
    wka_ctx = kmat(jnp.concatenate([zn, eye, zr], axis=-1))
    wkb_ctx = jnp.zeros((2 * KV_LORA, QK_WIDTH), BF16)

    zv = jnp.zeros_like(vv)
    even = (jnp.arange(MLA_HEADS) % 2 == 0)[None, :, None]
    wv = jnp.where(even, jnp.concatenate([vv, zv], -1), jnp.concatenate([zv, vv], -1))
    wvt = wv.reshape(KV_LORA, QK_WIDTH).T.astype(BF16)
    lane = jnp.arange(QK_WIDTH) % (2 * HEAD_PAD)
    vbt = ((lane == V_DIM) | (lane == HEAD_PAD)).astype(F32)[:, None]

    pw = jnp.zeros((POOL_WIDTH, POOL_WIDTH), F32)
    for gi in range(len(POOL_WINDOWS)):
        pw = pw.at[gi * POOL_GROUP:(gi + 1) * POOL_GROUP, gi * POOL_GROUP:(gi + 1) * POOL_GROUP].set(pool_w[l, gi])

    common = dict(
        w_in=w_in_p, q_g=q_norm_g[l][None], kv_g=kv_norm_g[l][None], wqat=wqat, wqbt=wqbt, wvt=wvt, vbt=vbt,
        pool_w=pw.astype(BF16), pool_s=pool_scale[l][None],
        conv_dw=jnp.concatenate([conv_dw[l], jnp.zeros((1, CONV_WIDTH), F32)], axis=0),
        conv_b=conv_b[l][None], conv_ln_g=conv_ln_g[l][None], conv_ln_b=conv_ln_b[l][None],
        conv_pw=conv_pw[l].astype(BF16), w_out=w_out[l].astype(BF16),
        ln1_g=ln1_g[l][None], ln1_b=ln1_b[l][None],
        router_w=jnp.pad(router_w[l], ((0, 0), (0, LANES - N_EXPERTS))).astype(BF16),
        router_b=router_bias[l][:, None],
        sh_wg=sh_wg[l].astype(BF16), sh_wu=sh_wu[l].astype(BF16), sh_wd=sh_wd[l].astype(BF16),
        ln2_g=ln2_g[l][None], ln2_b=ln2_b[l][None],
    )
    return dict(common, wka=wka_lat, wkb=wkb_lat), dict(common, wka=wka_ctx, wkb=wkb_ctx)


def _rope_tables(n_lat, n_ctx):
    t = jnp.arange(n_lat)
    inv = ROPE_BASE ** (-jnp.arange(0, ROPE_AXIS, 2, dtype=F32) / ROPE_AXIS)
    ang_r = (t // GRID_W).astype(F32)[:, None] * inv
    ang_c = (t % GRID_W).astype(F32)[:, None] * inv
    cos = jnp.concatenate([jnp.cos(ang_r), jnp.cos(ang_r), jnp.cos(ang_c), jnp.cos(ang_c)], axis=1)
    sin = jnp.concatenate([jnp.sin(ang_r), jnp.sin(ang_r), jnp.sin(ang_c), jnp.sin(ang_c)], axis=1)
    ck = jnp.concatenate([jnp.ones((n_lat, HEAD_PAD - QK_ROPE), F32), cos], axis=1)
    sk = jnp.concatenate([jnp.zeros((n_lat, HEAD_PAD - QK_ROPE), F32), sin], axis=1)
    qs = ATTN_SCALE * LOG2E
    lat = ((ck * qs).T, (sk * qs).T, ck, sk)
    one = jnp.ones((n_ctx, HEAD_PAD), F32)
    zero = jnp.zeros((n_ctx, HEAD_PAD), F32)
    ctx = ((one * qs).T, zero.T, one, zero)
    return lat, ctx


def _sublayers(z, mod, l, ew, w, att, up, uc, tm, tm_moe, alpha, sorted_moe):
    b, s, d = z.shape
    z1, h, hp = _mix(att, up, uc, z, mod, w, tm, alpha)
    h2, z2 = h.reshape(b * s, d), z1.reshape(b * s, d)
    if sorted_moe:
        out = _moe_sorted(h2, hp.reshape(b * s, d // 2), l, ew, w, z2, mod, tm, s // tm, alpha)
    else:
        gates = _route(h2, w["router_w"], w["router_b"], tm)
        out = _moe(h2, gates, l, ew, w, z2, mod, tm_moe, s // tm_moe, alpha)
    return out.reshape(b, s, d)


def kernel(x, c, ctx, c_ctx, ada_w, ada_b, w_in, q_norm_g, w_uq, kv_norm_g, w_ukv, pool_w, pool_scale, conv_dw, conv_b, conv_ln_g, conv_ln_b, conv_pw, w_out, ln1_g, ln1_b, router_w, router_bias, exp_wg, exp_wu, exp_wd, sh_wg, sh_wu, sh_wd, ln2_g, ln2_b):
    b, s, d = x.shape
    n_ctx = ctx.shape[1]
    depth = ada_w.shape[0]
    alpha = (2 * depth) ** 0.25
    assert b + 1 <= 8 and s % GRID_W == 0

    tm_l = min(512, s)
    tm_c = min(256, n_ctx)
    tm_moe_l = min(1024, s)
    tm_moe_c = n_ctx
    bq = min(1024, s)

    cond = jnp.concatenate([c, c_ctx[None], jnp.zeros((8 - b - 1, d), F32)], axis=0)
    mods = _ada(cond, ada_w, ada_b)
    tab_l, tab_c = _rope_tables(s, n_ctx)
    ew = (exp_wg.astype(BF16), exp_wu.astype(BF16), exp_wd.astype(BF16))

    zl, zc = x, ctx
    for l in range(depth):
        last = l == depth - 1
        w_l, w_c = _prep_layer(l, w_in, q_norm_g, w_uq, kv_norm_g, w_ukv, pool_w, pool_scale,
                               conv_dw, conv_b, conv_ln_g, conv_ln_b, conv_pw, w_out, ln1_g, ln1_b,
                               router_w, router_bias, exp_wg, exp_wu, exp_wd, sh_wg, sh_wu, sh_wd,
                               ln2_g, ln2_b)
        mod_l = mods[l, :b].reshape(b, 6, d)
        mod_c = jnp.broadcast_to(mods[l, b].reshape(1, 6, d), (b, 6, d))

        q_l, k_l, v_l, up_l, uc_l = _proj(zl, mod_l, w_l, tab_l, tm_l)
        q_c, k_c, v_c, up_c, uc_c = _proj(zc, mod_c, w_c, tab_c, tm_c)
        att_l = _attn(q_l, k_c, v_c, k_l, v_l, bq)
        zl = _sublayers(zl, mod_l, l, ew, w_l, att_l, up_l, uc_l, tm_l, tm_moe_l, alpha, True)
        if not last:
            att_c = _attn(q_c, k_c, v_c, None, None, tm_c)
            zc = _sublayers(zc, mod_c, l, ew, w_c, att_c, up_c, uc_c, tm_c, tm_moe_c, alpha, False)
    return zl
```

```python
import functools
import math

import jax
import jax.numpy as jnp
from jax import lax
from jax.experimental import pallas as pl
from jax.experimental.pallas import tpu as pltpu

GRID_W = 64
MLA_HEADS = 8
QK_NOPE = 64
QK_ROPE = 32
V_DIM = 64
Q_LORA = 256
KV_LORA = 128
ROPE_AXIS = QK_ROPE // 2
ROPE_BASE = 10000.0
ATTN_SCALE = (QK_NOPE + QK_ROPE) ** -0.5
POOL_WINDOWS = (2, 4, 8, 16)
POOL_GROUP = 64
POOL_WIDTH = POOL_GROUP * len(POOL_WINDOWS)
CONV_WIDTH = 256
CONV_K = 31
MLA_WIDTH = MLA_HEADS * V_DIM
OFF_KV = Q_LORA
OFF_KR = OFF_KV + KV_LORA
OFF_POOL = OFF_KR + QK_ROPE
OFF_CONV = OFF_POOL + POOL_WIDTH
N_EXPERTS = 32
TOP_K = 4
N_GROUPS = 4
TOPK_GROUPS = 2
GROUP_SIZE = N_EXPERTS // N_GROUPS
ROUTED_SCALE = 2.5
EPS = 1e-6

LANES = 128
HEAD_PAD = LANES
QK_WIDTH = MLA_HEADS * HEAD_PAD
HALO = 16
P_COLS = 1280
PC_Q, PC_KV, PC_KR, PC_POOL, PC_CA, PC_CG = 0, 256, 384, 512, 768, 1024
VMEM_LIMIT = 48 * 1024 * 1024
LOG2E = math.log2(math.e)

F32 = jnp.float32
BF16 = jnp.bfloat16


def _cparams(sem, flags=None):
    return pltpu.CompilerParams(dimension_semantics=sem, vmem_limit_bytes=VMEM_LIMIT, flags=flags)


def _ln(x):
    mu = jnp.mean(x, axis=-1, keepdims=True)
    xc = x - mu
    var = jnp.mean(xc * xc, axis=-1, keepdims=True)
    return xc * lax.rsqrt(var + EPS)


def _rms(x):
    return x * lax.rsqrt(jnp.mean(x * x, axis=-1, keepdims=True) + EPS)


def _sigmoid(x):
    return 1.0 / (1.0 + jnp.exp(-x))


def _dot(a, b):
    return jnp.dot(a, b, preferred_element_type=F32)


def _dot_nt(a, b):
    return lax.dot_general(a, b, (((1,), (1,)), ((), ())), preferred_element_type=F32)


def _ada_kernel(c_ref, w_ref, b_ref, o_ref):
    x = c_ref[...]
    x = x * _sigmoid(x)
    o_ref[0] = _dot(x.astype(BF16), w_ref[0].astype(BF16)) + b_ref[0]


def _ada(cond, ada_w, ada_b):
    depth, d, n = ada_w.shape
    tn = 1536
    return pl.pallas_call(
        _ada_kernel,
        grid=(depth, n // tn),
        in_specs=[
            pl.BlockSpec((8, d), lambda l, j: (0, 0)),
            pl.BlockSpec((1, d, tn), lambda l, j: (l, 0, j)),
            pl.BlockSpec((1, 1, tn), lambda l, j: (l, 0, j)),
        ],
        out_specs=pl.BlockSpec((1, 8, tn), lambda l, j: (l, 0, j)),
        out_shape=jax.ShapeDtypeStruct((depth, 8, n), F32),
        compiler_params=_cparams(("parallel", "parallel")),
        name="ada",
    )(cond, ada_w, ada_b.reshape(depth, 1, n))


def _proj_kernel(z_ref, mod_ref, win_ref, qg_ref, kvg_ref, wqat_ref, wqbt_ref, wka_ref, wkb_ref,
                 wvt_ref, vbt_ref, cqt_ref, sqt_ref, ck_ref, sk_ref,
                 qt_ref, k_ref, vt_ref, up_ref, uc_ref):
    z = z_ref[0]
    shift = mod_ref[0, 0:1, :]
    scale = mod_ref[0, 1:2, :]
    h = _ln(z) * (1.0 + scale) + shift
    p = _dot(h.astype(BF16), win_ref[...])
    qn = (_rms(p[:, PC_Q:PC_KV]) * qg_ref[...]).astype(BF16)
    kvn = (_rms(p[:, PC_KV:PC_KR]) * kvg_ref[...]).astype(BF16)
    xk = jnp.concatenate([kvn, p[:, PC_KR:PC_POOL].astype(BF16)], axis=1)
    qat = _dot_nt(wqat_ref[...], qn)
    qbt = _dot_nt(wqbt_ref[...], qn)
    ka = _dot(xk, wka_ref[...])
    kb = _dot(xk, wkb_ref[...])
    cqt, sqt, ck, sk = cqt_ref[...], sqt_ref[...], ck_ref[...], sk_ref[...]
    for hd in range(MLA_HEADS):
        sl = slice(hd * HEAD_PAD, (hd + 1) * HEAD_PAD)
        qt_ref[0, sl, :] = (qat[sl, :] * cqt + qbt[sl, :] * sqt).astype(BF16)
        k_ref[0, :, sl] = (ka[:, sl] * ck + kb[:, sl] * sk).astype(BF16)
    vt_ref[0, 0] = (_dot_nt(wvt_ref[...], kvn) + vbt_ref[...]).astype(BF16)
    up_ref[0] = p[:, PC_POOL:PC_CA]
    uc_ref[0] = p[:, PC_CA:PC_CG] * _sigmoid(p[:, PC_CG:P_COLS])


def _proj(z, mod, w, tabs, tm):
    b, s, d = z.shape
    full = lambda shape: pl.BlockSpec(shape, lambda i, j: (0,) * len(shape))
    tok = lambda width: pl.BlockSpec((1, tm, width), lambda i, j: (i, j, 0))
    tab = pl.BlockSpec((tm, HEAD_PAD), lambda i, j: (j, 0))
    tab_t = pl.BlockSpec((HEAD_PAD, tm), lambda i, j: (0, j))
    return pl.pallas_call(
        _proj_kernel,
        grid=(b, s // tm),
        in_specs=[
            tok(d),
            pl.BlockSpec((1, 6, d), lambda i, j: (i, 0, 0)),
            full((d, P_COLS)), full((1, Q_LORA)), full((1, KV_LORA)),
            full((QK_WIDTH, Q_LORA)), full((QK_WIDTH, Q_LORA)),
            full((2 * KV_LORA, QK_WIDTH)), full((2 * KV_LORA, QK_WIDTH)),
            full((QK_WIDTH, KV_LORA)), full((QK_WIDTH, 1)),
            tab_t, tab_t, tab, tab,
        ],
        out_specs=[
            pl.BlockSpec((1, QK_WIDTH, tm), lambda i, j: (i, 0, j)),
            tok(QK_WIDTH),
            pl.BlockSpec((1, 1, QK_WIDTH, tm), lambda i, j: (i, j, 0, 0)),
            tok(POOL_WIDTH), tok(CONV_WIDTH)],
        out_shape=[
            jax.ShapeDtypeStruct((b, QK_WIDTH, s), BF16),
            jax.ShapeDtypeStruct((b, s, QK_WIDTH), BF16),
            jax.ShapeDtypeStruct((b, s // tm, QK_WIDTH, tm), BF16),
            jax.ShapeDtypeStruct((b, s, POOL_WIDTH), F32),
            jax.ShapeDtypeStruct((b, s, CONV_WIDTH), F32),
        ],
        compiler_params=_cparams(("parallel", "parallel")),
        name="proj",
    )(z, mod, w["w_in"], w["q_g"], w["kv_g"], w["wqat"], w["wqbt"], w["wka"], w["wkb"],
      w["wvt"], w["vbt"], *tabs)


HEADS_PER_STEP = 2
QCOLS = 256


def _attn_body(qt_ref, kc_ref, vct_ref, kl_ref, vlt_ref, o_ref, s_refs):
    heads = range(HEADS_PER_STEP)
    sls = [slice(hh * HEAD_PAD, (hh + 1) * HEAD_PAD) for hh in heads]
    qts = [qt_ref[0, sl, :] for sl in sls]
    ms, accs = [], []
    for qt, sl in zip(qts, sls):
        st = _dot(kc_ref[0, :, sl], qt)
        m = jnp.max(st, axis=0, keepdims=True)
        pt = jnp.exp2(st - m)
        ms.append(m)
        accs.append(_dot(vct_ref[0, 0, sl, :], pt.astype(BF16)))

    if kl_ref is not None:
        n, bk = vlt_ref.shape[1], vlt_ref.shape[3]
        assert n >= 2 and n % 2 == 0

        def score(c, slot):
            r0 = c * bk if isinstance(c, int) else pl.multiple_of(c * bk, bk)
            cms = []
            for hh in heads:
                st = _dot(kl_ref[0, pl.ds(r0, bk), sls[hh]], qts[hh])
                s_refs[slot][hh] = st
                cms.append(jnp.max(st, axis=0, keepdims=True))
            return cms

        def half(c, cur, nxt, ms, cms, accs, with_score=True):
            m_new = [jnp.maximum(ms[hh], cms[hh]) for hh in heads]
            alphas = [jnp.exp2(ms[hh] - m_new[hh]) for hh in heads]
            if with_score:
                cms = score(c + 1, nxt)
            out = []
            for hh in heads:
                cols = []
                for g in range(0, qts[hh].shape[1], QCOLS):
                    gs = slice(g, g + QCOLS)
                    pt = jnp.exp2(s_refs[cur][hh, :, gs] - m_new[hh][:, gs]).astype(BF16)
                    cols.append(accs[hh][:, gs] * alphas[hh][:, gs]
                                + _dot(vlt_ref[0, c, sls[hh], :], pt))
                out.append(jnp.concatenate(cols, axis=1))
            return m_new, cms, out

        cms = score(0, 0)

        def pair(t, carry):
            ms, cms, accs = [list(x) for x in carry]
            c = 2 * t
            ms, cms, accs = half(c, 0, 1, ms, cms, accs)
            ms, cms, accs = half(c + 1, 1, 0, ms, cms, accs)
            return tuple(tuple(x) for x in (ms, cms, accs))

        carry = tuple(tuple(x) for x in (ms, cms, accs))
        carry = lax.fori_loop(0, (n - 2) // 2, pair, carry)
        ms, cms, accs = [list(x) for x in carry]
        ms, cms, accs = half(n - 2, 0, 1, ms, cms, accs)
        ms, cms, accs = half(n - 1, 1, 0, ms, cms, accs, with_score=False)

    acc_even, acc_odd = accs
    row = lax.broadcasted_iota(jnp.int32, acc_even.shape, 0)
    ot = jnp.where(row < V_DIM, acc_even / acc_even[V_DIM:V_DIM + 1, :], acc_odd / acc_odd[0:1, :])
    o_ref[0] = ot.T.astype(o_ref.dtype)


def _attn_kernel_full(qt_ref, kc_ref, vct_ref, kl_ref, vlt_ref, o_ref, s0_ref, s1_ref):
    _attn_body(qt_ref, kc_ref, vct_ref, kl_ref, vlt_ref, o_ref, (s0_ref, s1_ref))


def _attn_kernel_ctx(qt_ref, kc_ref, vct_ref, o_ref):
    _attn_body(qt_ref, kc_ref, vct_ref, None, None, o_ref, None)


def _attn(qt, kc, vct, kl, vlt, bq):
    b, _, s = qt.shape
    nc = kc.shape[1]
    w2 = HEADS_PER_STEP * HEAD_PAD
    in_specs = [
        pl.BlockSpec((1, w2, bq), lambda i, h, j: (i, h, j)),
        pl.BlockSpec((1, nc, w2), lambda i, h, j: (i, 0, h)),
        pl.BlockSpec((1, 1, w2, nc), lambda i, h, j: (i, 0, h, 0)),
    ]
    args = [qt, kc, vct]
    scratch = []
    if kl is None:
        body = _attn_kernel_ctx
    else:
        bk = vlt.shape[3]
        scratch = [pltpu.VMEM((HEADS_PER_STEP, bk, bq), F32)] * 2
        in_specs += [
            pl.BlockSpec((1, kl.shape[1], w2), lambda i, h, j: (i, 0, h)),
            pl.BlockSpec((1, vlt.shape[1], w2, vlt.shape[3]), lambda i, h, j: (i, 0, h, 0)),
        ]
        args += [kl, vlt]
        body = _attn_kernel_full
    return pl.pallas_call(
        body,
        grid=(b, MLA_HEADS // HEADS_PER_STEP, s // bq),
        in_specs=in_specs,
        out_specs=pl.BlockSpec((1, bq, HEADS_PER_STEP * V_DIM), lambda i, h, j: (i, j, h)),
        out_shape=jax.ShapeDtypeStruct((b, s, MLA_WIDTH), BF16),
        scratch_shapes=scratch,
        compiler_params=_cparams(("parallel", "parallel", "arbitrary")),
        name="attn",
    )(*args)


SUBLANES = 8


def _fill_ext(ext_ref, sh_ref, main_ref, left_ref, right_ref, tm):
    j = pl.program_id(1)
    last = pl.num_programs(1) - 1
    ext_ref[0:HALO, :] = jnp.where(j > 0, left_ref[0], 0.0)
    ext_ref[HALO:HALO + tm, :] = main_ref[0]
    ext_ref[HALO + tm:HALO + tm + HALO, :] = jnp.where(j < last, right_ref[0], 0.0)
    n = sh_ref.shape[1]
    for b in range(1, SUBLANES):
        sh_ref[b - 1, :, :] = ext_ref[b:b + n, :]


def _mix_kernel(att_ref, up_ref, upl_ref, upr_ref, uc_ref, ucl_ref, ucr_ref, z_ref, mod_ref,
                poolw_ref, pools_ref, dw_ref, cb_ref, clg_ref, clb_ref, cpw_ref, wout_ref,
                l1g_ref, l1b_ref, z1_ref, h_ref, hp_ref, ext_ref, sh_ref, *, tm, seq, alpha):
    j = pl.program_id(1)

    def shifted(d):
        a, b = divmod(HALO + d, SUBLANES)
        if b == 0:
            return ext_ref[a * SUBLANES:a * SUBLANES + tm, :]
        return sh_ref[b - 1, a * SUBLANES:a * SUBLANES + tm, :]

    _fill_ext(ext_ref, sh_ref, up_ref, upl_ref, upr_ref, tm)
    x = up_ref[0]
    lane = lax.broadcasted_iota(jnp.int32, (tm, POOL_WIDTH), 1)
    t = lax.broadcasted_iota(jnp.int32, (tm, POOL_WIDTH), 0) + j * tm

    run = x + shifted(-1)
    sums = [run]
    for lo in (2, 4, 8):
        for d in list(range(-lo, -lo // 2)) + list(range(lo // 2, lo)):
            run = run + shifted(d)
        sums.append(run)
    wsum = jnp.where(lane < POOL_GROUP, sums[0],
                     jnp.where(lane < 2 * POOL_GROUP, sums[1],
                               jnp.where(lane < 3 * POOL_GROUP, sums[2], sums[3])))
    lo = jnp.where(lane < POOL_GROUP, 1,
                   jnp.where(lane < 2 * POOL_GROUP, 2, jnp.where(lane < 3 * POOL_GROUP, 4, 8)))
    cnt = jnp.minimum(t + lo, seq) - jnp.maximum(t - lo, 0)
    diff = wsum / cnt.astype(F32) - x
    pooled = _dot(diff.astype(BF16), poolw_ref[...]) * pools_ref[...]

    _fill_ext(ext_ref, sh_ref, uc_ref, ucl_ref, ucr_ref, tm)
    acc = jnp.zeros((tm, CONV_WIDTH), F32) + cb_ref[...]
    for k in range(CONV_K):
        acc = acc + dw_ref[k:k + 1, :] * shifted(k - CONV_K // 2)
    y = _ln(acc) * clg_ref[...] + clb_ref[...]
    y = y * _sigmoid(y)
    conv = _dot(y.astype(BF16), cpw_ref[...])

    cat = jnp.concatenate([att_ref[0], pooled.astype(BF16), conv.astype(BF16)], axis=1)
    out = _dot(cat, wout_ref[...])
    gate = mod_ref[0, 2:3, :]
    z1 = _ln(alpha * z_ref[0] + gate * out) * l1g_ref[...] + l1b_ref[...]
    z1_ref[0] = z1
    h = _ln(z1) * (1.0 + mod_ref[0, 4:5, :]) + mod_ref[0, 3:4, :]
    h_ref[0] = h.astype(BF16)
    hp_ref[0] = _pack_rows(h)


def _mix(att, up, uc, z, mod, w, tm, alpha):
    b, s, d = z.shape
    hb = tm // HALO
    nhb = s // HALO
    full = lambda shape: pl.BlockSpec(shape, lambda i, j: (0,) * len(shape))
    tok = lambda width: pl.BlockSpec((1, tm, width), lambda i, j: (i, j, 0))
    left = lambda width: pl.BlockSpec((1, HALO, width),
                                      lambda i, j: (i, jnp.maximum(j * hb - 1, 0), 0))
    right = lambda width: pl.BlockSpec((1, HALO, width),
                                       lambda i, j: (i, jnp.minimum((j + 1) * hb, nhb - 1), 0))
    body = functools.partial(_mix_kernel, tm=tm, seq=s, alpha=alpha)
    return pl.pallas_call(
        body,
        grid=(b, s // tm),
        in_specs=[
            tok(MLA_WIDTH),
            tok(POOL_WIDTH), left(POOL_WIDTH), right(POOL_WIDTH),
            tok(CONV_WIDTH), left(CONV_WIDTH), right(CONV_WIDTH),
            tok(d),
            pl.BlockSpec((1, 6, d), lambda i, j: (i, 0, 0)),
            full((POOL_WIDTH, POOL_WIDTH)), full((1, POOL_WIDTH)),
            full((CONV_K + 1, CONV_WIDTH)), full((1, CONV_WIDTH)), full((1, CONV_WIDTH)),
            full((1, CONV_WIDTH)), full((CONV_WIDTH, CONV_WIDTH)),
            full((d, d)), full((1, d)), full((1, d)),
        ],
        out_specs=[tok(d), tok(d), tok(d // 2)],
        out_shape=[jax.ShapeDtypeStruct((b, s, d), F32), jax.ShapeDtypeStruct((b, s, d), BF16),
                   jax.ShapeDtypeStruct((b, s, d // 2), jnp.uint32)],
        scratch_shapes=[pltpu.VMEM((tm + 2 * HALO, POOL_WIDTH), F32),
                        pltpu.VMEM((SUBLANES - 1, tm + 2 * HALO - SUBLANES, POOL_WIDTH), F32)],
        compiler_params=_cparams(("parallel", "parallel")),
        name="mix",
    )(att, up, up, up, uc, uc, uc, z, mod, w["pool_w"], w["pool_s"], w["conv_dw"], w["conv_b"],
      w["conv_ln_g"], w["conv_ln_b"], w["conv_pw"], w["w_out"], w["ln1_g"], w["ln1_b"])


def _select_experts(h, rw, rb):
    tm = h.shape[0]
    logits = _dot(h, rw)
    scores = _sigmoid(logits.T[0:N_EXPERTS, :])
    sel = scores + rb
    neg = -jnp.inf

    row8 = lax.broadcasted_iota(jnp.int32, (GROUP_SIZE, tm), 0)
    gscore = []
    for g in range(N_GROUPS):
        xg = sel[g * GROUP_SIZE:(g + 1) * GROUP_SIZE, :]
        m1 = jnp.max(xg, axis=0, keepdims=True)
        i1 = jnp.min(jnp.where(xg == m1, row8, GROUP_SIZE), axis=0, keepdims=True)
        m2 = jnp.max(jnp.where(row8 == i1, neg, xg), axis=0, keepdims=True)
        gscore.append(m1 + m2)
    masked = []
    for g in range(N_GROUPS):
        ahead = jnp.zeros((1, tm), F32)
        for g2 in range(N_GROUPS):
            if g2 < g:
                ahead = ahead + jnp.where(gscore[g2] >= gscore[g], 1.0, 0.0)
            elif g2 > g:
                ahead = ahead + jnp.where(gscore[g2] > gscore[g], 1.0, 0.0)
        keep = ahead < TOPK_GROUPS
        masked.append(jnp.where(keep, sel[g * GROUP_SIZE:(g + 1) * GROUP_SIZE, :], neg))
    masked = jnp.concatenate(masked, axis=0)

    row = lax.broadcasted_iota(jnp.int32, (N_EXPERTS, tm), 0)
    chosen = jnp.zeros((N_EXPERTS, tm), F32)
    hits = []
    for _ in range(TOP_K):
        m = jnp.max(masked, axis=0, keepdims=True)
        idx = jnp.min(jnp.where(masked == m, row, N_EXPERTS), axis=0, keepdims=True)
        hit = row == idx
        hits.append(hit)
        chosen = jnp.where(hit, 1.0, chosen)
        masked = jnp.where(hit, neg, masked)
    w = chosen * scores
    gates = w / jnp.sum(w, axis=0, keepdims=True) * ROUTED_SCALE
    return gates, chosen, hits


def _lanes_to_rows(rows):
    tm = rows[0].shape[1]
    pad = jnp.zeros((LANES - len(rows), tm), F32)
    return jnp.concatenate(rows + [pad], axis=0).T


def _route_kernel(h_ref, rw_ref, rb_ref, g_ref):
    gates, _, _ = _select_experts(h_ref[...], rw_ref[...], rb_ref[...])
    tm = gates.shape[1]
    pad = jnp.zeros((LANES - N_EXPERTS, tm), F32)
    g_ref[...] = jnp.concatenate([gates, pad], axis=0).T


def _route_sorted_kernel(h_ref, rw_ref, rb_ref, g4_ref, eidx_ref, rank_ref, cnt_ref, base_ref):
    @pl.when(pl.program_id(0) == 0)
    def _():
        base_ref[...] = jnp.zeros_like(base_ref)

    gates, chosen, hits = _select_experts(h_ref[...], rw_ref[...], rb_ref[...])
    tm = gates.shape[1]
    ii = lax.broadcasted_iota(jnp.int32, (tm, tm), 0)
    jj = lax.broadcasted_iota(jnp.int32, (tm, tm), 1)
    upper = jnp.where(ii <= jj, 1.0, 0.0).astype(BF16)
    prefix = _dot(chosen.astype(BF16), upper)
    rank_full = base_ref[...] + prefix - 1.0
    row = lax.broadcasted_iota(jnp.int32, (N_EXPERTS, tm), 0).astype(F32)
    pick = lambda hit, val: jnp.sum(jnp.where(hit, val, 0.0), axis=0, keepdims=True)
    g4_ref[...] = _lanes_to_rows([pick(hit, gates) for hit in hits])
    eidx_ref[...] = jnp.concatenate([pick(hit, row) for hit in hits], axis=0).astype(jnp.int32)
    rank_ref[...] = jnp.concatenate([pick(hit, rank_full) for hit in hits], axis=0).astype(jnp.int32)
    base_ref[...] += jnp.sum(chosen, axis=1, keepdims=True)
    cnt_ref[...] = jnp.broadcast_to(base_ref[...], cnt_ref.shape)


def _route(h, rw, rb, tm):
    t, d = h.shape
    return pl.pallas_call(
        _route_kernel,
        grid=(t // tm,),
        in_specs=[
            pl.BlockSpec((tm, d), lambda i: (i, 0)),
            pl.BlockSpec((d, LANES), lambda i: (0, 0)),
            pl.BlockSpec((N_EXPERTS, 1), lambda i: (0, 0)),
        ],
        out_specs=pl.BlockSpec((tm, LANES), lambda i: (i, 0)),
        out_shape=jax.ShapeDtypeStruct((t, LANES), F32),
        compiler_params=_cparams(("parallel",)),
        name="route",
    )(h, rw, rb)


def _route_sorted(h, rw, rb, tm):
    t, d = h.shape
    pick = pl.BlockSpec((TOP_K, tm), lambda i: (0, i))
    return pl.pallas_call(
        _route_sorted_kernel,
        grid=(t // tm,),
        in_specs=[
            pl.BlockSpec((tm, d), lambda i: (i, 0)),
            pl.BlockSpec((d, LANES), lambda i: (0, 0)),
            pl.BlockSpec((N_EXPERTS, 1), lambda i: (0, 0)),
        ],
        out_specs=[pl.BlockSpec((tm, LANES), lambda i: (i, 0)), pick, pick,
                   pl.BlockSpec((N_EXPERTS, LANES), lambda i: (0, 0))],
        out_shape=[jax.ShapeDtypeStruct((t, LANES), F32),
                   jax.ShapeDtypeStruct((TOP_K, t), jnp.int32),
                   jax.ShapeDtypeStruct((TOP_K, t), jnp.int32),
                   jax.ShapeDtypeStruct((N_EXPERTS, LANES), F32)],
        scratch_shapes=[pltpu.VMEM((N_EXPERTS, 1), F32)],
        compiler_params=_cparams(("arbitrary",)),
        name="route_sorted",
    )(h, rw, rb)


def _swiglu_hidden(x, wg, wu):
    hg = _dot(x, wg)
    return hg * _sigmoid(hg) * _dot(x, wu)


ROW_TILE = 512


def _pack_rows(x):
    n = x.shape[1] // 2
    bits = lax.bitcast_convert_type(x.astype(BF16).astype(F32), jnp.uint32)
    return bits[:, :n] | (bits[:, n:] >> 16)


def _unpack_rows(w):
    hi = lax.bitcast_convert_type(w & jnp.uint32(0xFFFF0000), F32)
    lo = lax.bitcast_convert_type(w << 16, F32)
    return jnp.concatenate([hi, lo], axis=1)


def _row_copies(pos_ref, tm, make_copy):
    def body(t, carry):
        for k in range(TOP_K):
            make_copy(k, t, pos_ref[0, 0, k * tm + t]).start()
        return carry
    lax.fori_loop(0, tm, body, 0, unroll=8)


def _dispatch_kernel(pos_ref, hp_ref, xs_in_ref, xs_ref, sem):
    del xs_in_ref
    tm = hp_ref.shape[0]
    _row_copies(pos_ref, tm, lambda k, t, p: pltpu.make_async_copy(
        hp_ref.at[pl.ds(t, 1), :], xs_ref.at[pl.ds(p, 1), :], sem))
    n = TOP_K * tm
    pltpu.make_async_copy(xs_ref.at[pl.ds(0, n), :], xs_ref.at[pl.ds(0, n), :], sem).wait()


def _dispatch(pos_tiles, hp, n_rows, tm):
    t, half = hp.shape
    xs0 = jnp.zeros((n_rows, half), jnp.uint32)
    return pl.pallas_call(
        _dispatch_kernel,
        grid=(t // tm,),
        in_specs=[
            pl.BlockSpec((1, 1, TOP_K * tm), lambda i: (i, 0, 0), memory_space=pltpu.SMEM),
            pl.BlockSpec((tm, half), lambda i: (i, 0)),
            pl.BlockSpec(memory_space=pl.ANY),
        ],
        out_specs=pl.BlockSpec(memory_space=pl.ANY),
        out_shape=jax.ShapeDtypeStruct((n_rows, half), jnp.uint32),
        scratch_shapes=[pltpu.SemaphoreType.DMA(())],
        input_output_aliases={2: 0},
        compiler_params=_cparams(("arbitrary",)),
        name="dispatch",
    )(pos_tiles, hp, xs0)


def _experts_kernel(te_ref, nu_ref, xs_ref, wg_ref, wu_ref, wd_ref, ys_ref):
    del te_ref
    i = pl.program_id(0)

    @pl.when(i < nu_ref[0])
    def _():
        x = _unpack_rows(xs_ref[...]).astype(BF16)
        a = _swiglu_hidden(x, wg_ref[0, 0], wu_ref[0, 0])
        ys_ref[...] = _pack_rows(_dot(a.astype(BF16), wd_ref[0, 0]))

    @pl.when(i >= nu_ref[0])
    def _():
        ys_ref[...] = jnp.zeros_like(ys_ref)


def _experts(tile_expert, n_used, xs, l, ew):
    wg, wu, wd = ew
    d, de = wg.shape[2], wg.shape[3]
    n_rows, half = xs.shape
    rows = pl.BlockSpec((ROW_TILE, half), lambda i, te, nu: (i, 0))
    return pl.pallas_call(
        _experts_kernel,
        grid_spec=pltpu.PrefetchScalarGridSpec(
            num_scalar_prefetch=2,
            grid=(n_rows // ROW_TILE,),
            in_specs=[
                rows,
                pl.BlockSpec((1, 1, d, de), lambda i, te, nu: (l, te[i], 0, 0)),
                pl.BlockSpec((1, 1, d, de), lambda i, te, nu: (l, te[i], 0, 0)),
                pl.BlockSpec((1, 1, de, d), lambda i, te, nu: (l, te[i], 0, 0)),
            ],
            out_specs=rows,
        ),
        out_shape=jax.ShapeDtypeStruct((n_rows, half), jnp.uint32),
        compiler_params=_cparams(("arbitrary",)),
        name="experts",
    )(tile_expert, n_used, xs, wg, wu, wd)


def _combine_kernel(pos_ref, g4_ref, h_ref, swg_ref, swu_ref, swd_ref, z1_ref, mod_ref, l2g_ref,
                    l2b_ref, ys_ref, o_ref, buf_ref, sem, *, alpha):
    tm = h_ref.shape[0]
    _row_copies(pos_ref, tm, lambda k, t, p: pltpu.make_async_copy(
        ys_ref.at[pl.ds(p, 1), :], buf_ref.at[k, pl.ds(t, 1), :], sem))
    a = _swiglu_hidden(h_ref[...], swg_ref[...], swu_ref[...])
    acc = _dot(a.astype(BF16), swd_ref[...])
    for k in range(TOP_K):
        pltpu.make_async_copy(ys_ref.at[pl.ds(0, tm), :], buf_ref.at[k], sem).wait()
    for k in range(TOP_K):
        acc = acc + g4_ref[:, k:k + 1] * _unpack_rows(buf_ref[k])
    gate = mod_ref[0, 5:6, :]
    o_ref[...] = _ln(alpha * z1_ref[...] + gate * acc) * l2g_ref[...] + l2b_ref[...]


def _combine(pos_tiles, g4, h, ys, w, z1, mod, tm, tiles_per_batch, alpha):
    t, d = h.shape
    de = w["sh_wg"].shape[1]
    half = ys.shape[1]
    full = lambda shape: pl.BlockSpec(shape, lambda i: (0,) * len(shape))
    body = functools.partial(_combine_kernel, alpha=alpha)
    return pl.pallas_call(
        body,
        grid=(t // tm,),
        in_specs=[
            pl.BlockSpec((1, 1, TOP_K * tm), lambda i: (i, 0, 0), memory_space=pltpu.SMEM),
            pl.BlockSpec((tm, LANES), lambda i: (i, 0)),
            pl.BlockSpec((tm, d), lambda i: (i, 0)),
            full((d, de)), full((d, de)), full((de, d)),
            pl.BlockSpec((tm, d), lambda i: (i, 0)),
            pl.BlockSpec((1, 6, d), lambda i: (i // tiles_per_batch, 0, 0)),
            full((1, d)), full((1, d)),
            pl.BlockSpec(memory_space=pl.ANY),
        ],
        out_specs=pl.BlockSpec((tm, d), lambda i: (i, 0)),
        out_shape=jax.ShapeDtypeStruct((t, d), F32),
        scratch_shapes=[pltpu.VMEM((TOP_K, tm, half), jnp.uint32), pltpu.SemaphoreType.DMA(())],
        compiler_params=_cparams(("arbitrary",)),
        name="combine",
    )(pos_tiles, g4, h, w["sh_wg"], w["sh_wu"], w["sh_wd"], z1, mod, w["ln2_g"], w["ln2_b"], ys)


def _moe_sorted(h, hp, l, ew, w, z1, mod, tm, tiles_per_batch, alpha):
    t = h.shape[0]
    g4, eidx, rank, cnt = _route_sorted(h, w["router_w"], w["router_b"], tm)
    counts = cnt[:, 0].astype(jnp.int32)
    sizes = (counts + ROW_TILE - 1) // ROW_TILE * ROW_TILE
    ends = jnp.cumsum(sizes)
    starts = ends - sizes
    n_tiles = t * TOP_K // ROW_TILE + N_EXPERTS
    n_used = (ends[-1] // ROW_TILE).reshape(1)
    tile_expert = jnp.minimum(
        jnp.sum(jnp.arange(n_tiles)[:, None] >= (ends // ROW_TILE)[None, :], axis=1), N_EXPERTS - 1
    ).astype(jnp.int32)
    onehot = eidx[:, :, None] == jnp.arange(N_EXPERTS)[None, None, :]
    pos = rank + jnp.sum(jnp.where(onehot, starts[None, None, :], 0), axis=-1)
    pos_tiles = pos.reshape(TOP_K, t // tm, tm).transpose(1, 0, 2).reshape(t // tm, 1, TOP_K * tm)
    xs = _dispatch(pos_tiles, hp, n_tiles * ROW_TILE, tm)
    ys = _experts(tile_expert, n_used, xs, l, ew)
    return _combine(pos_tiles, g4, h, ys, w, z1, mod, tm, tiles_per_batch, alpha)


def _moe_kernel(h_ref, g_ref, wg_ref, wu_ref, wd_ref, swg_ref, swu_ref, swd_ref, z1_ref, mod_ref,
                l2g_ref, l2b_ref, o_ref, acc_ref, *, alpha):
    e = pl.program_id(1)
    x = h_ref[...]

    @pl.when(e == 0)
    def _():
        a = _swiglu_hidden(x, swg_ref[...], swu_ref[...])
        acc_ref[...] = _dot(a.astype(BF16), swd_ref[...])

    lane = lax.broadcasted_iota(jnp.int32, g_ref.shape, 1)
    gcol = jnp.sum(jnp.where(lane == e, g_ref[...], 0.0), axis=1, keepdims=True)
    a = _swiglu_hidden(x, wg_ref[0, 0], wu_ref[0, 0]) * gcol
    acc_ref[...] += _dot(a.astype(BF16), wd_ref[0, 0])

    @pl.when(e == pl.num_programs(1) - 1)
    def _():
        gate = mod_ref[0, 5:6, :]
        o_ref[...] = _ln(alpha * z1_ref[...] + gate * acc_ref[...]) * l2g_ref[...] + l2b_ref[...]


def _moe(h, gates, l, ew, w, z1, mod, tm, tiles_per_batch, alpha):
    t, d = h.shape
    wg, wu, wd = ew
    ne, de = wg.shape[1], wg.shape[3]
    body = functools.partial(_moe_kernel, alpha=alpha)
    return pl.pallas_call(
        body,
        grid=(t // tm, ne),
        in_specs=[
            pl.BlockSpec((tm, d), lambda i, e: (i, 0)),
            pl.BlockSpec((tm, LANES), lambda i, e: (i, 0)),
            pl.BlockSpec((1, 1, d, de), lambda i, e: (l, e, 0, 0)),
            pl.BlockSpec((1, 1, d, de), lambda i, e: (l, e, 0, 0)),
            pl.BlockSpec((1, 1, de, d), lambda i, e: (l, e, 0, 0)),
            pl.BlockSpec((d, de), lambda i, e: (0, 0)),
            pl.BlockSpec((d, de), lambda i, e: (0, 0)),
            pl.BlockSpec((de, d), lambda i, e: (0, 0)),
            pl.BlockSpec((tm, d), lambda i, e: (i, 0)),
            pl.BlockSpec((1, 6, d), lambda i, e: (i // tiles_per_batch, 0, 0)),
            pl.BlockSpec((1, d), lambda i, e: (0, 0)),
            pl.BlockSpec((1, d), lambda i, e: (0, 0)),
        ],
        out_specs=pl.BlockSpec((tm, d), lambda i, e: (i, 0)),
        out_shape=jax.ShapeDtypeStruct((t, d), F32),
        scratch_shapes=[pltpu.VMEM((tm, d), F32)],
        compiler_params=_cparams(("parallel", "arbitrary")),
        name="moe",
    )(h, gates, wg, wu, wd, w["sh_wg"], w["sh_wu"], w["sh_wd"], z1, mod, w["ln2_g"], w["ln2_b"])


def _rope_swap(r):
    r4 = r.reshape(r.shape[:-1] + (2, 2, ROPE_AXIS // 2))
    return jnp.stack([-r4[..., 1, :], r4[..., 0, :]], axis=-2).reshape(r.shape)


def _prep_layer(l, w_in, q_norm_g, w_uq, kv_norm_g, w_ukv, pool_w, pool_scale, conv_dw, conv_b,
                conv_ln_g, conv_ln_b, conv_pw, w_out, ln1_g, ln1_b, router_w, router_bias,
                exp_wg, exp_wu, exp_wd, sh_wg, sh_wu, sh_wd, ln2_g, ln2_b):
    d = w_in.shape[1]
    wi = w_in[l]
    w_in_p = jnp.concatenate([
        wi[:, 0:OFF_KR], wi[:, OFF_KR:OFF_POOL], jnp.zeros((d, PC_POOL - PC_KR - QK_ROPE), F32),
        wi[:, OFF_POOL:]], axis=1).astype(BF16)

    uq = w_uq[l].reshape(Q_LORA, MLA_HEADS, QK_NOPE + QK_ROPE)
    nope, rope = uq[..., :QK_NOPE], uq[..., QK_NOPE:]
    wqat = jnp.concatenate([nope, rope, rope], axis=-1).reshape(Q_LORA, QK_WIDTH).T.astype(BF16)
    wqbt = jnp.concatenate([jnp.zeros_like(nope), jnp.zeros_like(rope), _rope_swap(rope)],
                           axis=-1).reshape(Q_LORA, QK_WIDTH).T.astype(BF16)

    ukv = w_ukv[l].reshape(KV_LORA, MLA_HEADS, QK_NOPE + V_DIM)
    kn, vv = ukv[..., :QK_NOPE], ukv[..., QK_NOPE:]
    z32 = jnp.zeros((KV_LORA, MLA_HEADS, QK_ROPE), F32)
    top = jnp.concatenate([kn, z32, z32], axis=-1)
    eye = jnp.broadcast_to(jnp.eye(QK_ROPE, dtype=F32)[:, None, :], (QK_ROPE, MLA_HEADS, QK_ROPE))
    zr = jnp.zeros((QK_ROPE, MLA_HEADS, QK_ROPE), F32)
    zn = jnp.zeros((QK_ROPE, MLA_HEADS, QK_NOPE), F32)
    padrows = jnp.zeros((KV_LORA - QK_ROPE, MLA_HEADS, HEAD_PAD), F32)

    def kmat(kr_rows):
        return jnp.concatenate([top, kr_rows, padrows], axis=0).reshape(2 * KV_LORA, QK_WIDTH).astype(BF16)

    wka_lat = kmat(jnp.concatenate([zn, zr, eye], axis=-1))
    wkb_lat = kmat(jnp.concatenate([zn, zr, _rope_swap(eye)], axis=-1))
    wka_ctx = kmat(jnp.concatenate([zn, eye, zr], axis=-1))
    wkb_ctx = jnp.zeros((2 * KV_LORA, QK_WIDTH), BF16)

    zv = jnp.zeros_like(vv)
    even = (jnp.arange(MLA_HEADS) % 2 == 0)[None, :, None]
    wv = jnp.where(even, jnp.concatenate([vv, zv], -1), jnp.concatenate([zv, vv], -1))
    wvt = wv.reshape(KV_LORA, QK_WIDTH).T.astype(BF16)
    lane = jnp.arange(QK_WIDTH) % (2 * HEAD_PAD)
    vbt = ((lane == V_DIM) | (lane == HEAD_PAD)).astype(F32)[:, None]

    pw = jnp.zeros((POOL_WIDTH, POOL_WIDTH), F32)
    for gi in range(len(POOL_WINDOWS)):
        pw = pw.at[gi * POOL_GROUP:(gi + 1) * POOL_GROUP, gi * POOL_GROUP:(gi + 1) * POOL_GROUP].set(pool_w[l, gi])

    common = dict(
        w_in=w_in_p, q_g=q_norm_g[l][None], kv_g=kv_norm_g[l][None], wqat=wqat, wqbt=wqbt, wvt=wvt, vbt=vbt,
        pool_w=pw.astype(BF16), pool_s=pool_scale[l][None],
        conv_dw=jnp.concatenate([conv_dw[l], jnp.zeros((1, CONV_WIDTH), F32)], axis=0),
        conv_b=conv_b[l][None], conv_ln_g=conv_ln_g[l][None], conv_ln_b=conv_ln_b[l][None],
        conv_pw=conv_pw[l].astype(BF16), w_out=w_out[l].astype(BF16),
        ln1_g=ln1_g[l][None], ln1_b=ln1_b[l][None],
        router_w=jnp.pad(router_w[l], ((0, 0), (0, LANES - N_EXPERTS))).astype(BF16),
        router_b=router_bias[l][:, None],
        sh_wg=sh_wg[l].astype(BF16), sh_wu=sh_wu[l].astype(BF16), sh_wd=sh_wd[l].astype(BF16),
        ln2_g=ln2_g[l][None], ln2_b=ln2_b[l][None],
    )
    return dict(common, wka=wka_lat, wkb=wkb_lat), dict(common, wka=wka_ctx, wkb=wkb_ctx)


def _rope_tables(n_lat, n_ctx):
    t = jnp.arange(n_lat)
    inv = ROPE_BASE ** (-jnp.arange(0, ROPE_AXIS, 2, dtype=F32) / ROPE_AXIS)
    ang_r = (t // GRID_W).astype(F32)[:, None] * inv
    ang_c = (t % GRID_W).astype(F32)[:, None] * inv
    cos = jnp.concatenate([jnp.cos(ang_r), jnp.cos(ang_r), jnp.cos(ang_c), jnp.cos(ang_c)], axis=1)
    sin = jnp.concatenate([jnp.sin(ang_r), jnp.sin(ang_r), jnp.sin(ang_c), jnp.sin(ang_c)], axis=1)
    ck = jnp.concatenate([jnp.ones((n_lat, HEAD_PAD - QK_ROPE), F32), cos], axis=1)
    sk = jnp.concatenate([jnp.zeros((n_lat, HEAD_PAD - QK_ROPE), F32), sin], axis=1)
    qs = ATTN_SCALE * LOG2E
    lat = ((ck * qs).T, (sk * qs).T, ck, sk)
    one = jnp.ones((n_ctx, HEAD_PAD), F32)
    zero = jnp.zeros((n_ctx, HEAD_PAD), F32)
    ctx = ((one * qs).T, zero.T, one, zero)
    return lat, ctx


def _sublayers(z, mod, l, ew, w, att, up, uc, tm, tm_moe, alpha, sorted_moe):
    b, s, d = z.shape
    z1, h, hp = _mix(att, up, uc, z, mod, w, tm, alpha)
    h2, z2 = h.reshape(b * s, d), z1.reshape(b * s, d)
    if sorted_moe:
        out = _moe_sorted(h2, hp.reshape(b * s, d // 2), l, ew, w, z2, mod, tm, s // tm, alpha)
    else:
        gates = _route(h2, w["router_w"], w["router_b"], tm)
        out = _moe(h2, gates, l, ew, w, z2, mod, tm_moe, s // tm_moe, alpha)
    return out.reshape(b, s, d)


def kernel(x, c, ctx, c_ctx, ada_w, ada_b, w_in, q_norm_g, w_uq, kv_norm_g, w_ukv, pool_w, pool_scale, conv_dw, conv_b, conv_ln_g, conv_ln_b, conv_pw, w_out, ln1_g, ln1_b, router_w, router_bias, exp_wg, exp_wu, exp_wd, sh_wg, sh_wu, sh_wd, ln2_g, ln2_b):
    b, s, d = x.shape
    n_ctx = ctx.shape[1]
    depth = ada_w.shape[0]
    alpha = (2 * depth) ** 0.25
    assert b + 1 <= 8 and s % GRID_W == 0

    tm_l = min(512, s)
    tm_c = min(256, n_ctx)
    tm_moe_l = min(1024, s)
    tm_moe_c = n_ctx
    bq = min(1024, s)

    cond = jnp.concatenate([c, c_ctx[None], jnp.zeros((8 - b - 1, d), F32)], axis=0)
    mods = _ada(cond, ada_w, ada_b)
    tab_l, tab_c = _rope_tables(s, n_ctx)
    ew = (exp_wg.astype(BF16), exp_wu.astype(BF16), exp_wd.astype(BF16))

    zl, zc = x, ctx
    for l in range(depth):
        last = l == depth - 1
        w_l, w_c = _prep_layer(l, w_in, q_norm_g, w_uq, kv_norm_g, w_ukv, pool_w, pool_scale,
                               conv_dw, conv_b, conv_ln_g, conv_ln_b, conv_pw, w_out, ln1_g, ln1_b,
                               router_w, router_bias, exp_wg, exp_wu, exp_wd, sh_wg, sh_wu, sh_wd,
                               ln2_g, ln2_b)
        mod_l = mods[l, :b].reshape(b, 6, d)
        mod_c = jnp.broadcast_to(mods[l, b].reshape(1, 6, d), (b, 6, d))

        q_l, k_l, v_l, up_l, uc_l = _proj(zl, mod_l, w_l, tab_l, tm_l)
        q_c, k_c, v_c, up_c, uc_c = _proj(zc, mod_c, w_c, tab_c, tm_c)
        att_l = _attn(q_l, k_c, v_c, k_l, v_l, bq)
        zl = _sublayers(zl, mod_l, l, ew, w_l, att_l, up_l, uc_l, tm_l, tm_moe_l, alpha, True)
        if not last:
            att_c = _attn(q_c, k_c, v_c, None, None, tm_c)
            zc = _sublayers(zc, mod_c, l, ew, w_c, att_c, up_c, uc_c, tm_c, tm_moe_c, alpha, False)
    return zl
```

```python
import functools
import math

import jax
import jax.numpy as jnp
from jax import lax
from jax.experimental import pallas as pl
from jax.experimental.pallas import tpu as pltpu

GRID_W = 64
MLA_HEADS = 8
QK_NOPE = 64
QK_ROPE = 32
V_DIM = 64
Q_LORA = 256
KV_LORA = 128
ROPE_AXIS = QK_ROPE // 2
ROPE_BASE = 10000.0
ATTN_SCALE = (QK_NOPE + QK_ROPE) ** -0.5
POOL_WINDOWS = (2, 4, 8, 16)
POOL_GROUP = 64
POOL_WIDTH = POOL_GROUP * len(POOL_WINDOWS)
CONV_WIDTH = 256
CONV_K = 31
MLA_WIDTH = MLA_HEADS * V_DIM
OFF_KV = Q_LORA
OFF_KR = OFF_KV + KV_LORA
OFF_POOL = OFF_KR + QK_ROPE
OFF_CONV = OFF_POOL + POOL_WIDTH
N_EXPERTS = 32
TOP_K = 4
N_GROUPS = 4
TOPK_GROUPS = 2
GROUP_SIZE = N_EXPERTS // N_GROUPS
ROUTED_SCALE = 2.5
EPS = 1e-6

LANES = 128
HEAD_PAD = LANES
QK_WIDTH = MLA_HEADS * HEAD_PAD
HALO = 16
P_COLS = 1280
PC_Q, PC_KV, PC_KR, PC_POOL, PC_CA, PC_CG = 0, 256, 384, 512, 768, 1024
VMEM_LIMIT = 48 * 1024 * 1024
LOG2E = math.log2(math.e)

F32 = jnp.float32
BF16 = jnp.bfloat16


def _cparams(sem, flags=None):
    return pltpu.CompilerParams(dimension_semantics=sem, vmem_limit_bytes=VMEM_LIMIT, flags=flags)


def _ln(x):
    mu = jnp.mean(x, axis=-1, keepdims=True)
    xc = x - mu
    var = jnp.mean(xc * xc, axis=-1, keepdims=True)
    return xc * lax.rsqrt(var + EPS)


def _rms(x):
    return x * lax.rsqrt(jnp.mean(x * x, axis=-1, keepdims=True) + EPS)


def _sigmoid(x):
    return 1.0 / (1.0 + jnp.exp(-x))


def _dot(a, b):
    return jnp.dot(a, b, preferred_element_type=F32)


def _dot_nt(a, b):
    return lax.dot_general(a, b, (((1,), (1,)), ((), ())), preferred_element_type=F32)


def _ada_kernel(c_ref, w_ref, b_ref, o_ref):
    x = c_ref[...]
    x = x * _sigmoid(x)
    o_ref[0] = _dot(x.astype(BF16), w_ref[0].astype(BF16)) + b_ref[0]


def _ada(cond, ada_w, ada_b):
    depth, d, n = ada_w.shape
    tn = 1536
    return pl.pallas_call(
        _ada_kernel,
        grid=(depth, n // tn),
        in_specs=[
            pl.BlockSpec((8, d), lambda l, j: (0, 0)),
            pl.BlockSpec((1, d, tn), lambda l, j: (l, 0, j)),
            pl.BlockSpec((1, 1, tn), lambda l, j: (l, 0, j)),
        ],
        out_specs=pl.BlockSpec((1, 8, tn), lambda l, j: (l, 0, j)),
        out_shape=jax.ShapeDtypeStruct((depth, 8, n), F32),
        compiler_params=_cparams(("parallel", "parallel")),
        name="ada",
    )(cond, ada_w, ada_b.reshape(depth, 1, n))


def _proj_kernel(z_ref, mod_ref, win_ref, qg_ref, kvg_ref, wqat_ref, wqbt_ref, wka_ref, wkb_ref,
                 wvt_ref, vbt_ref, cqt_ref, sqt_ref, ck_ref, sk_ref,
                 qt_ref, k_ref, vt_ref, up_ref, uc_ref):
    z = z_ref[0]
    shift = mod_ref[0, 0:1, :]
    scale = mod_ref[0, 1:2, :]
    h = _ln(z) * (1.0 + scale) + shift
    p = _dot(h.astype(BF16), win_ref[...])
    qn = (_rms(p[:, PC_Q:PC_KV]) * qg_ref[...]).astype(BF16)
    kvn = (_rms(p[:, PC_KV:PC_KR]) * kvg_ref[...]).astype(BF16)
    xk = jnp.concatenate([kvn, p[:, PC_KR:PC_POOL].astype(BF16)], axis=1)
    qat = _dot_nt(wqat_ref[...], qn)
    qbt = _dot_nt(wqbt_ref[...], qn)
    ka = _dot(xk, wka_ref[...])
    kb = _dot(xk, wkb_ref[...])
    cqt, sqt, ck, sk = cqt_ref[...], sqt_ref[...], ck_ref[...], sk_ref[...]
    for hd in range(MLA_HEADS):
        sl = slice(hd * HEAD_PAD, (hd + 1) * HEAD_PAD)
        qt_ref[0, sl, :] = (qat[sl, :] * cqt + qbt[sl, :] * sqt).astype(BF16)
        k_ref[0, :, sl] = (ka[:, sl] * ck + kb[:, sl] * sk).astype(BF16)
    vt_ref[0, 0] = (_dot_nt(wvt_ref[...], kvn) + vbt_ref[...]).astype(BF16)
    up_ref[0] = p[:, PC_POOL:PC_CA]
    uc_ref[0] = p[:, PC_CA:PC_CG] * _sigmoid(p[:, PC_CG:P_COLS])


def _proj(z, mod, w, tabs, tm):
    b, s, d = z.shape
    full = lambda shape: pl.BlockSpec(shape, lambda i, j: (0,) * len(shape))
    tok = lambda width: pl.BlockSpec((1, tm, width), lambda i, j: (i, j, 0))
    tab = pl.BlockSpec((tm, HEAD_PAD), lambda i, j: (j, 0))
    tab_t = pl.BlockSpec((HEAD_PAD, tm), lambda i, j: (0, j))
    return pl.pallas_call(
        _proj_kernel,
        grid=(b, s // tm),
        in_specs=[
            tok(d),
            pl.BlockSpec((1, 6, d), lambda i, j: (i, 0, 0)),
            full((d, P_COLS)), full((1, Q_LORA)), full((1, KV_LORA)),
            full((QK_WIDTH, Q_LORA)), full((QK_WIDTH, Q_LORA)),
            full((2 * KV_LORA, QK_WIDTH)), full((2 * KV_LORA, QK_WIDTH)),
            full((QK_WIDTH, KV_LORA)), full((QK_WIDTH, 1)),
            tab_t, tab_t, tab, tab,
        ],
        out_specs=[
            pl.BlockSpec((1, QK_WIDTH, tm), lambda i, j: (i, 0, j)),
            tok(QK_WIDTH),
            pl.BlockSpec((1, 1, QK_WIDTH, tm), lambda i, j: (i, j, 0, 0)),
            tok(POOL_WIDTH), tok(CONV_WIDTH)],
        out_shape=[
            jax.ShapeDtypeStruct((b, QK_WIDTH, s), BF16),
            jax.ShapeDtypeStruct((b, s, QK_WIDTH), BF16),
            jax.ShapeDtypeStruct((b, s // tm, QK_WIDTH, tm), BF16),
            jax.ShapeDtypeStruct((b, s, POOL_WIDTH), F32),
            jax.ShapeDtypeStruct((b, s, CONV_WIDTH), F32),
        ],
        compiler_params=_cparams(("parallel", "parallel")),
        name="proj",
    )(z, mod, w["w_in"], w["q_g"], w["kv_g"], w["wqat"], w["wqbt"], w["wka"], w["wkb"],
      w["wvt"], w["vbt"], *tabs)


HEADS_PER_STEP = 2
QCOLS = 256
ATTN_UNROLL = 4


def _attn_body(qt_ref, kc_ref, vct_ref, kl_ref, vlt_ref, o_ref, s_refs):
    heads = range(HEADS_PER_STEP)
    sls = [slice(hh * HEAD_PAD, (hh + 1) * HEAD_PAD) for hh in heads]
    qts = [qt_ref[0, sl, :] for sl in sls]
    ms, accs = [], []
    for qt, sl in zip(qts, sls):
        st = _dot(kc_ref[0, :, sl], qt)
        m = jnp.max(st, axis=0, keepdims=True)
        pt = jnp.exp2(st - m)
        ms.append(m)
        accs.append(_dot(vct_ref[0, 0, sl, :], pt.astype(BF16)))

    if kl_ref is not None:
        n, bk = vlt_ref.shape[1], vlt_ref.shape[3]
        assert n >= 2 and n % 2 == 0

        def score(c, slot):
            r0 = c * bk if isinstance(c, int) else pl.multiple_of(c * bk, bk)
            cms = []
            for hh in heads:
                st = _dot(kl_ref[0, pl.ds(r0, bk), sls[hh]], qts[hh])
                s_refs[slot][hh] = st
                cms.append(jnp.max(st, axis=0, keepdims=True))
            return cms

        def half(c, cur, nxt, ms, cms, accs, with_score=True):
            m_new = [jnp.maximum(ms[hh], cms[hh]) for hh in heads]
            alphas = [jnp.exp2(ms[hh] - m_new[hh]) for hh in heads]
            if with_score:
                cms = score(c + 1, nxt)
            out = []
            for hh in heads:
                cols = []
                for g in range(0, qts[hh].shape[1], QCOLS):
                    gs = slice(g, g + QCOLS)
                    pt = jnp.exp2(s_refs[cur][hh, :, gs] - m_new[hh][:, gs]).astype(BF16)
                    cols.append(accs[hh][:, gs] * alphas[hh][:, gs]
                                + _dot(vlt_ref[0, c, sls[hh], :], pt))
                out.append(jnp.concatenate(cols, axis=1))
            return m_new, cms, out

        cms = score(0, 0)
        unroll = ATTN_UNROLL if n % ATTN_UNROLL == 0 else 2
        trips = (n - 2) // unroll

        def steps(t, carry):
            ms, cms, accs = [list(x) for x in carry]
            for u in range(unroll):
                ms, cms, accs = half(unroll * t + u, u % 2, (u + 1) % 2, ms, cms, accs)
            return tuple(tuple(x) for x in (ms, cms, accs))

        carry = tuple(tuple(x) for x in (ms, cms, accs))
        carry = lax.fori_loop(0, trips, steps, carry)
        ms, cms, accs = [list(x) for x in carry]
        for c in range(trips * unroll, n):
            ms, cms, accs = half(c, c % 2, (c + 1) % 2, ms, cms, accs, with_score=c + 1 < n)

    acc_even, acc_odd = accs
    row = lax.broadcasted_iota(jnp.int32, acc_even.shape, 0)
    ot = jnp.where(row < V_DIM, acc_even / acc_even[V_DIM:V_DIM + 1, :], acc_odd / acc_odd[0:1, :])
    o_ref[0] = ot.T.astype(o_ref.dtype)


def _attn_kernel_full(qt_ref, kc_ref, vct_ref, kl_ref, vlt_ref, o_ref, s0_ref, s1_ref):
    _attn_body(qt_ref, kc_ref, vct_ref, kl_ref, vlt_ref, o_ref, (s0_ref, s1_ref))


def _attn_kernel_ctx(qt_ref, kc_ref, vct_ref, o_ref):
    _attn_body(qt_ref, kc_ref, vct_ref, None, None, o_ref, None)


def _attn(qt, kc, vct, kl, vlt, bq):
    b, _, s = qt.shape
    nc = kc.shape[1]
    w2 = HEADS_PER_STEP * HEAD_PAD
    in_specs = [
        pl.BlockSpec((1, w2, bq), lambda i, h, j: (i, h, j)),
        pl.BlockSpec((1, nc, w2), lambda i, h, j: (i, 0, h)),
        pl.BlockSpec((1, 1, w2, nc), lambda i, h, j: (i, 0, h, 0)),
    ]
    args = [qt, kc, vct]
    scratch = []
    if kl is None:
        body = _attn_kernel_ctx
    else:
        bk = vlt.shape[3]
        scratch = [pltpu.VMEM((HEADS_PER_STEP, bk, bq), F32)] * 2
        in_specs += [
            pl.BlockSpec((1, kl.shape[1], w2), lambda i, h, j: (i, 0, h)),
            pl.BlockSpec((1, vlt.shape[1], w2, vlt.shape[3]), lambda i, h, j: (i, 0, h, 0)),
        ]
        args += [kl, vlt]
        body = _attn_kernel_full
    return pl.pallas_call(
        body,
        grid=(b, MLA_HEADS // HEADS_PER_STEP, s // bq),
        in_specs=in_specs,
        out_specs=pl.BlockSpec((1, bq, HEADS_PER_STEP * V_DIM), lambda i, h, j: (i, j, h)),
        out_shape=jax.ShapeDtypeStruct((b, s, MLA_WIDTH), BF16),
        scratch_shapes=scratch,
        compiler_params=_cparams(("parallel", "parallel", "arbitrary")),
        name="attn",
    )(*args)


SUBLANES = 8


def _fill_ext(ext_ref, sh_ref, main_ref, left_ref, right_ref, tm):
    j = pl.program_id(1)
    last = pl.num_programs(1) - 1
    ext_ref[0:HALO, :] = jnp.where(j > 0, left_ref[0], 0.0)
    ext_ref[HALO:HALO + tm, :] = main_ref[0]
    ext_ref[HALO + tm:HALO + tm + HALO, :] = jnp.where(j < last, right_ref[0], 0.0)
    n = sh_ref.shape[1]
    for b in range(1, SUBLANES):
        sh_ref[b - 1, :, :] = ext_ref[b:b + n, :]


def _mix_kernel(att_ref, up_ref, upl_ref, upr_ref, uc_ref, ucl_ref, ucr_ref, z_ref, mod_ref,
                poolw_ref, pools_ref, dw_ref, cb_ref, clg_ref, clb_ref, cpw_ref, wout_ref,
                l1g_ref, l1b_ref, z1_ref, h_ref, hp_ref, ext_ref, sh_ref, *, tm, seq, alpha):
    j = pl.program_id(1)

    def shifted(d):
        a, b = divmod(HALO + d, SUBLANES)
        if b == 0:
            return ext_ref[a * SUBLANES:a * SUBLANES + tm, :]
        return sh_ref[b - 1, a * SUBLANES:a * SUBLANES + tm, :]

    _fill_ext(ext_ref, sh_ref, up_ref, upl_ref, upr_ref, tm)
    x = up_ref[0]
    lane = lax.broadcasted_iota(jnp.int32, (tm, POOL_WIDTH), 1)
    t = lax.broadcasted_iota(jnp.int32, (tm, POOL_WIDTH), 0) + j * tm

    run = x + shifted(-1)
    sums = [run]
    for lo in (2, 4, 8):
        for d in list(range(-lo, -lo // 2)) + list(range(lo // 2, lo)):
            run = run + shifted(d)
        sums.append(run)
    wsum = jnp.where(lane < POOL_GROUP, sums[0],
                     jnp.where(lane < 2 * POOL_GROUP, sums[1],
                               jnp.where(lane < 3 * POOL_GROUP, sums[2], sums[3])))
    lo = jnp.where(lane < POOL_GROUP, 1,
                   jnp.where(lane < 2 * POOL_GROUP, 2, jnp.where(lane < 3 * POOL_GROUP, 4, 8)))
    cnt = jnp.minimum(t + lo, seq) - jnp.maximum(t - lo, 0)
    diff = wsum / cnt.astype(F32) - x
    pooled = _dot(diff.astype(BF16), poolw_ref[...]) * pools_ref[...]

    _fill_ext(ext_ref, sh_ref, uc_ref, ucl_ref, ucr_ref, tm)
    acc = jnp.zeros((tm, CONV_WIDTH), F32) + cb_ref[...]
    for k in range(CONV_K):
        acc = acc + dw_ref[k:k + 1, :] * shifted(k - CONV_K // 2)
    y = _ln(acc) * clg_ref[...] + clb_ref[...]
    y = y * _sigmoid(y)
    conv = _dot(y.astype(BF16), cpw_ref[...])

    cat = jnp.concatenate([att_ref[0], pooled.astype(BF16), conv.astype(BF16)], axis=1)
    out = _dot(cat, wout_ref[...])
    gate = mod_ref[0, 2:3, :]
    z1 = _ln(alpha * z_ref[0] + gate * out) * l1g_ref[...] + l1b_ref[...]
    z1_ref[0] = z1
    h = _ln(z1) * (1.0 + mod_ref[0, 4:5, :]) + mod_ref[0, 3:4, :]
    h_ref[0] = h.astype(BF16)
    hp_ref[0] = _pack_rows(h)


def _mix(att, up, uc, z, mod, w, tm, alpha):
    b, s, d = z.shape
    hb = tm // HALO
    nhb = s // HALO
    full = lambda shape: pl.BlockSpec(shape, lambda i, j: (0,) * len(shape))
    tok = lambda width: pl.BlockSpec((1, tm, width), lambda i, j: (i, j, 0))
    left = lambda width: pl.BlockSpec((1, HALO, width),
                                      lambda i, j: (i, jnp.maximum(j * hb - 1, 0), 0))
    right = lambda width: pl.BlockSpec((1, HALO, width),
                                       lambda i, j: (i, jnp.minimum((j + 1) * hb, nhb - 1), 0))
    body = functools.partial(_mix_kernel, tm=tm, seq=s, alpha=alpha)
    return pl.pallas_call(
        body,
        grid=(b, s // tm),
        in_specs=[
            tok(MLA_WIDTH),
            tok(POOL_WIDTH), left(POOL_WIDTH), right(POOL_WIDTH),
            tok(CONV_WIDTH), left(CONV_WIDTH), right(CONV_WIDTH),
            tok(d),
            pl.BlockSpec((1, 6, d), lambda i, j: (i, 0, 0)),
            full((POOL_WIDTH, POOL_WIDTH)), full((1, POOL_WIDTH)),
            full((CONV_K + 1, CONV_WIDTH)), full((1, CONV_WIDTH)), full((1, CONV_WIDTH)),
            full((1, CONV_WIDTH)), full((CONV_WIDTH, CONV_WIDTH)),
            full((d, d)), full((1, d)), full((1, d)),
        ],
        out_specs=[tok(d), tok(d), tok(d // 2)],
        out_shape=[jax.ShapeDtypeStruct((b, s, d), F32), jax.ShapeDtypeStruct((b, s, d), BF16),
                   jax.ShapeDtypeStruct((b, s, d // 2), jnp.uint32)],
        scratch_shapes=[pltpu.VMEM((tm + 2 * HALO, POOL_WIDTH), F32),
                        pltpu.VMEM((SUBLANES - 1, tm + 2 * HALO - SUBLANES, POOL_WIDTH), F32)],
        compiler_params=_cparams(("parallel", "parallel")),
        name="mix",
    )(att, up, up, up, uc, uc, uc, z, mod, w["pool_w"], w["pool_s"], w["conv_dw"], w["conv_b"],
      w["conv_ln_g"], w["conv_ln_b"], w["conv_pw"], w["w_out"], w["ln1_g"], w["ln1_b"])


def _select_experts(h, rw, rb):
    tm = h.shape[0]
    logits = _dot(h, rw)
    scores = _sigmoid(logits.T[0:N_EXPERTS, :])
    sel = scores + rb
    neg = -jnp.inf

    row8 = lax.broadcasted_iota(jnp.int32, (GROUP_SIZE, tm), 0)
    gscore = []
    for g in range(N_GROUPS):
        xg = sel[g * GROUP_SIZE:(g + 1) * GROUP_SIZE, :]
        m1 = jnp.max(xg, axis=0, keepdims=True)
        i1 = jnp.min(jnp.where(xg == m1, row8, GROUP_SIZE), axis=0, keepdims=True)
        m2 = jnp.max(jnp.where(row8 == i1, neg, xg), axis=0, keepdims=True)
        gscore.append(m1 + m2)
    masked = []
    for g in range(N_GROUPS):
        ahead = jnp.zeros((1, tm), F32)
        for g2 in range(N_GROUPS):
            if g2 < g:
                ahead = ahead + jnp.where(gscore[g2] >= gscore[g], 1.0, 0.0)
            elif g2 > g:
                ahead = ahead + jnp.where(gscore[g2] > gscore[g], 1.0, 0.0)
        keep = ahead < TOPK_GROUPS
        masked.append(jnp.where(keep, sel[g * GROUP_SIZE:(g + 1) * GROUP_SIZE, :], neg))
    masked = jnp.concatenate(masked, axis=0)

    row = lax.broadcasted_iota(jnp.int32, (N_EXPERTS, tm), 0)
    chosen = jnp.zeros((N_EXPERTS, tm), F32)
    hits = []
    for _ in range(TOP_K):
        m = jnp.max(masked, axis=0, keepdims=True)
        idx = jnp.min(jnp.where(masked == m, row, N_EXPERTS), axis=0, keepdims=True)
        hit = row == idx
        hits.append(hit)
        chosen = jnp.where(hit, 1.0, chosen)
        masked = jnp.where(hit, neg, masked)
    w = chosen * scores
    gates = w / jnp.sum(w, axis=0, keepdims=True) * ROUTED_SCALE
    return gates, chosen, hits


def _lanes_to_rows(rows):
    tm = rows[0].shape[1]
    pad = jnp.zeros((LANES - len(rows), tm), F32)
    return jnp.concatenate(rows + [pad], axis=0).T


def _route_kernel(h_ref, rw_ref, rb_ref, g_ref):
    gates, _, _ = _select_experts(h_ref[...], rw_ref[...], rb_ref[...])
    tm = gates.shape[1]
    pad = jnp.zeros((LANES - N_EXPERTS, tm), F32)
    g_ref[...] = jnp.concatenate([gates, pad], axis=0).T


def _route_sorted_kernel(h_ref, rw_ref, rb_ref, g4_ref, eidx_ref, rank_ref, cnt_ref, base_ref):
    @pl.when(pl.program_id(0) == 0)
    def _():
        base_ref[...] = jnp.zeros_like(base_ref)

    gates, chosen, hits = _select_experts(h_ref[...], rw_ref[...], rb_ref[...])
    tm = gates.shape[1]
    ii = lax.broadcasted_iota(jnp.int32, (tm, tm), 0)
    jj = lax.broadcasted_iota(jnp.int32, (tm, tm), 1)
    upper = jnp.where(ii <= jj, 1.0, 0.0).astype(BF16)
    prefix = _dot(chosen.astype(BF16), upper)
    rank_full = base_ref[...] + prefix - 1.0
    row = lax.broadcasted_iota(jnp.int32, (N_EXPERTS, tm), 0).astype(F32)
    pick = lambda hit, val: jnp.sum(jnp.where(hit, val, 0.0), axis=0, keepdims=True)
    g4_ref[...] = _lanes_to_rows([pick(hit, gates) for hit in hits])
    eidx_ref[...] = jnp.concatenate([pick(hit, row) for hit in hits], axis=0).astype(jnp.int32)
    rank_ref[...] = jnp.concatenate([pick(hit, rank_full) for hit in hits], axis=0).astype(jnp.int32)
    base_ref[...] += jnp.sum(chosen, axis=1, keepdims=True)
    cnt_ref[...] = jnp.broadcast_to(base_ref[...], cnt_ref.shape)


def _route(h, rw, rb, tm):
    t, d = h.shape
    return pl.pallas_call(
        _route_kernel,
        grid=(t // tm,),
        in_specs=[
            pl.BlockSpec((tm, d), lambda i: (i, 0)),
            pl.BlockSpec((d, LANES), lambda i: (0, 0)),
            pl.BlockSpec((N_EXPERTS, 1), lambda i: (0, 0)),
        ],
        out_specs=pl.BlockSpec((tm, LANES), lambda i: (i, 0)),
        out_shape=jax.ShapeDtypeStruct((t, LANES), F32),
        compiler_params=_cparams(("parallel",)),
        name="route",
    )(h, rw, rb)


def _route_sorted(h, rw, rb, tm):
    t, d = h.shape
    pick = pl.BlockSpec((TOP_K, tm), lambda i: (0, i))
    return pl.pallas_call(
        _route_sorted_kernel,
        grid=(t // tm,),
        in_specs=[
            pl.BlockSpec((tm, d), lambda i: (i, 0)),
            pl.BlockSpec((d, LANES), lambda i: (0, 0)),
            pl.BlockSpec((N_EXPERTS, 1), lambda i: (0, 0)),
        ],
        out_specs=[pl.BlockSpec((tm, LANES), lambda i: (i, 0)), pick, pick,
                   pl.BlockSpec((N_EXPERTS, LANES), lambda i: (0, 0))],
        out_shape=[jax.ShapeDtypeStruct((t, LANES), F32),
                   jax.ShapeDtypeStruct((TOP_K, t), jnp.int32),
                   jax.ShapeDtypeStruct((TOP_K, t), jnp.int32),
                   jax.ShapeDtypeStruct((N_EXPERTS, LANES), F32)],
        scratch_shapes=[pltpu.VMEM((N_EXPERTS, 1), F32)],
        compiler_params=_cparams(("arbitrary",)),
        name="route_sorted",
    )(h, rw, rb)


def _swiglu_hidden(x, wg, wu):
    hg = _dot(x, wg)
    return hg * _sigmoid(hg) * _dot(x, wu)


ROW_TILE = 512


def _pack_rows(x):
    n = x.shape[1] // 2
    bits = lax.bitcast_convert_type(x.astype(BF16).astype(F32), jnp.uint32)
    return bits[:, :n] | (bits[:, n:] >> 16)


def _unpack_rows(w):
    hi = lax.bitcast_convert_type(w & jnp.uint32(0xFFFF0000), F32)
    lo = lax.bitcast_convert_type(w << 16, F32)
    return jnp.concatenate([hi, lo], axis=1)


def _row_copies(pos_ref, tm, make_copy):
    def body(t, carry):
        for k in range(TOP_K):
            make_copy(k, t, pos_ref[0, 0, k * tm + t]).start()
        return carry
    lax.fori_loop(0, tm, body, 0, unroll=8)


def _dispatch_kernel(pos_ref, hp_ref, xs_in_ref, xs_ref, sem):
    del xs_in_ref
    tm = hp_ref.shape[0]
    _row_copies(pos_ref, tm, lambda k, t, p: pltpu.make_async_copy(
        hp_ref.at[pl.ds(t, 1), :], xs_ref.at[pl.ds(p, 1), :], sem))
    n = TOP_K * tm
    pltpu.make_async_copy(xs_ref.at[pl.ds(0, n), :], xs_ref.at[pl.ds(0, n), :], sem).wait()


def _dispatch(pos_tiles, hp, n_rows, tm):
    t, half = hp.shape
    xs0 = jnp.zeros((n_rows, half), jnp.uint32)
    return pl.pallas_call(
        _dispatch_kernel,
        grid=(t // tm,),
        in_specs=[
            pl.BlockSpec((1, 1, TOP_K * tm), lambda i: (i, 0, 0), memory_space=pltpu.SMEM),
            pl.BlockSpec((tm, half), lambda i: (i, 0)),
            pl.BlockSpec(memory_space=pl.ANY),
        ],
        out_specs=pl.BlockSpec(memory_space=pl.ANY),
        out_shape=jax.ShapeDtypeStruct((n_rows, half), jnp.uint32),
        scratch_shapes=[pltpu.SemaphoreType.DMA(())],
        input_output_aliases={2: 0},
        compiler_params=_cparams(("arbitrary",)),
        name="dispatch",
    )(pos_tiles, hp, xs0)


def _experts_kernel(te_ref, nu_ref, xs_ref, wg_ref, wu_ref, wd_ref, ys_ref):
    del te_ref
    i = pl.program_id(0)

    @pl.when(i < nu_ref[0])
    def _():
        x = _unpack_rows(xs_ref[...]).astype(BF16)
        a = _swiglu_hidden(x, wg_ref[0, 0], wu_ref[0, 0])
        ys_ref[...] = _pack_rows(_dot(a.astype(BF16), wd_ref[0, 0]))

    @pl.when(i >= nu_ref[0])
    def _():
        ys_ref[...] = jnp.zeros_like(ys_ref)


def _experts(tile_expert, n_used, xs, l, ew):
    wg, wu, wd = ew
    d, de = wg.shape[2], wg.shape[3]
    n_rows, half = xs.shape
    rows = pl.BlockSpec((ROW_TILE, half), lambda i, te, nu: (i, 0))
    return pl.pallas_call(
        _experts_kernel,
        grid_spec=pltpu.PrefetchScalarGridSpec(
            num_scalar_prefetch=2,
            grid=(n_rows // ROW_TILE,),
            in_specs=[
                rows,
                pl.BlockSpec((1, 1, d, de), lambda i, te, nu: (l, te[i], 0, 0)),
                pl.BlockSpec((1, 1, d, de), lambda i, te, nu: (l, te[i], 0, 0)),
                pl.BlockSpec((1, 1, de, d), lambda i, te, nu: (l, te[i], 0, 0)),
            ],
            out_specs=rows,
        ),
        out_shape=jax.ShapeDtypeStruct((n_rows, half), jnp.uint32),
        compiler_params=_cparams(("arbitrary",)),
        name="experts",
    )(tile_expert, n_used, xs, wg, wu, wd)


def _combine_kernel(pos0_ref, pos_ref, g4_ref, h_ref, swg_ref, swu_ref, swd_ref, z1_ref, mod_ref,
                    l2g_ref, l2b_ref, ys_ref, o_ref, buf0_ref, buf1_ref, sems, *, alpha, tm):
    j = pl.program_id(0)
    bufs = (buf0_ref, buf1_ref)
    n = TOP_K * tm

    def gather(idx_ref, off, slot):
        for k in range(TOP_K):
            for t in range(tm):
                pltpu.make_async_copy(ys_ref.at[pl.ds(idx_ref[0, 0, off + k * tm + t], 1), :],
                                      bufs[slot].at[k, pl.ds(t, 1), :], sems.at[slot]).start()

    def wait_tile(slot):
        for k in range(TOP_K):
            pltpu.make_async_copy(ys_ref.at[pl.ds(0, tm), :], bufs[slot].at[k], sems.at[slot]).wait()

    @pl.when(j == 0)
    def _():
        gather(pos0_ref, 0, 0)

    gate = mod_ref[0, 5:6, :]
    for slot in range(2):
        rows = slice(slot * tm, (slot + 1) * tm)
        wait_tile(slot)
        gather(pos_ref, slot * n, 1 - slot)
        a = _swiglu_hidden(h_ref[rows, :], swg_ref[...], swu_ref[...])
        acc = _dot(a.astype(BF16), swd_ref[...])
        for k in range(TOP_K):
            acc = acc + g4_ref[rows, k:k + 1] * _unpack_rows(bufs[slot][k])
        o_ref[rows, :] = _ln(alpha * z1_ref[rows, :] + gate * acc) * l2g_ref[...] + l2b_ref[...]

    @pl.when(j == pl.num_programs(0) - 1)
    def _():
        wait_tile(0)


def _combine(pos_tiles, g4, h, ys, w, z1, mod, tm, tiles_per_batch, alpha):
    t, d = h.shape
    de = w["sh_wg"].shape[1]
    half = ys.shape[1]
    nt = t // tm
    assert nt % 2 == 0 and tiles_per_batch % 2 == 0
    n = TOP_K * tm
    nxt = jnp.minimum(jnp.arange(nt) + 1, nt - 1)
    pos_next = pos_tiles[nxt].reshape(nt // 2, 1, 2 * n)
    full = lambda shape: pl.BlockSpec(shape, lambda i: (0,) * len(shape))
    two = lambda width: pl.BlockSpec((2 * tm, width), lambda i: (i, 0))
    body = functools.partial(_combine_kernel, alpha=alpha, tm=tm)
    buf = pltpu.VMEM((TOP_K, tm, half), jnp.uint32)
    return pl.pallas_call(
        body,
        grid=(nt // 2,),
        in_specs=[
            pl.BlockSpec((1, 1, n), lambda i: (0, 0, 0), memory_space=pltpu.SMEM),
            pl.BlockSpec((1, 1, 2 * n), lambda i: (i, 0, 0), memory_space=pltpu.SMEM),
            two(LANES), two(d),
            full((d, de)), full((d, de)), full((de, d)),
            two(d),
            pl.BlockSpec((1, 6, d), lambda i: (2 * i // tiles_per_batch, 0, 0)),
            full((1, d)), full((1, d)),
            pl.BlockSpec(memory_space=pl.ANY),
        ],
        out_specs=two(d),
        out_shape=jax.ShapeDtypeStruct((t, d), F32),
        scratch_shapes=[buf, buf, pltpu.SemaphoreType.DMA((2,))],
        compiler_params=_cparams(("arbitrary",)),
        name="combine",
    )(pos_tiles, pos_next, g4, h, w["sh_wg"], w["sh_wu"], w["sh_wd"], z1, mod, w["ln2_g"], w["ln2_b"], ys)


def _moe_sorted(h, hp, l, ew, w, z1, mod, tm, tiles_per_batch, alpha):
    t = h.shape[0]
    g4, eidx, rank, cnt = _route_sorted(h, w["router_w"], w["router_b"], tm)
    counts = cnt[:, 0].astype(jnp.int32)
    sizes = (counts + ROW_TILE - 1) // ROW_TILE * ROW_TILE
    ends = jnp.cumsum(sizes)
    starts = ends - sizes
    n_tiles = t * TOP_K // ROW_TILE + N_EXPERTS
    n_used = (ends[-1] // ROW_TILE).reshape(1)
    tile_expert = jnp.minimum(
        jnp.sum(jnp.arange(n_tiles)[:, None] >= (ends // ROW_TILE)[None, :], axis=1), N_EXPERTS - 1
    ).astype(jnp.int32)
    onehot = eidx[:, :, None] == jnp.arange(N_EXPERTS)[None, None, :]
    pos = rank + jnp.sum(jnp.where(onehot, starts[None, None, :], 0), axis=-1)
    pos_tiles = pos.reshape(TOP_K, t // tm, tm).transpose(1, 0, 2).reshape(t // tm, 1, TOP_K * tm)
    xs = _dispatch(pos_tiles, hp, n_tiles * ROW_TILE, tm)
    ys = _experts(tile_expert, n_used, xs, l, ew)
    return _combine(pos_tiles, g4, h, ys, w, z1, mod, tm, tiles_per_batch, alpha)


def _moe_kernel(h_ref, g_ref, wg_ref, wu_ref, wd_ref, swg_ref, swu_ref, swd_ref, z1_ref, mod_ref,
                l2g_ref, l2b_ref, o_ref, acc_ref, *, alpha):
    e = pl.program_id(1)
    x = h_ref[...]

    @pl.when(e == 0)
    def _():
        a = _swiglu_hidden(x, swg_ref[...], swu_ref[...])
        acc_ref[...] = _dot(a.astype(BF16), swd_ref[...])

    lane = lax.broadcasted_iota(jnp.int32, g_ref.shape, 1)
    gcol = jnp.sum(jnp.where(lane == e, g_ref[...], 0.0), axis=1, keepdims=True)
    a = _swiglu_hidden(x, wg_ref[0, 0], wu_ref[0, 0]) * gcol
    acc_ref[...] += _dot(a.astype(BF16), wd_ref[0, 0])

    @pl.when(e == pl.num_programs(1) - 1)
    def _():
        gate = mod_ref[0, 5:6, :]
        o_ref[...] = _ln(alpha * z1_ref[...] + gate * acc_ref[...]) * l2g_ref[...] + l2b_ref[...]


def _moe(h, gates, l, ew, w, z1, mod, tm, tiles_per_batch, alpha):
    t, d = h.shape
    wg, wu, wd = ew
    ne, de = wg.shape[1], wg.shape[3]
    body = functools.partial(_moe_kernel, alpha=alpha)
    return pl.pallas_call(
        body,
        grid=(t // tm, ne),
        in_specs=[
            pl.BlockSpec((tm, d), lambda i, e: (i, 0)),
            pl.BlockSpec((tm, LANES), lambda i, e: (i, 0)),
            pl.BlockSpec((1, 1, d, de), lambda i, e: (l, e, 0, 0)),
            pl.BlockSpec((1, 1, d, de), lambda i, e: (l, e, 0, 0)),
            pl.BlockSpec((1, 1, de, d), lambda i, e: (l, e, 0, 0)),
            pl.BlockSpec((d, de), lambda i, e: (0, 0)),
            pl.BlockSpec((d, de), lambda i, e: (0, 0)),
            pl.BlockSpec((de, d), lambda i, e: (0, 0)),
            pl.BlockSpec((tm, d), lambda i, e: (i, 0)),
            pl.BlockSpec((1, 6, d), lambda i, e: (i // tiles_per_batch, 0, 0)),
            pl.BlockSpec((1, d), lambda i, e: (0, 0)),
            pl.BlockSpec((1, d), lambda i, e: (0, 0)),
        ],
        out_specs=pl.BlockSpec((tm, d), lambda i, e: (i, 0)),
        out_shape=jax.ShapeDtypeStruct((t, d), F32),
        scratch_shapes=[pltpu.VMEM((tm, d), F32)],
        compiler_params=_cparams(("parallel", "arbitrary")),
        name="moe",
    )(h, gates, wg, wu, wd, w["sh_wg"], w["sh_wu"], w["sh_wd"], z1, mod, w["ln2_g"], w["ln2_b"])


def _rope_swap(r):
    r4 = r.reshape(r.shape[:-1] + (2, 2, ROPE_AXIS // 2))
    return jnp.stack([-r4[..., 1, :], r4[..., 0, :]], axis=-2).reshape(r.shape)


def _prep_layer(l, w_in, q_norm_g, w_uq, kv_norm_g, w_ukv, pool_w, pool_scale, conv_dw, conv_b,
                conv_ln_g, conv_ln_b, conv_pw, w_out, ln1_g, ln1_b, router_w, router_bias,
                exp_wg, exp_wu, exp_wd, sh_wg, sh_wu, sh_wd, ln2_g, ln2_b):
    d = w_in.shape[1]
    wi = w_in[l]
    w_in_p = jnp.concatenate([
        wi[:, 0:OFF_KR], wi[:, OFF_KR:OFF_POOL], jnp.zeros((d, PC_POOL - PC_KR - QK_ROPE), F32),
        wi[:, OFF_POOL:]], axis=1).astype(BF16)

    uq = w_uq[l].reshape(Q_LORA, MLA_HEADS, QK_NOPE + QK_ROPE)
    nope, rope = uq[..., :QK_NOPE], uq[..., QK_NOPE:]
    wqat = jnp.concatenate([nope, rope, rope], axis=-1).reshape(Q_LORA, QK_WIDTH).T.astype(BF16)
    wqbt = jnp.concatenate([jnp.zeros_like(nope), jnp.zeros_like(rope), _rope_swap(rope)],
                           axis=-1).reshape(Q_LORA, QK_WIDTH).T.astype(BF16)

    ukv = w_ukv[l].reshape(KV_LORA, MLA_HEADS, QK_NOPE + V_DIM)
    kn, vv = ukv[..., :QK_NOPE], ukv[..., QK_NOPE:]
    z32 = jnp.zeros((KV_LORA, MLA_HEADS, QK_ROPE), F32)
    top = jnp.concatenate([kn, z32, z32], axis=-1)
    eye = jnp.broadcast_to(jnp.eye(QK_ROPE, dtype=F32)[:, None, :], (QK_ROPE, MLA_HEADS, QK_ROPE))
    zr = jnp.zeros((QK_ROPE, MLA_HEADS, QK_ROPE), F32)
    zn = jnp.zeros((QK_ROPE, MLA_HEADS, QK_NOPE), F32)
    padrows = jnp.zeros((KV_LORA - QK_ROPE, MLA_HEADS, HEAD_PAD), F32)

    def kmat(kr_rows):
        return jnp.concatenate([top, kr_rows, padrows], axis=0).reshape(2 * KV_LORA, QK_WIDTH).astype(BF16)

    wka_lat = kmat(jnp.concatenate([zn, zr, eye], axis=-1))
    wkb_lat = kmat(jnp.concatenate([zn, zr, _rope_swap(eye)], axis=-1))
    wka_ctx = kmat(jnp.concatenate([zn, eye, zr], axis=-1))
    wkb_ctx = jnp.zeros((2 * KV_LORA, QK_WIDTH), BF16)

    zv = jnp.zeros_like(vv)
    even = (jnp.arange(MLA_HEADS) % 2 == 0)[None, :, None]
    wv = jnp.where(even, jnp.concatenate([vv, zv], -1), jnp.concatenate([zv, vv], -1))
    wvt = wv.reshape(KV_LORA, QK_WIDTH).T.astype(BF16)
    lane = jnp.arange(QK_WIDTH) % (2 * HEAD_PAD)
    vbt = ((lane == V_DIM) | (lane == HEAD_PAD)).astype(F32)[:, None]

    pw = jnp.zeros((POOL_WIDTH, POOL_WIDTH), F32)
    for gi in range(len(POOL_WINDOWS)):
        pw = pw.at[gi * POOL_GROUP:(gi + 1) * POOL_GROUP, gi * POOL_GROUP:(gi + 1) * POOL_GROUP].set(pool_w[l, gi])

    common = dict(
        w_in=w_in_p, q_g=q_norm_g[l][None], kv_g=kv_norm_g[l][None], wqat=wqat, wqbt=wqbt, wvt=wvt, vbt=vbt,
        pool_w=pw.astype(BF16), pool_s=pool_scale[l][None],
        conv_dw=jnp.concatenate([conv_dw[l], jnp.zeros((1, CONV_WIDTH), F32)], axis=0),
        conv_b=conv_b[l][None], conv_ln_g=conv_ln_g[l][None], conv_ln_b=conv_ln_b[l][None],
        conv_pw=conv_pw[l].astype(BF16), w_out=w_out[l].astype(BF16),
        ln1_g=ln1_g[l][None], ln1_b=ln1_b[l][None],
        router_w=jnp.pad(router_w[l], ((0, 0), (0, LANES - N_EXPERTS))).astype(BF16),
        router_b=router_bias[l][:, None],
        sh_wg=sh_wg[l].astype(BF16), sh_wu=sh_wu[l].astype(BF16), sh_wd=sh_wd[l].astype(BF16),
        ln2_g=ln2_g[l][None], ln2_b=ln2_b[l][None],
    )
    return dict(common, wka=wka_lat, wkb=wkb_lat), dict(common, wka=wka_ctx, wkb=wkb_ctx)


def _rope_tables(n_lat, n_ctx):
    t = jnp.arange(n_lat)
    inv = ROPE_BASE ** (-jnp.arange(0, ROPE_AXIS, 2, dtype=F32) / ROPE_AXIS)
    ang_r = (t // GRID_W).astype(F32)[:, None] * inv
    ang_c = (t % GRID_W).astype(F32)[:, None] * inv
    cos = jnp.concatenate([jnp.cos(ang_r), jnp.cos(ang_r), jnp.cos(ang_c), jnp.cos(ang_c)], axis=1)
    sin = jnp.concatenate([jnp.sin(ang_r), jnp.sin(ang_r), jnp.sin(ang_c), jnp.sin(ang_c)], axis=1)
    ck = jnp.concatenate([jnp.ones((n_lat, HEAD_PAD - QK_ROPE), F32), cos], axis=1)
    sk = jnp.concatenate([jnp.zeros((n_lat, HEAD_PAD - QK_ROPE), F32), sin], axis=1)
    qs = ATTN_SCALE * LOG2E
    lat = ((ck * qs).T, (sk * qs).T, ck, sk)
    one = jnp.ones((n_ctx, HEAD_PAD), F32)
    zero = jnp.zeros((n_ctx, HEAD_PAD), F32)
    ctx = ((one * qs).T, zero.T, one, zero)
    return lat, ctx


def _sublayers(z, mod, l, ew, w, att, up, uc, tm, tm_moe, alpha, sorted_moe):
    b, s, d = z.shape
    z1, h, hp = _mix(att, up, uc, z, mod, w, tm, alpha)
    h2, z2 = h.reshape(b * s, d), z1.reshape(b * s, d)
    if sorted_moe:
        out = _moe_sorted(h2, hp.reshape(b * s, d // 2), l, ew, w, z2, mod, tm, s // tm, alpha)
    else:
        gates = _route(h2, w["router_w"], w["router_b"], tm)
        out = _moe(h2, gates, l, ew, w, z2, mod, tm_moe, s // tm_moe, alpha)
    return out.reshape(b, s, d)


def kernel(x, c, ctx, c_ctx, ada_w, ada_b, w_in, q_norm_g, w_uq, kv_norm_g, w_ukv, pool_w, pool_scale, conv_dw, conv_b, conv_ln_g, conv_ln_b, conv_pw, w_out, ln1_g, ln1_b, router_w, router_bias, exp_wg, exp_wu, exp_wd, sh_wg, sh_wu, sh_wd, ln2_g, ln2_b):
    b, s, d = x.shape
    n_ctx = ctx.shape[1]
    depth = ada_w.shape[0]
    alpha = (2 * depth) ** 0.25
    assert b + 1 <= 8 and s % GRID_W == 0

    tm_l = min(512, s)
    tm_c = min(256, n_ctx)
    tm_moe_l = min(1024, s)
    tm_moe_c = n_ctx
    bq = min(1024, s)

    cond = jnp.concatenate([c, c_ctx[None], jnp.zeros((8 - b - 1, d), F32)], axis=0)
    mods = _ada(cond, ada_w, ada_b)
    tab_l, tab_c = _rope_tables(s, n_ctx)
    ew = (exp_wg.astype(BF16), exp_wu.astype(BF16), exp_wd.astype(BF16))

    zl, zc = x, ctx
    for l in range(depth):
        last = l == depth - 1
        w_l, w_c = _prep_layer(l, w_in, q_norm_g, w_uq, kv_norm_g, w_ukv, pool_w, pool_scale,
                               conv_dw, conv_b, conv_ln_g, conv_ln_b, conv_pw, w_out, ln1_g, ln1_b,
                               router_w, router_bias, exp_wg, exp_wu, exp_wd, sh_wg, sh_wu, sh_wd,
                               ln2_g, ln2_b)
        mod_l = mods[l, :b].reshape(b, 6, d)
        mod_c = jnp.broadcast_to(mods[l, b].reshape(1, 6, d), (b, 6, d))

        q_l, k_l, v_l, up_l, uc_l = _proj(zl, mod_l, w_l, tab_l, tm_l)
        q_c, k_c, v_c, up_c, uc_c = _proj(zc, mod_c, w_c, tab_c, tm_c)
        att_l = _attn(q_l, k_c, v_c, k_l, v_l, bq)
        zl = _sublayers(zl, mod_l, l, ew, w_l, att_l, up_l, uc_l, tm_l, tm_moe_l, alpha, True)
        if not last:
            att_c = _attn(q_c, k_c, v_c, None, None, tm_c)
            zc = _sublayers(zc, mod_c, l, ew, w_c, att_c, up_c, uc_c, tm_c, tm_moe_c, alpha, False)
    return zl
```

```python
import functools
import math

import jax
import jax.numpy as jnp
from jax import lax
from jax.experimental import pallas as pl
from jax.experimental.pallas import tpu as pltpu

GRID_W = 64
MLA_HEADS = 8
QK_NOPE = 64
QK_ROPE = 32
V_DIM = 64
Q_LORA = 256
KV_LORA = 128
ROPE_AXIS = QK_ROPE // 2
ROPE_BASE = 10000.0
ATTN_SCALE = (QK_NOPE + QK_ROPE) ** -0.5
POOL_WINDOWS = (2, 4, 8, 16)
POOL_GROUP = 64
POOL_WIDTH = POOL_GROUP * len(POOL_WINDOWS)
CONV_WIDTH = 256
CONV_K = 31
MLA_WIDTH = MLA_HEADS * V_DIM
OFF_KV = Q_LORA
OFF_KR = OFF_KV + KV_LORA
OFF_POOL = OFF_KR + QK_ROPE
OFF_CONV = OFF_POOL + POOL_WIDTH
N_EXPERTS = 32
TOP_K = 4
N_GROUPS = 4
TOPK_GROUPS = 2
GROUP_SIZE = N_EXPERTS // N_GROUPS
ROUTED_SCALE = 2.5
EPS = 1e-6

LANES = 128
HEAD_PAD = LANES
QK_WIDTH = MLA_HEADS * HEAD_PAD
V_ROWS = 80
VT_WIDTH = MLA_HEADS * V_ROWS
HALO = 16
P_COLS = 1280
PC_Q, PC_KV, PC_KR, PC_POOL, PC_CA, PC_CG = 0, 256, 384, 512, 768, 1024
VMEM_LIMIT = 48 * 1024 * 1024
LOG2E = math.log2(math.e)

F32 = jnp.float32
BF16 = jnp.bfloat16


def _cparams(sem, flags=None):
    return pltpu.CompilerParams(dimension_semantics=sem, vmem_limit_bytes=VMEM_LIMIT, flags=flags)


def _ln(x):
    mu = jnp.mean(x, axis=-1, keepdims=True)
    xc = x - mu
    var = jnp.mean(xc * xc, axis=-1, keepdims=True)
    return xc * lax.rsqrt(var + EPS)


def _rms(x):
    return x * lax.rsqrt(jnp.mean(x * x, axis=-1, keepdims=True) + EPS)


def _sigmoid(x):
    return 1.0 / (1.0 + jnp.exp(-x))


def _dot(a, b):
    return jnp.dot(a, b, preferred_element_type=F32)


def _dot_nt(a, b):
    return lax.dot_general(a, b, (((1,), (1,)), ((), ())), preferred_element_type=F32)


def _ada_kernel(c_ref, w_ref, b_ref, o_ref):
    x = c_ref[...]
    x = x * _sigmoid(x)
    o_ref[0] = _dot(x.astype(BF16), w_ref[0].astype(BF16)) + b_ref[0]


def _ada(cond, ada_w, ada_b):
    depth, d, n = ada_w.shape
    tn = 1536
    return pl.pallas_call(
        _ada_kernel,
        grid=(depth, n // tn),
        in_specs=[
            pl.BlockSpec((8, d), lambda l, j: (0, 0)),
            pl.BlockSpec((1, d, tn), lambda l, j: (l, 0, j)),
            pl.BlockSpec((1, 1, tn), lambda l, j: (l, 0, j)),
        ],
        out_specs=pl.BlockSpec((1, 8, tn), lambda l, j: (l, 0, j)),
        out_shape=jax.ShapeDtypeStruct((depth, 8, n), F32),
        compiler_params=_cparams(("parallel", "parallel")),
        name="ada",
    )(cond, ada_w, ada_b.reshape(depth, 1, n))


def _proj_kernel(z_ref, mod_ref, win_ref, qg_ref, kvg_ref, wqat_ref, wqbt_ref, wka_ref, wkb_ref,
                 wvt_ref, vbt_ref, cqt_ref, sqt_ref, ck_ref, sk_ref,
                 qt_ref, k_ref, vt_ref, up_ref, uc_ref):
    z = z_ref[0]
    shift = mod_ref[0, 0:1, :]
    scale = mod_ref[0, 1:2, :]
    h = _ln(z) * (1.0 + scale) + shift
    p = _dot(h.astype(BF16), win_ref[...])
    qn = (_rms(p[:, PC_Q:PC_KV]) * qg_ref[...]).astype(BF16)
    kvn = (_rms(p[:, PC_KV:PC_KR]) * kvg_ref[...]).astype(BF16)
    xk = jnp.concatenate([kvn, p[:, PC_KR:PC_POOL].astype(BF16)], axis=1)
    qat = _dot_nt(wqat_ref[...], qn)
    qbt = _dot_nt(wqbt_ref[...], qn)
    ka = _dot(xk, wka_ref[...])
    kb = _dot(xk, wkb_ref[...])
    cqt, sqt, ck, sk = cqt_ref[...], sqt_ref[...], ck_ref[...], sk_ref[...]
    for hd in range(MLA_HEADS):
        sl = slice(hd * HEAD_PAD, (hd + 1) * HEAD_PAD)
        qt_ref[0, sl, :] = (qat[sl, :] * cqt + qbt[sl, :] * sqt).astype(BF16)
        k_ref[0, :, sl] = (ka[:, sl] * ck + kb[:, sl] * sk).astype(BF16)
    vt_ref[0, 0] = (_dot_nt(wvt_ref[...], kvn) + vbt_ref[...]).astype(BF16)
    up_ref[0] = p[:, PC_POOL:PC_CA]
    uc_ref[0] = p[:, PC_CA:PC_CG] * _sigmoid(p[:, PC_CG:P_COLS])


def _proj(z, mod, w, tabs, tm):
    b, s, d = z.shape
    full = lambda shape: pl.BlockSpec(shape, lambda i, j: (0,) * len(shape))
    tok = lambda width: pl.BlockSpec((1, tm, width), lambda i, j: (i, j, 0))
    tab = pl.BlockSpec((tm, HEAD_PAD), lambda i, j: (j, 0))
    tab_t = pl.BlockSpec((HEAD_PAD, tm), lambda i, j: (0, j))
    return pl.pallas_call(
        _proj_kernel,
        grid=(b, s // tm),
        in_specs=[
            tok(d),
            pl.BlockSpec((1, 6, d), lambda i, j: (i, 0, 0)),
            full((d, P_COLS)), full((1, Q_LORA)), full((1, KV_LORA)),
            full((QK_WIDTH, Q_LORA)), full((QK_WIDTH, Q_LORA)),
            full((2 * KV_LORA, QK_WIDTH)), full((2 * KV_LORA, QK_WIDTH)),
            full((VT_WIDTH, KV_LORA)), full((VT_WIDTH, 1)),
            tab_t, tab_t, tab, tab,
        ],
        out_specs=[
            pl.BlockSpec((1, QK_WIDTH, tm), lambda i, j: (i, 0, j)),
            tok(QK_WIDTH),
            pl.BlockSpec((1, 1, VT_WIDTH, tm), lambda i, j: (i, j, 0, 0)),
            tok(POOL_WIDTH), tok(CONV_WIDTH)],
        out_shape=[
            jax.ShapeDtypeStruct((b, QK_WIDTH, s), BF16),
            jax.ShapeDtypeStruct((b, s, QK_WIDTH), BF16),
            jax.ShapeDtypeStruct((b, s // tm, VT_WIDTH, tm), BF16),
            jax.ShapeDtypeStruct((b, s, POOL_WIDTH), F32),
            jax.ShapeDtypeStruct((b, s, CONV_WIDTH), F32),
        ],
        compiler_params=_cparams(("parallel", "parallel")),
        name="proj",
    )(z, mod, w["w_in"], w["q_g"], w["kv_g"], w["wqat"], w["wqbt"], w["wka"], w["wkb"],
      w["wvt"], w["vbt"], *tabs)


HEADS_PER_STEP = 2
QCOLS = 256
ATTN_UNROLL = 4


def _attn_body(qt_ref, kc_ref, vct_ref, kl_ref, vlt_ref, o_ref, s_refs):
    heads = range(HEADS_PER_STEP)
    sls = [slice(hh * HEAD_PAD, (hh + 1) * HEAD_PAD) for hh in heads]
    vsl = [slice(hh * V_ROWS, (hh + 1) * V_ROWS) for hh in heads]
    qts = [qt_ref[0, sl, :] for sl in sls]
    ms, accs = [], []
    for hh in heads:
        st = _dot(kc_ref[0, :, sls[hh]], qts[hh])
        m = jnp.max(st, axis=0, keepdims=True)
        pt = jnp.exp2(st - m)
        ms.append(m)
        accs.append(_dot(vct_ref[0, 0, vsl[hh], :], pt.astype(BF16)))

    if kl_ref is not None:
        n, bk = vlt_ref.shape[1], vlt_ref.shape[3]
        assert n >= 2 and n % 2 == 0

        def score(c, slot):
            r0 = c * bk if isinstance(c, int) else pl.multiple_of(c * bk, bk)
            cms = []
            for hh in heads:
                st = _dot(kl_ref[0, pl.ds(r0, bk), sls[hh]], qts[hh])
                s_refs[slot][hh] = st
                cms.append(jnp.max(st, axis=0, keepdims=True))
            return cms

        def half(c, cur, nxt, ms, cms, accs, with_score=True):
            m_new = [jnp.maximum(ms[hh], cms[hh]) for hh in heads]
            alphas = [jnp.exp2(ms[hh] - m_new[hh]) for hh in heads]
            if with_score:
                cms = score(c + 1, nxt)
            out = []
            for hh in heads:
                cols = []
                for g in range(0, qts[hh].shape[1], QCOLS):
                    gs = slice(g, g + QCOLS)
                    pt = jnp.exp2(s_refs[cur][hh, :, gs] - m_new[hh][:, gs]).astype(BF16)
                    cols.append(accs[hh][:, gs] * alphas[hh][:, gs]
                                + _dot(vlt_ref[0, c, vsl[hh], :], pt))
                out.append(jnp.concatenate(cols, axis=1))
            return m_new, cms, out

        cms = score(0, 0)
        unroll = ATTN_UNROLL if n % ATTN_UNROLL == 0 else 2
        trips = (n - 2) // unroll

        def steps(t, carry):
            ms, cms, accs = [list(x) for x in carry]
            for u in range(unroll):
                ms, cms, accs = half(unroll * t + u, u % 2, (u + 1) % 2, ms, cms, accs)
            return tuple(tuple(x) for x in (ms, cms, accs))

        carry = tuple(tuple(x) for x in (ms, cms, accs))
        carry = lax.fori_loop(0, trips, steps, carry)
        ms, cms, accs = [list(x) for x in carry]
        for c in range(trips * unroll, n):
            ms, cms, accs = half(c, c % 2, (c + 1) % 2, ms, cms, accs, with_score=c + 1 < n)

    ot = jnp.concatenate([acc[0:V_DIM, :] / acc[V_DIM:V_DIM + 1, :] for acc in accs], axis=0)
    o_ref[0] = ot.T.astype(o_ref.dtype)


def _attn_kernel_full(qt_ref, kc_ref, vct_ref, kl_ref, vlt_ref, o_ref, s0_ref, s1_ref):
    _attn_body(qt_ref, kc_ref, vct_ref, kl_ref, vlt_ref, o_ref, (s0_ref, s1_ref))


def _attn_kernel_ctx(qt_ref, kc_ref, vct_ref, o_ref):
    _attn_body(qt_ref, kc_ref, vct_ref, None, None, o_ref, None)


def _attn(qt, kc, vct, kl, vlt, bq):
    b, _, s = qt.shape
    nc = kc.shape[1]
    w2 = HEADS_PER_STEP * HEAD_PAD
    v2 = HEADS_PER_STEP * V_ROWS
    in_specs = [
        pl.BlockSpec((1, w2, bq), lambda i, h, j: (i, h, j)),
        pl.BlockSpec((1, nc, w2), lambda i, h, j: (i, 0, h)),
        pl.BlockSpec((1, 1, v2, nc), lambda i, h, j: (i, 0, h, 0)),
    ]
    args = [qt, kc, vct]
    scratch = []
    if kl is None:
        body = _attn_kernel_ctx
    else:
        bk = vlt.shape[3]
        scratch = [pltpu.VMEM((HEADS_PER_STEP, bk, bq), F32)] * 2
        in_specs += [
            pl.BlockSpec((1, kl.shape[1], w2), lambda i, h, j: (i, 0, h)),
            pl.BlockSpec((1, vlt.shape[1], v2, vlt.shape[3]), lambda i, h, j: (i, 0, h, 0)),
        ]
        args += [kl, vlt]
        body = _attn_kernel_full
    return pl.pallas_call(
        body,
        grid=(b, MLA_HEADS // HEADS_PER_STEP, s // bq),
        in_specs=in_specs,
        out_specs=pl.BlockSpec((1, bq, HEADS_PER_STEP * V_DIM), lambda i, h, j: (i, j, h)),
        out_shape=jax.ShapeDtypeStruct((b, s, MLA_WIDTH), BF16),
        scratch_shapes=scratch,
        compiler_params=_cparams(("parallel", "parallel", "arbitrary")),
        name="attn",
    )(*args)


SUBLANES = 8


def _fill_ext(ext_ref, sh_ref, main_ref, left_ref, right_ref, tm):
    j = pl.program_id(1)
    last = pl.num_programs(1) - 1
    ext_ref[0:HALO, :] = jnp.where(j > 0, left_ref[0], 0.0)
    ext_ref[HALO:HALO + tm, :] = main_ref[0]
    ext_ref[HALO + tm:HALO + tm + HALO, :] = jnp.where(j < last, right_ref[0], 0.0)
    n = sh_ref.shape[1]
    for b in range(1, SUBLANES):
        sh_ref[b - 1, :, :] = ext_ref[b:b + n, :]


def _mix_kernel(att_ref, up_ref, upl_ref, upr_ref, uc_ref, ucl_ref, ucr_ref, z_ref, mod_ref,
                poolw_ref, pools_ref, dw_ref, cb_ref, clg_ref, clb_ref, cpw_ref, wout_ref,
                l1g_ref, l1b_ref, z1_ref, h_ref, hp_ref, ext_ref, sh_ref, *, tm, seq, alpha):
    j = pl.program_id(1)

    def shifted(d):
        a, b = divmod(HALO + d, SUBLANES)
        if b == 0:
            return ext_ref[a * SUBLANES:a * SUBLANES + tm, :]
        return sh_ref[b - 1, a * SUBLANES:a * SUBLANES + tm, :]

    _fill_ext(ext_ref, sh_ref, up_ref, upl_ref, upr_ref, tm)
    x = up_ref[0]
    lane = lax.broadcasted_iota(jnp.int32, (tm, POOL_WIDTH), 1)
    t = lax.broadcasted_iota(jnp.int32, (tm, POOL_WIDTH), 0) + j * tm

    run = x + shifted(-1)
    sums = [run]
    for lo in (2, 4, 8):
        for d in list(range(-lo, -lo // 2)) + list(range(lo // 2, lo)):
            run = run + shifted(d)
        sums.append(run)
    wsum = jnp.where(lane < POOL_GROUP, sums[0],
                     jnp.where(lane < 2 * POOL_GROUP, sums[1],
                               jnp.where(lane < 3 * POOL_GROUP, sums[2], sums[3])))
    lo = jnp.where(lane < POOL_GROUP, 1,
                   jnp.where(lane < 2 * POOL_GROUP, 2, jnp.where(lane < 3 * POOL_GROUP, 4, 8)))
    cnt = jnp.minimum(t + lo, seq) - jnp.maximum(t - lo, 0)
    diff = wsum / cnt.astype(F32) - x
    pooled = _dot(diff.astype(BF16), poolw_ref[...]) * pools_ref[...]

    _fill_ext(ext_ref, sh_ref, uc_ref, ucl_ref, ucr_ref, tm)
    acc = jnp.zeros((tm, CONV_WIDTH), F32) + cb_ref[...]
    for k in range(CONV_K):
        acc = acc + dw_ref[k:k + 1, :] * shifted(k - CONV_K // 2)
    y = _ln(acc) * clg_ref[...] + clb_ref[...]
    y = y * _sigmoid(y)
    conv = _dot(y.astype(BF16), cpw_ref[...])

    cat = jnp.concatenate([att_ref[0], pooled.astype(BF16), conv.astype(BF16)], axis=1)
    out = _dot(cat, wout_ref[...])
    gate = mod_ref[0, 2:3, :]
    z1 = _ln(alpha * z_ref[0] + gate * out) * l1g_ref[...] + l1b_ref[...]
    z1_ref[0] = z1
    h = _ln(z1) * (1.0 + mod_ref[0, 4:5, :]) + mod_ref[0, 3:4, :]
    h_ref[0] = h.astype(BF16)
    hp_ref[0] = _pack_rows(h)


def _mix(att, up, uc, z, mod, w, tm, alpha):
    b, s, d = z.shape
    hb = tm // HALO
    nhb = s // HALO
    full = lambda shape: pl.BlockSpec(shape, lambda i, j: (0,) * len(shape))
    tok = lambda width: pl.BlockSpec((1, tm, width), lambda i, j: (i, j, 0))
    left = lambda width: pl.BlockSpec((1, HALO, width),
                                      lambda i, j: (i, jnp.maximum(j * hb - 1, 0), 0))
    right = lambda width: pl.BlockSpec((1, HALO, width),
                                       lambda i, j: (i, jnp.minimum((j + 1) * hb, nhb - 1), 0))
    body = functools.partial(_mix_kernel, tm=tm, seq=s, alpha=alpha)
    return pl.pallas_call(
        body,
        grid=(b, s // tm),
        in_specs=[
            tok(MLA_WIDTH),
            tok(POOL_WIDTH), left(POOL_WIDTH), right(POOL_WIDTH),
            tok(CONV_WIDTH), left(CONV_WIDTH), right(CONV_WIDTH),
            tok(d),
            pl.BlockSpec((1, 6, d), lambda i, j: (i, 0, 0)),
            full((POOL_WIDTH, POOL_WIDTH)), full((1, POOL_WIDTH)),
            full((CONV_K + 1, CONV_WIDTH)), full((1, CONV_WIDTH)), full((1, CONV_WIDTH)),
            full((1, CONV_WIDTH)), full((CONV_WIDTH, CONV_WIDTH)),
            full((d, d)), full((1, d)), full((1, d)),
        ],
        out_specs=[tok(d), tok(d), tok(d // 2)],
        out_shape=[jax.ShapeDtypeStruct((b, s, d), F32), jax.ShapeDtypeStruct((b, s, d), BF16),
                   jax.ShapeDtypeStruct((b, s, d // 2), jnp.uint32)],
        scratch_shapes=[pltpu.VMEM((tm + 2 * HALO, POOL_WIDTH), F32),
                        pltpu.VMEM((SUBLANES - 1, tm + 2 * HALO - SUBLANES, POOL_WIDTH), F32)],
        compiler_params=_cparams(("parallel", "parallel")),
        name="mix",
    )(att, up, up, up, uc, uc, uc, z, mod, w["pool_w"], w["pool_s"], w["conv_dw"], w["conv_b"],
      w["conv_ln_g"], w["conv_ln_b"], w["conv_pw"], w["w_out"], w["ln1_g"], w["ln1_b"])


def _select_experts(h, rw, rb):
    tm = h.shape[0]
    logits = _dot(h, rw)
    scores = _sigmoid(logits.T[0:N_EXPERTS, :])
    sel = scores + rb
    neg = -jnp.inf

    row8 = lax.broadcasted_iota(jnp.int32, (GROUP_SIZE, tm), 0)
    gscore = []
    for g in range(N_GROUPS):
        xg = sel[g * GROUP_SIZE:(g + 1) * GROUP_SIZE, :]
        m1 = jnp.max(xg, axis=0, keepdims=True)
        i1 = jnp.min(jnp.where(xg == m1, row8, GROUP_SIZE), axis=0, keepdims=True)
        m2 = jnp.max(jnp.where(row8 == i1, neg, xg), axis=0, keepdims=True)
        gscore.append(m1 + m2)
    masked = []
    for g in range(N_GROUPS):
        ahead = jnp.zeros((1, tm), F32)
        for g2 in range(N_GROUPS):
            if g2 < g:
                ahead = ahead + jnp.where(gscore[g2] >= gscore[g], 1.0, 0.0)
            elif g2 > g:
                ahead = ahead + jnp.where(gscore[g2] > gscore[g], 1.0, 0.0)
        keep = ahead < TOPK_GROUPS
        masked.append(jnp.where(keep, sel[g * GROUP_SIZE:(g + 1) * GROUP_SIZE, :], neg))
    masked = jnp.concatenate(masked, axis=0)

    row = lax.broadcasted_iota(jnp.int32, (N_EXPERTS, tm), 0)
    chosen = jnp.zeros((N_EXPERTS, tm), F32)
    hits = []
    for _ in range(TOP_K):
        m = jnp.max(masked, axis=0, keepdims=True)
        idx = jnp.min(jnp.where(masked == m, row, N_EXPERTS), axis=0, keepdims=True)
        hit = row == idx
        hits.append(hit)
        chosen = jnp.where(hit, 1.0, chosen)
        masked = jnp.where(hit, neg, masked)
    w = chosen * scores
    gates = w / jnp.sum(w, axis=0, keepdims=True) * ROUTED_SCALE
    return gates, chosen, hits


def _lanes_to_rows(rows):
    tm = rows[0].shape[1]
    pad = jnp.zeros((LANES - len(rows), tm), F32)
    return jnp.concatenate(rows + [pad], axis=0).T


def _route_kernel(h_ref, rw_ref, rb_ref, g_ref):
    gates, _, _ = _select_experts(h_ref[...], rw_ref[...], rb_ref[...])
    tm = gates.shape[1]
    pad = jnp.zeros((LANES - N_EXPERTS, tm), F32)
    g_ref[...] = jnp.concatenate([gates, pad], axis=0).T


def _route_sorted_kernel(h_ref, rw_ref, rb_ref, g4_ref, eidx_ref, rank_ref, cnt_ref, base_ref):
    @pl.when(pl.program_id(0) == 0)
    def _():
        base_ref[...] = jnp.zeros_like(base_ref)

    gates, chosen, hits = _select_experts(h_ref[...], rw_ref[...], rb_ref[...])
    tm = gates.shape[1]
    ii = lax.broadcasted_iota(jnp.int32, (tm, tm), 0)
    jj = lax.broadcasted_iota(jnp.int32, (tm, tm), 1)
    upper = jnp.where(ii <= jj, 1.0, 0.0).astype(BF16)
    prefix = _dot(chosen.astype(BF16), upper)
    rank_full = base_ref[...] + prefix - 1.0
    row = lax.broadcasted_iota(jnp.int32, (N_EXPERTS, tm), 0).astype(F32)
    pick = lambda hit, val: jnp.sum(jnp.where(hit, val, 0.0), axis=0, keepdims=True)
    g4_ref[...] = _lanes_to_rows([pick(hit, gates) for hit in hits])
    eidx_ref[...] = jnp.concatenate([pick(hit, row) for hit in hits], axis=0).astype(jnp.int32)
    rank_ref[...] = jnp.concatenate([pick(hit, rank_full) for hit in hits], axis=0).astype(jnp.int32)
    base_ref[...] += jnp.sum(chosen, axis=1, keepdims=True)
    cnt_ref[...] = jnp.broadcast_to(base_ref[...], cnt_ref.shape)


def _route(h, rw, rb, tm):
    t, d = h.shape
    return pl.pallas_call(
        _route_kernel,
        grid=(t // tm,),
        in_specs=[
            pl.BlockSpec((tm, d), lambda i: (i, 0)),
            pl.BlockSpec((d, LANES), lambda i: (0, 0)),
            pl.BlockSpec((N_EXPERTS, 1), lambda i: (0, 0)),
        ],
        out_specs=pl.BlockSpec((tm, LANES), lambda i: (i, 0)),
        out_shape=jax.ShapeDtypeStruct((t, LANES), F32),
        compiler_params=_cparams(("parallel",)),
        name="route",
    )(h, rw, rb)


def _route_sorted(h, rw, rb, tm):
    t, d = h.shape
    pick = pl.BlockSpec((TOP_K, tm), lambda i: (0, i))
    return pl.pallas_call(
        _route_sorted_kernel,
        grid=(t // tm,),
        in_specs=[
            pl.BlockSpec((tm, d), lambda i: (i, 0)),
            pl.BlockSpec((d, LANES), lambda i: (0, 0)),
            pl.BlockSpec((N_EXPERTS, 1), lambda i: (0, 0)),
        ],
        out_specs=[pl.BlockSpec((tm, LANES), lambda i: (i, 0)), pick, pick,
                   pl.BlockSpec((N_EXPERTS, LANES), lambda i: (0, 0))],
        out_shape=[jax.ShapeDtypeStruct((t, LANES), F32),
                   jax.ShapeDtypeStruct((TOP_K, t), jnp.int32),
                   jax.ShapeDtypeStruct((TOP_K, t), jnp.int32),
                   jax.ShapeDtypeStruct((N_EXPERTS, LANES), F32)],
        scratch_shapes=[pltpu.VMEM((N_EXPERTS, 1), F32)],
        compiler_params=_cparams(("arbitrary",)),
        name="route_sorted",
    )(h, rw, rb)


def _swiglu_hidden(x, wg, wu):
    hg = _dot(x, wg)
    return hg * _sigmoid(hg) * _dot(x, wu)


ROW_TILE = 512


def _pack_rows(x):
    n = x.shape[1] // 2
    bits = lax.bitcast_convert_type(x.astype(BF16).astype(F32), jnp.uint32)
    return bits[:, :n] | (bits[:, n:] >> 16)


def _unpack_rows(w):
    hi = lax.bitcast_convert_type(w & jnp.uint32(0xFFFF0000), F32)
    lo = lax.bitcast_convert_type(w << 16, F32)
    return jnp.concatenate([hi, lo], axis=1)


def _row_copies(pos_ref, tm, make_copy):
    def body(t, carry):
        for k in range(TOP_K):
            make_copy(k, t, pos_ref[0, 0, k * tm + t]).start()
        return carry
    lax.fori_loop(0, tm, body, 0, unroll=8)


def _dispatch_kernel(ends_ref, sizes_ref, pos_ref, hp_ref, xs_ref, zero_ref, sem, zsem):
    tm = hp_ref.shape[0]

    @pl.when(pl.program_id(0) == 0)
    def _():
        zero_ref[...] = jnp.zeros_like(zero_ref)
        n_rows = xs_ref.shape[0]

        def zero_tile(r0):
            return pltpu.make_async_copy(
                zero_ref, xs_ref.at[pl.ds(pl.multiple_of(r0, ROW_TILE), ROW_TILE), :], zsem)

        for start_or_wait in ("start", "wait"):
            for e in range(N_EXPERTS):
                @pl.when(sizes_ref[e] > 0)
                def _():
                    getattr(zero_tile(ends_ref[e] - ROW_TILE), start_or_wait)()

                tail = ends_ref[N_EXPERTS - 1] + e * ROW_TILE

                @pl.when(tail < n_rows)
                def _():
                    getattr(zero_tile(tail), start_or_wait)()

    _row_copies(pos_ref, tm, lambda k, t, p: pltpu.make_async_copy(
        hp_ref.at[pl.ds(t, 1), :], xs_ref.at[pl.ds(p, 1), :], sem))
    n = TOP_K * tm
    pltpu.make_async_copy(xs_ref.at[pl.ds(0, n), :], xs_ref.at[pl.ds(0, n), :], sem).wait()


def _dispatch(ends, sizes, pos_tiles, hp, n_rows, tm):
    t, half = hp.shape
    return pl.pallas_call(
        _dispatch_kernel,
        grid_spec=pltpu.PrefetchScalarGridSpec(
            num_scalar_prefetch=2,
            grid=(t // tm,),
            in_specs=[
                pl.BlockSpec((1, 1, TOP_K * tm), lambda i, en, sz: (i, 0, 0), memory_space=pltpu.SMEM),
                pl.BlockSpec((tm, half), lambda i, en, sz: (i, 0)),
            ],
            out_specs=pl.BlockSpec(memory_space=pl.ANY),
            scratch_shapes=[pltpu.VMEM((ROW_TILE, half), jnp.uint32),
                            pltpu.SemaphoreType.DMA(()), pltpu.SemaphoreType.DMA(())],
        ),
        out_shape=jax.ShapeDtypeStruct((n_rows, half), jnp.uint32),
        compiler_params=_cparams(("arbitrary",)),
        name="dispatch",
    )(ends, sizes, pos_tiles, hp)


def _experts_kernel(te_ref, nu_ref, xs_ref, wg_ref, wu_ref, wd_ref, ys_ref, wgb_ref, wub_ref, wdb_ref):
    i = pl.program_id(0)

    @pl.when((i == 0) | (te_ref[i] != te_ref[jnp.maximum(i - 1, 0)]))
    def _():
        wgb_ref[...] = wg_ref[0, 0].astype(BF16)
        wub_ref[...] = wu_ref[0, 0].astype(BF16)
        wdb_ref[...] = wd_ref[0, 0].astype(BF16)

    @pl.when(i < nu_ref[0])
    def _():
        x = _unpack_rows(xs_ref[...]).astype(BF16)
        a = _swiglu_hidden(x, wgb_ref[...], wub_ref[...])
        ys_ref[...] = _pack_rows(_dot(a.astype(BF16), wdb_ref[...]))

    @pl.when(i >= nu_ref[0])
    def _():
        ys_ref[...] = jnp.zeros_like(ys_ref)


def _experts(tile_expert, n_used, xs, l, ew):
    wg, wu, wd = ew
    d, de = wg.shape[2], wg.shape[3]
    n_rows, half = xs.shape
    return pl.pallas_call(
        _experts_kernel,
        grid_spec=pltpu.PrefetchScalarGridSpec(
            num_scalar_prefetch=2,
            grid=(n_rows // ROW_TILE,),
            in_specs=[
                pl.BlockSpec((ROW_TILE, half), lambda i, te, nu: (i, 0)),
                pl.BlockSpec((1, 1, d, de), lambda i, te, nu: (l, te[i], 0, 0)),
                pl.BlockSpec((1, 1, d, de), lambda i, te, nu: (l, te[i], 0, 0)),
                pl.BlockSpec((1, 1, de, d), lambda i, te, nu: (l, te[i], 0, 0)),
            ],
            out_specs=pl.BlockSpec((ROW_TILE, half), lambda i, te, nu: (i, 0)),
            scratch_shapes=[pltpu.VMEM((d, de), BF16), pltpu.VMEM((d, de), BF16), pltpu.VMEM((de, d), BF16)],
        ),
        out_shape=jax.ShapeDtypeStruct((n_rows, half), jnp.uint32),
        compiler_params=_cparams(("arbitrary",)),
        name="experts",
    )(tile_expert, n_used, xs, wg, wu, wd)


def _combine_kernel(pos0_ref, pos_ref, g4_ref, h_ref, swg_ref, swu_ref, swd_ref, z1_ref, mod_ref,
                    l2g_ref, l2b_ref, ys_ref, o_ref, buf0_ref, buf1_ref, sems, *, alpha, tm):
    j = pl.program_id(0)
    bufs = (buf0_ref, buf1_ref)
    n = TOP_K * tm

    def gather(idx_ref, off, slot):
        for k in range(TOP_K):
            for t in range(tm):
                pltpu.make_async_copy(ys_ref.at[pl.ds(idx_ref[0, 0, off + k * tm + t], 1), :],
                                      bufs[slot].at[k, pl.ds(t, 1), :], sems.at[slot]).start()

    def wait_tile(slot):
        for k in range(TOP_K):
            pltpu.make_async_copy(ys_ref.at[pl.ds(0, tm), :], bufs[slot].at[k], sems.at[slot]).wait()

    @pl.when(j == 0)
    def _():
        gather(pos0_ref, 0, 0)

    gate = mod_ref[0, 5:6, :]
    for slot in range(2):
        rows = slice(slot * tm, (slot + 1) * tm)
        wait_tile(slot)
        gather(pos_ref, slot * n, 1 - slot)
        a = _swiglu_hidden(h_ref[rows, :], swg_ref[...], swu_ref[...])
        acc = _dot(a.astype(BF16), swd_ref[...])
        for k in range(TOP_K):
            acc = acc + g4_ref[rows, k:k + 1] * _unpack_rows(bufs[slot][k])
        o_ref[rows, :] = _ln(alpha * z1_ref[rows, :] + gate * acc) * l2g_ref[...] + l2b_ref[...]

    @pl.when(j == pl.num_programs(0) - 1)
    def _():
        wait_tile(0)


def _combine(pos_tiles, g4, h, ys, w, z1, mod, tm, tiles_per_batch, alpha):
    t, d = h.shape
    de = w["sh_wg"].shape[1]
    half = ys.shape[1]
    nt = t // tm
    assert nt % 2 == 0 and tiles_per_batch % 2 == 0
    n = TOP_K * tm
    nxt = jnp.minimum(jnp.arange(nt) + 1, nt - 1)
    pos_next = pos_tiles[nxt].reshape(nt // 2, 1, 2 * n)
    full = lambda shape: pl.BlockSpec(shape, lambda i: (0,) * len(shape))
    two = lambda width: pl.BlockSpec((2 * tm, width), lambda i: (i, 0))
    body = functools.partial(_combine_kernel, alpha=alpha, tm=tm)
    buf = pltpu.VMEM((TOP_K, tm, half), jnp.uint32)
    return pl.pallas_call(
        body,
        grid=(nt // 2,),
        in_specs=[
            pl.BlockSpec((1, 1, n), lambda i: (0, 0, 0), memory_space=pltpu.SMEM),
            pl.BlockSpec((1, 1, 2 * n), lambda i: (i, 0, 0), memory_space=pltpu.SMEM),
            two(LANES), two(d),
            full((d, de)), full((d, de)), full((de, d)),
            two(d),
            pl.BlockSpec((1, 6, d), lambda i: (2 * i // tiles_per_batch, 0, 0)),
            full((1, d)), full((1, d)),
            pl.BlockSpec(memory_space=pl.ANY),
        ],
        out_specs=two(d),
        out_shape=jax.ShapeDtypeStruct((t, d), F32),
        scratch_shapes=[buf, buf, pltpu.SemaphoreType.DMA((2,))],
        compiler_params=_cparams(("arbitrary",)),
        name="combine",
    )(pos_tiles, pos_next, g4, h, w["sh_wg"], w["sh_wu"], w["sh_wd"], z1, mod, w["ln2_g"], w["ln2_b"], ys)


def _moe_sorted(h, hp, l, ew, w, z1, mod, tm, tiles_per_batch, alpha):
    t = h.shape[0]
    g4, eidx, rank, cnt = _route_sorted(h, w["router_w"], w["router_b"], tm)
    counts = cnt[:, 0].astype(jnp.int32)
    sizes = (counts + ROW_TILE - 1) // ROW_TILE * ROW_TILE
    ends = jnp.cumsum(sizes)
    starts = ends - sizes
    n_tiles = t * TOP_K // ROW_TILE + N_EXPERTS
    n_used = (ends[-1] // ROW_TILE).reshape(1)
    tile_expert = jnp.minimum(
        jnp.sum(jnp.arange(n_tiles)[:, None] >= (ends // ROW_TILE)[None, :], axis=1), N_EXPERTS - 1
    ).astype(jnp.int32)
    onehot = eidx[:, :, None] == jnp.arange(N_EXPERTS)[None, None, :]
    pos = rank + jnp.sum(jnp.where(onehot, starts[None, None, :], 0), axis=-1)
    pos_tiles = pos.reshape(TOP_K, t // tm, tm).transpose(1, 0, 2).reshape(t // tm, 1, TOP_K * tm)
    xs = _dispatch(ends.astype(jnp.int32), sizes.astype(jnp.int32), pos_tiles, hp, n_tiles * ROW_TILE, tm)
    ys = _experts(tile_expert, n_used, xs, l, ew)
    return _combine(pos_tiles, g4, h, ys, w, z1, mod, tm, tiles_per_batch, alpha)


def _moe_kernel(h_ref, g_ref, wg_ref, wu_ref, wd_ref, swg_ref, swu_ref, swd_ref, z1_ref, mod_ref,
                l2g_ref, l2b_ref, o_ref, acc_ref, *, alpha):
    e = pl.program_id(1)
    x = h_ref[...]

    @pl.when(e == 0)
    def _():
        a = _swiglu_hidden(x, swg_ref[...], swu_ref[...])
        acc_ref[...] = _dot(a.astype(BF16), swd_ref[...])

    lane = lax.broadcasted_iota(jnp.int32, g_ref.shape, 1)
    gcol = jnp.sum(jnp.where(lane == e, g_ref[...], 0.0), axis=1, keepdims=True)
    a = _swiglu_hidden(x, wg_ref[0, 0].astype(BF16), wu_ref[0, 0].astype(BF16)) * gcol
    acc_ref[...] += _dot(a.astype(BF16), wd_ref[0, 0].astype(BF16))

    @pl.when(e == pl.num_programs(1) - 1)
    def _():
        gate = mod_ref[0, 5:6, :]
        o_ref[...] = _ln(alpha * z1_ref[...] + gate * acc_ref[...]) * l2g_ref[...] + l2b_ref[...]


def _moe(h, gates, l, ew, w, z1, mod, tm, tiles_per_batch, alpha):
    t, d = h.shape
    wg, wu, wd = ew
    ne, de = wg.shape[1], wg.shape[3]
    body = functools.partial(_moe_kernel, alpha=alpha)
    return pl.pallas_call(
        body,
        grid=(t // tm, ne),
        in_specs=[
            pl.BlockSpec((tm, d), lambda i, e: (i, 0)),
            pl.BlockSpec((tm, LANES), lambda i, e: (i, 0)),
            pl.BlockSpec((1, 1, d, de), lambda i, e: (l, e, 0, 0)),
            pl.BlockSpec((1, 1, d, de), lambda i, e: (l, e, 0, 0)),
            pl.BlockSpec((1, 1, de, d), lambda i, e: (l, e, 0, 0)),
            pl.BlockSpec((d, de), lambda i, e: (0, 0)),
            pl.BlockSpec((d, de), lambda i, e: (0, 0)),
            pl.BlockSpec((de, d), lambda i, e: (0, 0)),
            pl.BlockSpec((tm, d), lambda i, e: (i, 0)),
            pl.BlockSpec((1, 6, d), lambda i, e: (i // tiles_per_batch, 0, 0)),
            pl.BlockSpec((1, d), lambda i, e: (0, 0)),
            pl.BlockSpec((1, d), lambda i, e: (0, 0)),
        ],
        out_specs=pl.BlockSpec((tm, d), lambda i, e: (i, 0)),
        out_shape=jax.ShapeDtypeStruct((t, d), F32),
        scratch_shapes=[pltpu.VMEM((tm, d), F32)],
        compiler_params=_cparams(("parallel", "arbitrary")),
        name="moe",
    )(h, gates, wg, wu, wd, w["sh_wg"], w["sh_wu"], w["sh_wd"], z1, mod, w["ln2_g"], w["ln2_b"])


def _rope_swap(r):
    r4 = r.reshape(r.shape[:-1] + (2, 2, ROPE_AXIS // 2))
    return jnp.stack([-r4[..., 1, :], r4[..., 0, :]], axis=-2).reshape(r.shape)


def _prep_layer(l, w_in, q_norm_g, w_uq, kv_norm_g, w_ukv, pool_w, pool_scale, conv_dw, conv_b,
                conv_ln_g, conv_ln_b, conv_pw, w_out, ln1_g, ln1_b, router_w, router_bias,
                exp_wg, exp_wu, exp_wd, sh_wg, sh_wu, sh_wd, ln2_g, ln2_b):
    d = w_in.shape[1]
    wi = w_in[l]
    w_in_p = jnp.concatenate([
        wi[:, 0:OFF_KR], wi[:, OFF_KR:OFF_POOL], jnp.zeros((d, PC_POOL - PC_KR - QK_ROPE), F32),
        wi[:, OFF_POOL:]], axis=1).astype(BF16)

    uq = w_uq[l].reshape(Q_LORA, MLA_HEADS, QK_NOPE + QK_ROPE)
    nope, rope = uq[..., :QK_NOPE], uq[..., QK_NOPE:]
    wqat = jnp.concatenate([nope, rope, rope], axis=-1).reshape(Q_LORA, QK_WIDTH).T.astype(BF16)
    wqbt = jnp.concatenate([jnp.zeros_like(nope), jnp.zeros_like(rope), _rope_swap(rope)],
                           axis=-1).reshape(Q_LORA, QK_WIDTH).T.astype(BF16)

    ukv = w_ukv[l].reshape(KV_LORA, MLA_HEADS, QK_NOPE + V_DIM)
    kn, vv = ukv[..., :QK_NOPE], ukv[..., QK_NOPE:]
    z32 = jnp.zeros((KV_LORA, MLA_HEADS, QK_ROPE), F32)
    top = jnp.concatenate([kn, z32, z32], axis=-1)
    eye = jnp.broadcast_to(jnp.eye(QK_ROPE, dtype=F32)[:, None, :], (QK_ROPE, MLA_HEADS, QK_ROPE))
    zr = jnp.zeros((QK_ROPE, MLA_HEADS, QK_ROPE), F32)
    zn = jnp.zeros((QK_ROPE, MLA_HEADS, QK_NOPE), F32)
    padrows = jnp.zeros((KV_LORA - QK_ROPE, MLA_HEADS, HEAD_PAD), F32)

    def kmat(kr_rows):
        return jnp.concatenate([top, kr_rows, padrows], axis=0).reshape(2 * KV_LORA, QK_WIDTH).astype(BF16)

    wka_lat = kmat(jnp.concatenate([zn, zr, eye], axis=-1))
    wkb_lat = kmat(jnp.concatenate([zn, zr, _rope_swap(eye)], axis=-1))
    wka_ctx = kmat(jnp.concatenate([zn, eye, zr], axis=-1))
    wkb_ctx = jnp.zeros((2 * KV_LORA, QK_WIDTH), BF16)

    wv = jnp.concatenate([vv, jnp.zeros((KV_LORA, MLA_HEADS, V_ROWS - V_DIM), F32)], axis=-1)
    wvt = wv.reshape(KV_LORA, VT_WIDTH).T.astype(BF16)
    vbt = (jnp.arange(VT_WIDTH) % V_ROWS == V_DIM).astype(F32)[:, None]

    pw = jnp.zeros((POOL_WIDTH, POOL_WIDTH), F32)
    for gi in range(len(POOL_WINDOWS)):
        pw = pw.at[gi * POOL_GROUP:(gi + 1) * POOL_GROUP, gi * POOL_GROUP:(gi + 1) * POOL_GROUP].set(pool_w[l, gi])

    common = dict(
        w_in=w_in_p, q_g=q_norm_g[l][None], kv_g=kv_norm_g[l][None], wqat=wqat, wqbt=wqbt, wvt=wvt, vbt=vbt,
        pool_w=pw.astype(BF16), pool_s=pool_scale[l][None],
        conv_dw=jnp.concatenate([conv_dw[l], jnp.zeros((1, CONV_WIDTH), F32)], axis=0),
        conv_b=conv_b[l][None], conv_ln_g=conv_ln_g[l][None], conv_ln_b=conv_ln_b[l][None],
        conv_pw=conv_pw[l].astype(BF16), w_out=w_out[l].astype(BF16),
        ln1_g=ln1_g[l][None], ln1_b=ln1_b[l][None],
        router_w=jnp.pad(router_w[l], ((0, 0), (0, LANES - N_EXPERTS))).astype(BF16),
        router_b=router_bias[l][:, None],
        sh_wg=sh_wg[l].astype(BF16), sh_wu=sh_wu[l].astype(BF16), sh_wd=sh_wd[l].astype(BF16),
        ln2_g=ln2_g[l][None], ln2_b=ln2_b[l][None],
    )
    return dict(common, wka=wka_lat, wkb=wkb_lat), dict(common, wka=wka_ctx, wkb=wkb_ctx)


def _rope_tables(n_lat, n_ctx):
    n_rows = n_lat // GRID_W
    inv = ROPE_BASE ** (-jnp.arange(0, ROPE_AXIS, 2, dtype=F32) / ROPE_AXIS)
    ang_r = jnp.arange(n_rows, dtype=F32)[:, None] * inv
    ang_c = jnp.arange(GRID_W, dtype=F32)[:, None] * inv
    by_row = lambda a: jnp.repeat(a, GRID_W, axis=0)
    by_col = lambda a: jnp.tile(a, (n_rows, 1))
    cos_r, sin_r, cos_c, sin_c = by_row(jnp.cos(ang_r)), by_row(jnp.sin(ang_r)), by_col(jnp.cos(ang_c)), by_col(jnp.sin(ang_c))
    cos = jnp.concatenate([cos_r, cos_r, cos_c, cos_c], axis=1)
    sin = jnp.concatenate([sin_r, sin_r, sin_c, sin_c], axis=1)
    ck = jnp.concatenate([jnp.ones((n_lat, HEAD_PAD - QK_ROPE), F32), cos], axis=1)
    sk = jnp.concatenate([jnp.zeros((n_lat, HEAD_PAD - QK_ROPE), F32), sin], axis=1)
    qs = ATTN_SCALE * LOG2E
    lat = ((ck * qs).T, (sk * qs).T, ck, sk)
    one = jnp.ones((n_ctx, HEAD_PAD), F32)
    zero = jnp.zeros((n_ctx, HEAD_PAD), F32)
    ctx = ((one * qs).T, zero.T, one, zero)
    return lat, ctx


def _sublayers(z, mod, l, ew, w, att, up, uc, tm, tm_moe, alpha, sorted_moe):
    b, s, d = z.shape
    z1, h, hp = _mix(att, up, uc, z, mod, w, tm, alpha)
    h2, z2 = h.reshape(b * s, d), z1.reshape(b * s, d)
    if sorted_moe:
        out = _moe_sorted(h2, hp.reshape(b * s, d // 2), l, ew, w, z2, mod, tm, s // tm, alpha)
    else:
        gates = _route(h2, w["router_w"], w["router_b"], tm)
        out = _moe(h2, gates, l, ew, w, z2, mod, tm_moe, s // tm_moe, alpha)
    return out.reshape(b, s, d)


def kernel(x, c, ctx, c_ctx, ada_w, ada_b, w_in, q_norm_g, w_uq, kv_norm_g, w_ukv, pool_w, pool_scale, conv_dw, conv_b, conv_ln_g, conv_ln_b, conv_pw, w_out, ln1_g, ln1_b, router_w, router_bias, exp_wg, exp_wu, exp_wd, sh_wg, sh_wu, sh_wd, ln2_g, ln2_b):
    b, s, d = x.shape
    n_ctx = ctx.shape[1]
    depth = ada_w.shape[0]
    alpha = (2 * depth) ** 0.25
    assert b + 1 <= 8 and s % GRID_W == 0

    tm_l = min(512, s)
    tm_c = min(256, n_ctx)
    tm_moe_l = min(1024, s)
    tm_moe_c = n_ctx
    bq = min(1024, s)

    cond = jnp.concatenate([c, c_ctx[None], jnp.zeros((8 - b - 1, d), F32)], axis=0)
    mods = _ada(cond, ada_w, ada_b)
    tab_l, tab_c = _rope_tables(s, n_ctx)
    ew = (exp_wg, exp_wu, exp_wd)

    zl, zc = x, ctx
    for l in range(depth):
        last = l == depth - 1
        w_l, w_c = _prep_layer(l, w_in, q_norm_g, w_uq, kv_norm_g, w_ukv, pool_w, pool_scale,
                               conv_dw, conv_b, conv_ln_g, conv_ln_b, conv_pw, w_out, ln1_g, ln1_b,
                               router_w, router_bias, exp_wg, exp_wu, exp_wd, sh_wg, sh_wu, sh_wd,
                               ln2_g, ln2_b)
        mod_l = mods[l, :b].reshape(b, 6, d)
        mod_c = jnp.broadcast_to(mods[l, b].reshape(1, 6, d), (b, 6, d))

        q_l, k_l, v_l, up_l, uc_l = _proj(zl, mod_l, w_l, tab_l, tm_l)
        q_c, k_c, v_c, up_c, uc_c = _proj(zc, mod_c, w_c, tab_c, tm_c)
        att_l = _attn(q_l, k_c, v_c, k_l, v_l, bq)
        zl = _sublayers(zl, mod_l, l, ew, w_l, att_l, up_l, uc_l, tm_l, tm_moe_l, alpha, True)
        if not last:
            att_c = _attn(q_c, k_c, v_c, None, None, tm_c)
            zc = _sublayers(zc, mod_c, l, ew, w_c, att_c, up_c, uc_c, tm_c, tm_moe_c, alpha, False)
    return zl
```

```python
import functools
import math

import jax
import jax.numpy as jnp
from jax import lax
from jax.experimental import pallas as pl
from jax.experimental.pallas import tpu as pltpu

GRID_W = 64
MLA_HEADS = 8
QK_NOPE = 64
QK_ROPE = 32
V_DIM = 64
Q_LORA = 256
KV_LORA = 128
ROPE_AXIS = QK_ROPE // 2
ROPE_BASE = 10000.0
ATTN_SCALE = (QK_NOPE + QK_ROPE) ** -0.5
POOL_WINDOWS = (2, 4, 8, 16)
POOL_GROUP = 64
POOL_WIDTH = POOL_GROUP * len(POOL_WINDOWS)
CONV_WIDTH = 256
CONV_K = 31
MLA_WIDTH = MLA_HEADS * V_DIM
OFF_KV = Q_LORA
OFF_KR = OFF_KV + KV_LORA
OFF_POOL = OFF_KR + QK_ROPE
OFF_CONV = OFF_POOL + POOL_WIDTH
N_EXPERTS = 32
TOP_K = 4
N_GROUPS = 4
TOPK_GROUPS = 2
GROUP_SIZE = N_EXPERTS // N_GROUPS
ROUTED_SCALE = 2.5
EPS = 1e-6

LANES = 128
HEAD_PAD = LANES
QK_WIDTH = MLA_HEADS * HEAD_PAD
V_ROWS = 80
VT_WIDTH = MLA_HEADS * V_ROWS
HALO = 16
P_COLS = 1280
PC_Q, PC_KV, PC_KR, PC_POOL, PC_CA, PC_CG = 0, 256, 384, 512, 768, 1024
VMEM_LIMIT = 48 * 1024 * 1024
LOG2E = math.log2(math.e)

F32 = jnp.float32
BF16 = jnp.bfloat16


def _cparams(sem, flags=None):
    return pltpu.CompilerParams(dimension_semantics=sem, vmem_limit_bytes=VMEM_LIMIT, flags=flags)


def _ln(x):
    mu = jnp.mean(x, axis=-1, keepdims=True)
    xc = x - mu
    var = jnp.mean(xc * xc, axis=-1, keepdims=True)
    return xc * lax.rsqrt(var + EPS)


def _rms(x):
    return x * lax.rsqrt(jnp.mean(x * x, axis=-1, keepdims=True) + EPS)


def _sigmoid(x):
    return 1.0 / (1.0 + jnp.exp(-x))


def _dot(a, b):
    return jnp.dot(a, b, preferred_element_type=F32)


def _dot_nt(a, b):
    return lax.dot_general(a, b, (((1,), (1,)), ((), ())), preferred_element_type=F32)


def _ada_kernel(c_ref, w_ref, b_ref, o_ref):
    x = c_ref[...]
    x = x * _sigmoid(x)
    o_ref[0] = _dot(x.astype(BF16), w_ref[0].astype(BF16)) + b_ref[0]


def _ada(cond, ada_w, ada_b):
    depth, d, n = ada_w.shape
    tn = 1536
    return pl.pallas_call(
        _ada_kernel,
        grid=(depth, n // tn),
        in_specs=[
            pl.BlockSpec((8, d), lambda l, j: (0, 0)),
            pl.BlockSpec((1, d, tn), lambda l, j: (l, 0, j)),
            pl.BlockSpec((1, 1, tn), lambda l, j: (l, 0, j)),
        ],
        out_specs=pl.BlockSpec((1, 8, tn), lambda l, j: (l, 0, j)),
        out_shape=jax.ShapeDtypeStruct((depth, 8, n), F32),
        compiler_params=_cparams(("parallel", "parallel")),
        name="ada",
    )(cond, ada_w, ada_b.reshape(depth, 1, n))


def _proj_kernel(z_ref, mod_ref, win_ref, qg_ref, kvg_ref, wqat_ref, wqbt_ref, wka_ref, wkb_ref,
                 wvt_ref, vbt_ref, cqt_ref, sqt_ref, ck_ref, sk_ref,
                 qt_ref, k_ref, vt_ref, up_ref, uc_ref):
    z = z_ref[0]
    shift = mod_ref[0, 0:1, :]
    scale = mod_ref[0, 1:2, :]
    h = _ln(z) * (1.0 + scale) + shift
    p = _dot(h.astype(BF16), win_ref[...])
    qn = (_rms(p[:, PC_Q:PC_KV]) * qg_ref[...]).astype(BF16)
    kvn = (_rms(p[:, PC_KV:PC_KR]) * kvg_ref[...]).astype(BF16)
    xk = jnp.concatenate([kvn, p[:, PC_KR:PC_POOL].astype(BF16)], axis=1)
    qat = _dot_nt(wqat_ref[...], qn)
    qbt = _dot_nt(wqbt_ref[...], qn)
    ka = _dot(xk, wka_ref[...])
    kb = _dot(xk, wkb_ref[...])
    cqt, sqt, ck, sk = cqt_ref[...], sqt_ref[...], ck_ref[...], sk_ref[...]
    for hd in range(MLA_HEADS):
        sl = slice(hd * HEAD_PAD, (hd + 1) * HEAD_PAD)
        qt_ref[0, sl, :] = (qat[sl, :] * cqt + qbt[sl, :] * sqt).astype(BF16)
        k_ref[0, :, sl] = (ka[:, sl] * ck + kb[:, sl] * sk).astype(BF16)
    vt_ref[0, 0] = (_dot_nt(wvt_ref[...], kvn) + vbt_ref[...]).astype(BF16)
    up_ref[0] = p[:, PC_POOL:PC_CA]
    uc_ref[0] = p[:, PC_CA:PC_CG] * _sigmoid(p[:, PC_CG:P_COLS])


def _proj(z, mod, w, tabs, tm):
    b, s, d = z.shape
    full = lambda shape: pl.BlockSpec(shape, lambda i, j: (0,) * len(shape))
    tok = lambda width: pl.BlockSpec((1, tm, width), lambda i, j: (i, j, 0))
    tab = pl.BlockSpec((tm, HEAD_PAD), lambda i, j: (j, 0))
    tab_t = pl.BlockSpec((HEAD_PAD, tm), lambda i, j: (0, j))
    return pl.pallas_call(
        _proj_kernel,
        grid=(b, s // tm),
        in_specs=[
            tok(d),
            pl.BlockSpec((1, 6, d), lambda i, j: (i, 0, 0)),
            full((d, P_COLS)), full((1, Q_LORA)), full((1, KV_LORA)),
            full((QK_WIDTH, Q_LORA)), full((QK_WIDTH, Q_LORA)),
            full((2 * KV_LORA, QK_WIDTH)), full((2 * KV_LORA, QK_WIDTH)),
            full((VT_WIDTH, KV_LORA)), full((VT_WIDTH, 1)),
            tab_t, tab_t, tab, tab,
        ],
        out_specs=[
            pl.BlockSpec((1, QK_WIDTH, tm), lambda i, j: (i, 0, j)),
            tok(QK_WIDTH),
            pl.BlockSpec((1, 1, VT_WIDTH, tm), lambda i, j: (i, j, 0, 0)),
            tok(POOL_WIDTH), tok(CONV_WIDTH)],
        out_shape=[
            jax.ShapeDtypeStruct((b, QK_WIDTH, s), BF16),
            jax.ShapeDtypeStruct((b, s, QK_WIDTH), BF16),
            jax.ShapeDtypeStruct((b, s // tm, VT_WIDTH, tm), BF16),
            jax.ShapeDtypeStruct((b, s, POOL_WIDTH), F32),
            jax.ShapeDtypeStruct((b, s, CONV_WIDTH), F32),
        ],
        compiler_params=_cparams(("parallel", "parallel")),
        name="proj",
    )(z, mod, w["w_in"], w["q_g"], w["kv_g"], w["wqat"], w["wqbt"], w["wka"], w["wkb"],
      w["wvt"], w["vbt"], *tabs)


HEADS_PER_STEP = 2
QCOLS = 256
ATTN_UNROLL = 4


def _attn_body(qt_ref, kc_ref, vct_ref, kl_ref, vlt_ref, o_ref, s_refs):
    heads = range(HEADS_PER_STEP)
    sls = [slice(hh * HEAD_PAD, (hh + 1) * HEAD_PAD) for hh in heads]
    vsl = [slice(hh * V_ROWS, (hh + 1) * V_ROWS) for hh in heads]
    qts = [qt_ref[0, sl, :] for sl in sls]
    ms, accs = [], []
    for hh in heads:
        st = _dot(kc_ref[0, :, sls[hh]], qts[hh])
        m = jnp.max(st, axis=0, keepdims=True)
        pt = jnp.exp2(st - m)
        ms.append(m)
        accs.append(_dot(vct_ref[0, 0, vsl[hh], :], pt.astype(BF16)))

    if kl_ref is not None:
        n, bk = vlt_ref.shape[1], vlt_ref.shape[3]
        assert n >= 2 and n % 2 == 0

        def score(c, slot):
            r0 = c * bk if isinstance(c, int) else pl.multiple_of(c * bk, bk)
            cms = []
            for hh in heads:
                st = _dot(kl_ref[0, pl.ds(r0, bk), sls[hh]], qts[hh])
                s_refs[slot][hh] = st
                cms.append(jnp.max(st, axis=0, keepdims=True))
            return cms

        def half(c, cur, nxt, ms, cms, accs, with_score=True):
            m_new = [jnp.maximum(ms[hh], cms[hh]) for hh in heads]
            alphas = [jnp.exp2(ms[hh] - m_new[hh]) for hh in heads]
            if with_score:
                cms = score(c + 1, nxt)
            out = []
            for hh in heads:
                cols = []
                for g in range(0, qts[hh].shape[1], QCOLS):
                    gs = slice(g, g + QCOLS)
                    pt = jnp.exp2(s_refs[cur][hh, :, gs] - m_new[hh][:, gs]).astype(BF16)
                    cols.append(accs[hh][:, gs] * alphas[hh][:, gs]
                                + _dot(vlt_ref[0, c, vsl[hh], :], pt))
                out.append(jnp.concatenate(cols, axis=1))
            return m_new, cms, out

        cms = score(0, 0)
        unroll = ATTN_UNROLL if n % ATTN_UNROLL == 0 else 2
        trips = (n - 2) // unroll

        def steps(t, carry):
            ms, cms, accs = [list(x) for x in carry]
            for u in range(unroll):
                ms, cms, accs = half(unroll * t + u, u % 2, (u + 1) % 2, ms, cms, accs)
            return tuple(tuple(x) for x in (ms, cms, accs))

        carry = tuple(tuple(x) for x in (ms, cms, accs))
        carry = lax.fori_loop(0, trips, steps, carry)
        ms, cms, accs = [list(x) for x in carry]
        for c in range(trips * unroll, n):
            ms, cms, accs = half(c, c % 2, (c + 1) % 2, ms, cms, accs, with_score=c + 1 < n)

    ot = jnp.concatenate([acc[0:V_DIM, :] / acc[V_DIM:V_DIM + 1, :] for acc in accs], axis=0)
    o_ref[0] = ot.T.astype(o_ref.dtype)


def _attn_kernel_full(qt_ref, kc_ref, vct_ref, kl_ref, vlt_ref, o_ref, s0_ref, s1_ref):
    _attn_body(qt_ref, kc_ref, vct_ref, kl_ref, vlt_ref, o_ref, (s0_ref, s1_ref))


def _attn_kernel_ctx(qt_ref, kc_ref, vct_ref, o_ref):
    _attn_body(qt_ref, kc_ref, vct_ref, None, None, o_ref, None)


def _attn(qt, kc, vct, kl, vlt, bq):
    b, _, s = qt.shape
    nc = kc.shape[1]
    w2 = HEADS_PER_STEP * HEAD_PAD
    v2 = HEADS_PER_STEP * V_ROWS
    in_specs = [
        pl.BlockSpec((1, w2, bq), lambda i, h, j: (i, h, j)),
        pl.BlockSpec((1, nc, w2), lambda i, h, j: (i, 0, h)),
        pl.BlockSpec((1, 1, v2, nc), lambda i, h, j: (i, 0, h, 0)),
    ]
    args = [qt, kc, vct]
    scratch = []
    if kl is None:
        body = _attn_kernel_ctx
    else:
        bk = vlt.shape[3]
        scratch = [pltpu.VMEM((HEADS_PER_STEP, bk, bq), F32)] * 2
        in_specs += [
            pl.BlockSpec((1, kl.shape[1], w2), lambda i, h, j: (i, 0, h)),
            pl.BlockSpec((1, vlt.shape[1], v2, vlt.shape[3]), lambda i, h, j: (i, 0, h, 0)),
        ]
        args += [kl, vlt]
        body = _attn_kernel_full
    return pl.pallas_call(
        body,
        grid=(b, MLA_HEADS // HEADS_PER_STEP, s // bq),
        in_specs=in_specs,
        out_specs=pl.BlockSpec((1, bq, HEADS_PER_STEP * V_DIM), lambda i, h, j: (i, j, h)),
        out_shape=jax.ShapeDtypeStruct((b, s, MLA_WIDTH), BF16),
        scratch_shapes=scratch,
        compiler_params=_cparams(("parallel", "parallel", "arbitrary")),
        name="attn",
    )(*args)


SUBLANES = 8


def _fill_ext(ext_ref, sh_ref, main_ref, left_ref, right_ref, tm):
    j = pl.program_id(1)
    last = pl.num_programs(1) - 1
    ext_ref[0:HALO, :] = jnp.where(j > 0, left_ref[0], 0.0)
    ext_ref[HALO:HALO + tm, :] = main_ref[0]
    ext_ref[HALO + tm:HALO + tm + HALO, :] = jnp.where(j < last, right_ref[0], 0.0)
    n = sh_ref.shape[1]
    for b in range(1, SUBLANES):
        sh_ref[b - 1, :, :] = ext_ref[b:b + n, :]


def _mix_kernel(att_ref, up_ref, upl_ref, upr_ref, uc_ref, ucl_ref, ucr_ref, z_ref, mod_ref,
                poolw_ref, pools_ref, dw_ref, cb_ref, clg_ref, clb_ref, cpw_ref, wout_ref,
                l1g_ref, l1b_ref, z1_ref, h_ref, hp_ref, ext_ref, sh_ref, *, tm, seq, alpha):
    j = pl.program_id(1)

    def shifted(d):
        a, b = divmod(HALO + d, SUBLANES)
        if b == 0:
            return ext_ref[a * SUBLANES:a * SUBLANES + tm, :]
        return sh_ref[b - 1, a * SUBLANES:a * SUBLANES + tm, :]

    _fill_ext(ext_ref, sh_ref, up_ref, upl_ref, upr_ref, tm)
    x = up_ref[0]
    lane = lax.broadcasted_iota(jnp.int32, (tm, POOL_WIDTH), 1)
    t = lax.broadcasted_iota(jnp.int32, (tm, POOL_WIDTH), 0) + j * tm

    run = x + shifted(-1)
    sums = [run]
    for lo in (2, 4, 8):
        for d in list(range(-lo, -lo // 2)) + list(range(lo // 2, lo)):
            run = run + shifted(d)
        sums.append(run)
    wsum = jnp.where(lane < POOL_GROUP, sums[0],
                     jnp.where(lane < 2 * POOL_GROUP, sums[1],
                               jnp.where(lane < 3 * POOL_GROUP, sums[2], sums[3])))
    lo = jnp.where(lane < POOL_GROUP, 1,
                   jnp.where(lane < 2 * POOL_GROUP, 2, jnp.where(lane < 3 * POOL_GROUP, 4, 8)))
    cnt = jnp.minimum(t + lo, seq) - jnp.maximum(t - lo, 0)
    diff = wsum / cnt.astype(F32) - x
    pooled = _dot(diff.astype(BF16), poolw_ref[...]) * pools_ref[...]

    _fill_ext(ext_ref, sh_ref, uc_ref, ucl_ref, ucr_ref, tm)
    acc = jnp.zeros((tm, CONV_WIDTH), F32) + cb_ref[...]
    for k in range(CONV_K):
        acc = acc + dw_ref[k:k + 1, :] * shifted(k - CONV_K // 2)
    y = _ln(acc) * clg_ref[...] + clb_ref[...]
    y = y * _sigmoid(y)
    conv = _dot(y.astype(BF16), cpw_ref[...])

    cat = jnp.concatenate([att_ref[0], pooled.astype(BF16), conv.astype(BF16)], axis=1)
    out = _dot(cat, wout_ref[...])
    gate = mod_ref[0, 2:3, :]
    z1 = _ln(alpha * z_ref[0] + gate * out) * l1g_ref[...] + l1b_ref[...]
    z1_ref[0] = z1
    h = _ln(z1) * (1.0 + mod_ref[0, 4:5, :]) + mod_ref[0, 3:4, :]
    h_ref[0] = h.astype(BF16)
    hp_ref[0] = _pack_rows(h)


def _mix(att, up, uc, z, mod, w, tm, alpha):
    b, s, d = z.shape
    hb = tm // HALO
    nhb = s // HALO
    full = lambda shape: pl.BlockSpec(shape, lambda i, j: (0,) * len(shape))
    tok = lambda width: pl.BlockSpec((1, tm, width), lambda i, j: (i, j, 0))
    left = lambda width: pl.BlockSpec((1, HALO, width),
                                      lambda i, j: (i, jnp.maximum(j * hb - 1, 0), 0))
    right = lambda width: pl.BlockSpec((1, HALO, width),
                                       lambda i, j: (i, jnp.minimum((j + 1) * hb, nhb - 1), 0))
    body = functools.partial(_mix_kernel, tm=tm, seq=s, alpha=alpha)
    return pl.pallas_call(
        body,
        grid=(b, s // tm),
        in_specs=[
            tok(MLA_WIDTH),
            tok(POOL_WIDTH), left(POOL_WIDTH), right(POOL_WIDTH),
            tok(CONV_WIDTH), left(CONV_WIDTH), right(CONV_WIDTH),
            tok(d),
            pl.BlockSpec((1, 6, d), lambda i, j: (i, 0, 0)),
            full((POOL_WIDTH, POOL_WIDTH)), full((1, POOL_WIDTH)),
            full((CONV_K + 1, CONV_WIDTH)), full((1, CONV_WIDTH)), full((1, CONV_WIDTH)),
            full((1, CONV_WIDTH)), full((CONV_WIDTH, CONV_WIDTH)),
            full((d, d)), full((1, d)), full((1, d)),
        ],
        out_specs=[tok(d), tok(d), tok(d // 2)],
        out_shape=[jax.ShapeDtypeStruct((b, s, d), F32), jax.ShapeDtypeStruct((b, s, d), BF16),
                   jax.ShapeDtypeStruct((b, s, d // 2), jnp.uint32)],
        scratch_shapes=[pltpu.VMEM((tm + 2 * HALO, POOL_WIDTH), F32),
                        pltpu.VMEM((SUBLANES - 1, tm + 2 * HALO - SUBLANES, POOL_WIDTH), F32)],
        compiler_params=_cparams(("parallel", "parallel")),
        name="mix",
    )(att, up, up, up, uc, uc, uc, z, mod, w["pool_w"], w["pool_s"], w["conv_dw"], w["conv_b"],
      w["conv_ln_g"], w["conv_ln_b"], w["conv_pw"], w["w_out"], w["ln1_g"], w["ln1_b"])


def _select_experts(h, rw, rb):
    tm = h.shape[0]
    logits = _dot(h, rw)
    scores = _sigmoid(logits.T[0:N_EXPERTS, :])
    sel = scores + rb
    neg = -jnp.inf

    row8 = lax.broadcasted_iota(jnp.int32, (GROUP_SIZE, tm), 0)
    gscore = []
    for g in range(N_GROUPS):
        xg = sel[g * GROUP_SIZE:(g + 1) * GROUP_SIZE, :]
        m1 = jnp.max(xg, axis=0, keepdims=True)
        i1 = jnp.min(jnp.where(xg == m1, row8, GROUP_SIZE), axis=0, keepdims=True)
        m2 = jnp.max(jnp.where(row8 == i1, neg, xg), axis=0, keepdims=True)
        gscore.append(m1 + m2)
    masked = []
    for g in range(N_GROUPS):
        ahead = jnp.zeros((1, tm), F32)
        for g2 in range(N_GROUPS):
            if g2 < g:
                ahead = ahead + jnp.where(gscore[g2] >= gscore[g], 1.0, 0.0)
            elif g2 > g:
                ahead = ahead + jnp.where(gscore[g2] > gscore[g], 1.0, 0.0)
        keep = ahead < TOPK_GROUPS
        masked.append(jnp.where(keep, sel[g * GROUP_SIZE:(g + 1) * GROUP_SIZE, :], neg))
    masked = jnp.concatenate(masked, axis=0)

    row = lax.broadcasted_iota(jnp.int32, (N_EXPERTS, tm), 0)
    chosen = jnp.zeros((N_EXPERTS, tm), F32)
    hits = []
    for _ in range(TOP_K):
        m = jnp.max(masked, axis=0, keepdims=True)
        idx = jnp.min(jnp.where(masked == m, row, N_EXPERTS), axis=0, keepdims=True)
        hit = row == idx
        hits.append(hit)
        chosen = jnp.where(hit, 1.0, chosen)
        masked = jnp.where(hit, neg, masked)
    w = chosen * scores
    gates = w / jnp.sum(w, axis=0, keepdims=True) * ROUTED_SCALE
    return gates, chosen, hits


def _lanes_to_rows(rows):
    tm = rows[0].shape[1]
    pad = jnp.zeros((LANES - len(rows), tm), F32)
    return jnp.concatenate(rows + [pad], axis=0).T


def _route_sorted_kernel(h_ref, rw_ref, rb_ref, g4_ref, eidx_ref, rank_ref, cnt_ref, base_ref):
    @pl.when(pl.program_id(0) == 0)
    def _():
        base_ref[...] = jnp.zeros_like(base_ref)

    gates, chosen, hits = _select_experts(h_ref[...], rw_ref[...], rb_ref[...])
    tm = gates.shape[1]
    ii = lax.broadcasted_iota(jnp.int32, (tm, tm), 0)
    jj = lax.broadcasted_iota(jnp.int32, (tm, tm), 1)
    upper = jnp.where(ii <= jj, 1.0, 0.0).astype(BF16)
    prefix = _dot(chosen.astype(BF16), upper)
    rank_full = base_ref[...] + prefix - 1.0
    row = lax.broadcasted_iota(jnp.int32, (N_EXPERTS, tm), 0).astype(F32)
    pick = lambda hit, val: jnp.sum(jnp.where(hit, val, 0.0), axis=0, keepdims=True)
    g4_ref[...] = _lanes_to_rows([pick(hit, gates) for hit in hits])
    eidx_ref[...] = jnp.concatenate([pick(hit, row) for hit in hits], axis=0).astype(jnp.int32)
    rank_ref[...] = jnp.concatenate([pick(hit, rank_full) for hit in hits], axis=0).astype(jnp.int32)
    base_ref[...] += jnp.sum(chosen, axis=1, keepdims=True)
    cnt_ref[...] = jnp.broadcast_to(base_ref[...], cnt_ref.shape)


def _route_sorted(h, rw, rb, tm):
    t, d = h.shape
    pick = pl.BlockSpec((TOP_K, tm), lambda i: (0, i))
    return pl.pallas_call(
        _route_sorted_kernel,
        grid=(t // tm,),
        in_specs=[
            pl.BlockSpec((tm, d), lambda i: (i, 0)),
            pl.BlockSpec((d, LANES), lambda i: (0, 0)),
            pl.BlockSpec((N_EXPERTS, 1), lambda i: (0, 0)),
        ],
        out_specs=[pl.BlockSpec((tm, LANES), lambda i: (i, 0)), pick, pick,
                   pl.BlockSpec((N_EXPERTS, LANES), lambda i: (0, 0))],
        out_shape=[jax.ShapeDtypeStruct((t, LANES), F32),
                   jax.ShapeDtypeStruct((TOP_K, t), jnp.int32),
                   jax.ShapeDtypeStruct((TOP_K, t), jnp.int32),
                   jax.ShapeDtypeStruct((N_EXPERTS, LANES), F32)],
        scratch_shapes=[pltpu.VMEM((N_EXPERTS, 1), F32)],
        compiler_params=_cparams(("arbitrary",)),
        name="route_sorted",
    )(h, rw, rb)


def _swiglu_hidden(x, wg, wu):
    hg = _dot(x, wg)
    return hg * _sigmoid(hg) * _dot(x, wu)


ROW_TILE = 1024


def _pack_rows(x):
    n = x.shape[1] // 2
    bits = lax.bitcast_convert_type(x.astype(BF16).astype(F32), jnp.uint32)
    return bits[:, :n] | (bits[:, n:] >> 16)


def _unpack_rows(w):
    hi = lax.bitcast_convert_type(w & jnp.uint32(0xFFFF0000), F32)
    lo = lax.bitcast_convert_type(w << 16, F32)
    return jnp.concatenate([hi, lo], axis=1)


def _row_copies(pos_ref, tm, make_copy):
    def body(t, carry):
        for k in range(TOP_K):
            make_copy(k, t, pos_ref[0, 0, k * tm + t]).start()
        return carry
    lax.fori_loop(0, tm, body, 0, unroll=8)


def _dispatch_kernel(ends_ref, sizes_ref, pos_ref, hp_ref, xs_ref, zero_ref, sem, zsem):
    tm = hp_ref.shape[0]

    @pl.when(pl.program_id(0) == 0)
    def _():
        zero_ref[...] = jnp.zeros_like(zero_ref)
        n_rows = xs_ref.shape[0]

        def zero_tile(r0):
            return pltpu.make_async_copy(
                zero_ref, xs_ref.at[pl.ds(pl.multiple_of(r0, ROW_TILE), ROW_TILE), :], zsem)

        for start_or_wait in ("start", "wait"):
            for e in range(N_EXPERTS):
                @pl.when(sizes_ref[e] > 0)
                def _():
                    getattr(zero_tile(ends_ref[e] - ROW_TILE), start_or_wait)()

                tail = ends_ref[N_EXPERTS - 1] + e * ROW_TILE

                @pl.when(tail < n_rows)
                def _():
                    getattr(zero_tile(tail), start_or_wait)()

    _row_copies(pos_ref, tm, lambda k, t, p: pltpu.make_async_copy(
        hp_ref.at[pl.ds(t, 1), :], xs_ref.at[pl.ds(p, 1), :], sem))
    n = TOP_K * tm
    pltpu.make_async_copy(xs_ref.at[pl.ds(0, n), :], xs_ref.at[pl.ds(0, n), :], sem).wait()


def _dispatch(ends, sizes, pos_tiles, hp, n_rows, tm):
    t, half = hp.shape
    return pl.pallas_call(
        _dispatch_kernel,
        grid_spec=pltpu.PrefetchScalarGridSpec(
            num_scalar_prefetch=2,
            grid=(t // tm,),
            in_specs=[
                pl.BlockSpec((1, 1, TOP_K * tm), lambda i, en, sz: (i, 0, 0), memory_space=pltpu.SMEM),
                pl.BlockSpec((tm, half), lambda i, en, sz: (i, 0)),
            ],
            out_specs=pl.BlockSpec(memory_space=pl.ANY),
            scratch_shapes=[pltpu.VMEM((ROW_TILE, half), jnp.uint32),
                            pltpu.SemaphoreType.DMA(()), pltpu.SemaphoreType.DMA(())],
        ),
        out_shape=jax.ShapeDtypeStruct((n_rows, half), jnp.uint32),
        compiler_params=_cparams(("arbitrary",)),
        name="dispatch",
    )(ends, sizes, pos_tiles, hp)


def _experts_kernel(te_ref, nu_ref, xs_ref, wg_ref, wu_ref, wd_ref, ys_ref, wgb_ref, wub_ref, wdb_ref):
    i = pl.program_id(0)

    @pl.when((i == 0) | (te_ref[i] != te_ref[jnp.maximum(i - 1, 0)]))
    def _():
        wgb_ref[...] = wg_ref[0, 0].astype(BF16)
        wub_ref[...] = wu_ref[0, 0].astype(BF16)
        wdb_ref[...] = wd_ref[0, 0].astype(BF16)

    @pl.when(i < nu_ref[0])
    def _():
        x = _unpack_rows(xs_ref[...]).astype(BF16)
        a = _swiglu_hidden(x, wgb_ref[...], wub_ref[...])
        ys_ref[...] = _pack_rows(_dot(a.astype(BF16), wdb_ref[...]))

    @pl.when(i >= nu_ref[0])
    def _():
        ys_ref[...] = jnp.zeros_like(ys_ref)


def _experts(tile_expert, n_used, xs, l, ew):
    wg, wu, wd = ew
    d, de = wg.shape[2], wg.shape[3]
    n_rows, half = xs.shape
    return pl.pallas_call(
        _experts_kernel,
        grid_spec=pltpu.PrefetchScalarGridSpec(
            num_scalar_prefetch=2,
            grid=(n_rows // ROW_TILE,),
            in_specs=[
                pl.BlockSpec((ROW_TILE, half), lambda i, te, nu: (i, 0)),
                pl.BlockSpec((1, 1, d, de), lambda i, te, nu: (l, te[i], 0, 0)),
                pl.BlockSpec((1, 1, d, de), lambda i, te, nu: (l, te[i], 0, 0)),
                pl.BlockSpec((1, 1, de, d), lambda i, te, nu: (l, te[i], 0, 0)),
            ],
            out_specs=pl.BlockSpec((ROW_TILE, half), lambda i, te, nu: (i, 0)),
            scratch_shapes=[pltpu.VMEM((d, de), BF16), pltpu.VMEM((d, de), BF16), pltpu.VMEM((de, d), BF16)],
        ),
        out_shape=jax.ShapeDtypeStruct((n_rows, half), jnp.uint32),
        compiler_params=_cparams(("arbitrary",)),
        name="experts",
    )(tile_expert, n_used, xs, wg, wu, wd)


def _combine_kernel(pos0_ref, pos_ref, g4_ref, h_ref, swg_ref, swu_ref, swd_ref, z1_ref, mod_ref,
                    l2g_ref, l2b_ref, ys_ref, o_ref, buf0_ref, buf1_ref, sems, *, alpha, tm):
    j = pl.program_id(0)
    bufs = (buf0_ref, buf1_ref)
    n = TOP_K * tm

    def gather(idx_ref, off, slot):
        for k in range(TOP_K):
            for t in range(tm):
                pltpu.make_async_copy(ys_ref.at[pl.ds(idx_ref[0, 0, off + k * tm + t], 1), :],
                                      bufs[slot].at[k, pl.ds(t, 1), :], sems.at[slot]).start()

    def wait_tile(slot):
        for k in range(TOP_K):
            pltpu.make_async_copy(ys_ref.at[pl.ds(0, tm), :], bufs[slot].at[k], sems.at[slot]).wait()

    @pl.when(j == 0)
    def _():
        gather(pos0_ref, 0, 0)

    gate = mod_ref[0, 5:6, :]
    for slot in range(2):
        rows = slice(slot * tm, (slot + 1) * tm)
        wait_tile(slot)
        gather(pos_ref, slot * n, 1 - slot)
        a = _swiglu_hidden(h_ref[rows, :], swg_ref[...], swu_ref[...])
        acc = _dot(a.astype(BF16), swd_ref[...])
        for k in range(TOP_K):
            acc = acc + g4_ref[rows, k:k + 1] * _unpack_rows(bufs[slot][k])
        o_ref[rows, :] = _ln(alpha * z1_ref[rows, :] + gate * acc) * l2g_ref[...] + l2b_ref[...]

    @pl.when(j == pl.num_programs(0) - 1)
    def _():
        wait_tile(0)


def _combine(pos_tiles, g4, h, ys, w, z1, mod, tm, tiles_per_batch, alpha):
    t, d = h.shape
    de = w["sh_wg"].shape[1]
    half = ys.shape[1]
    nt = t // tm
    assert nt % 2 == 0 and tiles_per_batch % 2 == 0
    n = TOP_K * tm
    nxt = jnp.minimum(jnp.arange(nt) + 1, nt - 1)
    pos_next = pos_tiles[nxt].reshape(nt // 2, 1, 2 * n)
    full = lambda shape: pl.BlockSpec(shape, lambda i: (0,) * len(shape))
    two = lambda width: pl.BlockSpec((2 * tm, width), lambda i: (i, 0))
    body = functools.partial(_combine_kernel, alpha=alpha, tm=tm)
    buf = pltpu.VMEM((TOP_K, tm, half), jnp.uint32)
    return pl.pallas_call(
        body,
        grid=(nt // 2,),
        in_specs=[
            pl.BlockSpec((1, 1, n), lambda i: (0, 0, 0), memory_space=pltpu.SMEM),
            pl.BlockSpec((1, 1, 2 * n), lambda i: (i, 0, 0), memory_space=pltpu.SMEM),
            two(LANES), two(d),
            full((d, de)), full((d, de)), full((de, d)),
            two(d),
            pl.BlockSpec((1, 6, d), lambda i: (2 * i // tiles_per_batch, 0, 0)),
            full((1, d)), full((1, d)),
            pl.BlockSpec(memory_space=pl.ANY),
        ],
        out_specs=two(d),
        out_shape=jax.ShapeDtypeStruct((t, d), F32),
        scratch_shapes=[buf, buf, pltpu.SemaphoreType.DMA((2,))],
        compiler_params=_cparams(("arbitrary",)),
        name="combine",
    )(pos_tiles, pos_next, g4, h, w["sh_wg"], w["sh_wu"], w["sh_wd"], z1, mod, w["ln2_g"], w["ln2_b"], ys)


def _moe_sorted(h, hp, l, ew, w, z1, mod, tm, tiles_per_batch, alpha):
    t = h.shape[0]
    g4, eidx, rank, cnt = _route_sorted(h, w["router_w"], w["router_b"], tm)
    counts = cnt[:, 0].astype(jnp.int32)
    sizes = (counts + ROW_TILE - 1) // ROW_TILE * ROW_TILE
    ends = jnp.cumsum(sizes)
    starts = ends - sizes
    n_tiles = t * TOP_K // ROW_TILE + N_EXPERTS
    n_used = (ends[-1] // ROW_TILE).reshape(1)
    tile_expert = jnp.minimum(
        jnp.sum(jnp.arange(n_tiles)[:, None] >= (ends // ROW_TILE)[None, :], axis=1), N_EXPERTS - 1
    ).astype(jnp.int32)
    onehot = eidx[:, :, None] == jnp.arange(N_EXPERTS)[None, None, :]
    pos = rank + jnp.sum(jnp.where(onehot, starts[None, None, :], 0), axis=-1)
    pos_tiles = pos.reshape(TOP_K, t // tm, tm).transpose(1, 0, 2).reshape(t // tm, 1, TOP_K * tm)
    xs = _dispatch(ends.astype(jnp.int32), sizes.astype(jnp.int32), pos_tiles, hp, n_tiles * ROW_TILE, tm)
    ys = _experts(tile_expert, n_used, xs, l, ew)
    return _combine(pos_tiles, g4, h, ys, w, z1, mod, tm, tiles_per_batch, alpha)


def _rope_swap(r):
    r4 = r.reshape(r.shape[:-1] + (2, 2, ROPE_AXIS // 2))
    return jnp.stack([-r4[..., 1, :], r4[..., 0, :]], axis=-2).reshape(r.shape)


def _prep_layer(l, w_in, q_norm_g, w_uq, kv_norm_g, w_ukv, pool_w, pool_scale, conv_dw, conv_b,
                conv_ln_g, conv_ln_b, conv_pw, w_out, ln1_g, ln1_b, router_w, router_bias,
                exp_wg, exp_wu, exp_wd, sh_wg, sh_wu, sh_wd, ln2_g, ln2_b):
    d = w_in.shape[1]
    wi = w_in[l]
    w_in_p = jnp.concatenate([
        wi[:, 0:OFF_KR], wi[:, OFF_KR:OFF_POOL], jnp.zeros((d, PC_POOL - PC_KR - QK_ROPE), F32),
        wi[:, OFF_POOL:]], axis=1).astype(BF16)

    uq = w_uq[l].reshape(Q_LORA, MLA_HEADS, QK_NOPE + QK_ROPE)
    nope, rope = uq[..., :QK_NOPE], uq[..., QK_NOPE:]
    wqat = jnp.concatenate([nope, rope, rope], axis=-1).reshape(Q_LORA, QK_WIDTH).T.astype(BF16)
    wqbt = jnp.concatenate([jnp.zeros_like(nope), jnp.zeros_like(rope), _rope_swap(rope)],
                           axis=-1).reshape(Q_LORA, QK_WIDTH).T.astype(BF16)

    ukv = w_ukv[l].reshape(KV_LORA, MLA_HEADS, QK_NOPE + V_DIM)
    kn, vv = ukv[..., :QK_NOPE], ukv[..., QK_NOPE:]
    z32 = jnp.zeros((KV_LORA, MLA_HEADS, QK_ROPE), F32)
    top = jnp.concatenate([kn, z32, z32], axis=-1)
    eye = jnp.broadcast_to(jnp.eye(QK_ROPE, dtype=F32)[:, None, :], (QK_ROPE, MLA_HEADS, QK_ROPE))
    zr = jnp.zeros((QK_ROPE, MLA_HEADS, QK_ROPE), F32)
    zn = jnp.zeros((QK_ROPE, MLA_HEADS, QK_NOPE), F32)
    padrows = jnp.zeros((KV_LORA - QK_ROPE, MLA_HEADS, HEAD_PAD), F32)

    def kmat(kr_rows):
        return jnp.concatenate([top, kr_rows, padrows], axis=0).reshape(2 * KV_LORA, QK_WIDTH).astype(BF16)

    wka_lat = kmat(jnp.concatenate([zn, zr, eye], axis=-1))
    wkb_lat = kmat(jnp.concatenate([zn, zr, _rope_swap(eye)], axis=-1))
    wka_ctx = kmat(jnp.concatenate([zn, eye, zr], axis=-1))
    wkb_ctx = jnp.zeros((2 * KV_LORA, QK_WIDTH), BF16)

    wv = jnp.concatenate([vv, jnp.zeros((KV_LORA, MLA_HEADS, V_ROWS - V_DIM), F32)], axis=-1)
    wvt = wv.reshape(KV_LORA, VT_WIDTH).T.astype(BF16)
    vbt = (jnp.arange(VT_WIDTH) % V_ROWS == V_DIM).astype(F32)[:, None]

    pw = jnp.zeros((POOL_WIDTH, POOL_WIDTH), F32)
    for gi in range(len(POOL_WINDOWS)):
        pw = pw.at[gi * POOL_GROUP:(gi + 1) * POOL_GROUP, gi * POOL_GROUP:(gi + 1) * POOL_GROUP].set(pool_w[l, gi])

    common = dict(
        w_in=w_in_p, q_g=q_norm_g[l][None], kv_g=kv_norm_g[l][None], wqat=wqat, wqbt=wqbt, wvt=wvt, vbt=vbt,
        pool_w=pw.astype(BF16), pool_s=pool_scale[l][None],
        conv_dw=jnp.concatenate([conv_dw[l], jnp.zeros((1, CONV_WIDTH), F32)], axis=0),
        conv_b=conv_b[l][None], conv_ln_g=conv_ln_g[l][None], conv_ln_b=conv_ln_b[l][None],
        conv_pw=conv_pw[l].astype(BF16), w_out=w_out[l].astype(BF16),
        ln1_g=ln1_g[l][None], ln1_b=ln1_b[l][None],
        router_w=jnp.pad(router_w[l], ((0, 0), (0, LANES - N_EXPERTS))).astype(BF16),
        router_b=router_bias[l][:, None],
        sh_wg=sh_wg[l].astype(BF16), sh_wu=sh_wu[l].astype(BF16), sh_wd=sh_wd[l].astype(BF16),
        ln2_g=ln2_g[l][None], ln2_b=ln2_b[l][None],
    )
    return dict(common, wka=wka_lat, wkb=wkb_lat), dict(common, wka=wka_ctx, wkb=wkb_ctx)


def _rope_tables(n_lat, n_ctx):
    n_rows = n_lat // GRID_W
    inv = ROPE_BASE ** (-jnp.arange(0, ROPE_AXIS, 2, dtype=F32) / ROPE_AXIS)
    ang_r = jnp.arange(n_rows, dtype=F32)[:, None] * inv
    ang_c = jnp.arange(GRID_W, dtype=F32)[:, None] * inv
    by_row = lambda a: jnp.repeat(a, GRID_W, axis=0)
    by_col = lambda a: jnp.tile(a, (n_rows, 1))
    cos_r, sin_r, cos_c, sin_c = by_row(jnp.cos(ang_r)), by_row(jnp.sin(ang_r)), by_col(jnp.cos(ang_c)), by_col(jnp.sin(ang_c))
    cos = jnp.concatenate([cos_r, cos_r, cos_c, cos_c], axis=1)
    sin = jnp.concatenate([sin_r, sin_r, sin_c, sin_c], axis=1)
    ck = jnp.concatenate([jnp.ones((n_lat, HEAD_PAD - QK_ROPE), F32), cos], axis=1)
    sk = jnp.concatenate([jnp.zeros((n_lat, HEAD_PAD - QK_ROPE), F32), sin], axis=1)
    qs = ATTN_SCALE * LOG2E
    lat = ((ck * qs).T, (sk * qs).T, ck, sk)
    one = jnp.ones((n_ctx, HEAD_PAD), F32)
    zero = jnp.zeros((n_ctx, HEAD_PAD), F32)
    ctx = ((one * qs).T, zero.T, one, zero)
    return lat, ctx


def _sublayers(z, mod, l, ew, w, att, up, uc, tm, tm_moe, alpha):
    b, s, d = z.shape
    z1, h, hp = _mix(att, up, uc, z, mod, w, tm, alpha)
    out = _moe_sorted(h.reshape(b * s, d), hp.reshape(b * s, d // 2), l, ew, w, z1.reshape(b * s, d),
                      mod, tm_moe, s // tm_moe, alpha)
    return out.reshape(b, s, d)


def kernel(x, c, ctx, c_ctx, ada_w, ada_b, w_in, q_norm_g, w_uq, kv_norm_g, w_ukv, pool_w, pool_scale, conv_dw, conv_b, conv_ln_g, conv_ln_b, conv_pw, w_out, ln1_g, ln1_b, router_w, router_bias, exp_wg, exp_wu, exp_wd, sh_wg, sh_wu, sh_wd, ln2_g, ln2_b):
    b, s, d = x.shape
    n_ctx = ctx.shape[1]
    depth = ada_w.shape[0]
    alpha = (2 * depth) ** 0.25
    assert b + 1 <= 8 and s % GRID_W == 0

    tm_l = min(512, s)
    tm_c = min(256, n_ctx)
    tm_moe_l = min(512, s)
    tm_moe_c = min(128, n_ctx)
    bq = min(1024, s)

    cond = jnp.concatenate([c, c_ctx[None], jnp.zeros((8 - b - 1, d), F32)], axis=0)
    mods = _ada(cond, ada_w, ada_b)
    tab_l, tab_c = _rope_tables(s, n_ctx)
    ew = (exp_wg, exp_wu, exp_wd)

    zl, zc = x, ctx
    for l in range(depth):
        last = l == depth - 1
        w_l, w_c = _prep_layer(l, w_in, q_norm_g, w_uq, kv_norm_g, w_ukv, pool_w, pool_scale,
                               conv_dw, conv_b, conv_ln_g, conv_ln_b, conv_pw, w_out, ln1_g, ln1_b,
                               router_w, router_bias, exp_wg, exp_wu, exp_wd, sh_wg, sh_wu, sh_wd,
                               ln2_g, ln2_b)
        mod_l = mods[l, :b].reshape(b, 6, d)
        mod_c = jnp.broadcast_to(mods[l, b].reshape(1, 6, d), (b, 6, d))

        q_l, k_l, v_l, up_l, uc_l = _proj(zl, mod_l, w_l, tab_l, tm_l)
        q_c, k_c, v_c, up_c, uc_c = _proj(zc, mod_c, w_c, tab_c, tm_c)
        att_l = _attn(q_l, k_c, v_c, k_l, v_l, bq)
        zl = _sublayers(zl, mod_l, l, ew, w_l, att_l, up_l, uc_l, tm_l, tm_moe_l, alpha)
        if not last:
            att_c = _attn(q_c, k_c, v_c, None, None, tm_c)
            zc = _sublayers(zc, mod_c, l, ew, w_c, att_c, up_c, uc_c, tm_c, tm_moe_c, alpha)
    return zl
```

```python
import functools
import math

import jax
import jax.numpy as jnp
from jax import lax
from jax.experimental import pallas as pl
from jax.experimental.pallas import tpu as pltpu

GRID_W = 64
MLA_HEADS = 8
QK_NOPE = 64
QK_ROPE = 32
V_DIM = 64
Q_LORA = 256
KV_LORA = 128
ROPE_AXIS = QK_ROPE // 2
ROPE_BASE = 10000.0
ATTN_SCALE = (QK_NOPE + QK_ROPE) ** -0.5
POOL_WINDOWS = (2, 4, 8, 16)
POOL_GROUP = 64
POOL_WIDTH = POOL_GROUP * len(POOL_WINDOWS)
CONV_WIDTH = 256
CONV_K = 31
MLA_WIDTH = MLA_HEADS * V_DIM
OFF_KV = Q_LORA
OFF_KR = OFF_KV + KV_LORA
OFF_POOL = OFF_KR + QK_ROPE
OFF_CONV = OFF_POOL + POOL_WIDTH
N_EXPERTS = 32
TOP_K = 4
N_GROUPS = 4
TOPK_GROUPS = 2
GROUP_SIZE = N_EXPERTS // N_GROUPS
ROUTED_SCALE = 2.5
EPS = 1e-6

LANES = 128
HEAD_PAD = LANES
QK_WIDTH = MLA_HEADS * HEAD_PAD
V_ROWS = 80
VT_WIDTH = MLA_HEADS * V_ROWS
HALO = 16
P_COLS = 1280
PC_Q, PC_KV, PC_KR, PC_POOL, PC_CA, PC_CG = 0, 256, 384, 512, 768, 1024
VMEM_LIMIT = 48 * 1024 * 1024
LOG2E = math.log2(math.e)

F32 = jnp.float32
BF16 = jnp.bfloat16


def _cparams(sem, flags=None):
    return pltpu.CompilerParams(dimension_semantics=sem, vmem_limit_bytes=VMEM_LIMIT, flags=flags)


def _ln(x):
    mu = jnp.mean(x, axis=-1, keepdims=True)
    xc = x - mu
    var = jnp.mean(xc * xc, axis=-1, keepdims=True)
    return xc * lax.rsqrt(var + EPS)


def _rms(x):
    return x * lax.rsqrt(jnp.mean(x * x, axis=-1, keepdims=True) + EPS)


def _sigmoid(x):
    return 1.0 / (1.0 + jnp.exp(-x))


def _dot(a, b):
    return jnp.dot(a, b, preferred_element_type=F32)


def _dot_nt(a, b):
    return lax.dot_general(a, b, (((1,), (1,)), ((), ())), preferred_element_type=F32)


def _ada_kernel(c_ref, w_ref, b_ref, o_ref):
    x = c_ref[...]
    x = x * _sigmoid(x)
    o_ref[0] = _dot(x.astype(BF16), w_ref[0].astype(BF16)) + b_ref[0]


def _ada(cond, ada_w, ada_b):
    depth, d, n = ada_w.shape
    tn = 1536
    return pl.pallas_call(
        _ada_kernel,
        grid=(depth, n // tn),
        in_specs=[
            pl.BlockSpec((8, d), lambda l, j: (0, 0)),
            pl.BlockSpec((1, d, tn), lambda l, j: (l, 0, j)),
            pl.BlockSpec((1, 1, tn), lambda l, j: (l, 0, j)),
        ],
        out_specs=pl.BlockSpec((1, 8, tn), lambda l, j: (l, 0, j)),
        out_shape=jax.ShapeDtypeStruct((depth, 8, n), F32),
        compiler_params=_cparams(("parallel", "parallel")),
        name="ada",
    )(cond, ada_w, ada_b.reshape(depth, 1, n))


def _proj_kernel(z_ref, mod_ref, win_ref, qg_ref, kvg_ref, wqat_ref, wqbt_ref, wka_ref, wkb_ref,
                 wvt_ref, vbt_ref, cqt_ref, sqt_ref, ck_ref, sk_ref,
                 qt_ref, k_ref, vt_ref, up_ref, uc_ref):
    z = z_ref[0]
    shift = mod_ref[0, 0:1, :]
    scale = mod_ref[0, 1:2, :]
    h = _ln(z) * (1.0 + scale) + shift
    p = _dot(h.astype(BF16), win_ref[...])
    qn = (_rms(p[:, PC_Q:PC_KV]) * qg_ref[...]).astype(BF16)
    kvn = (_rms(p[:, PC_KV:PC_KR]) * kvg_ref[...]).astype(BF16)
    xk = jnp.concatenate([kvn, p[:, PC_KR:PC_POOL].astype(BF16)], axis=1)
    qat = _dot_nt(wqat_ref[...], qn)
    qbt = _dot_nt(wqbt_ref[...], qn)
    ka = _dot(xk, wka_ref[...])
    kb = _dot(xk, wkb_ref[...])
    cqt, sqt, ck, sk = cqt_ref[...], sqt_ref[...], ck_ref[...], sk_ref[...]
    for hd in range(MLA_HEADS):
        sl = slice(hd * HEAD_PAD, (hd + 1) * HEAD_PAD)
        qt_ref[0, sl, :] = (qat[sl, :] * cqt + qbt[sl, :] * sqt).astype(BF16)
        k_ref[0, :, sl] = (ka[:, sl] * ck + kb[:, sl] * sk).astype(BF16)
    vt_ref[0, 0] = (_dot_nt(wvt_ref[...], kvn) + vbt_ref[...]).astype(BF16)
    up_ref[0] = p[:, PC_POOL:PC_CA]
    uc_ref[0] = p[:, PC_CA:PC_CG] * _sigmoid(p[:, PC_CG:P_COLS])


def _proj(z, mod, w, tabs, tm):
    b, s, d = z.shape
    full = lambda shape: pl.BlockSpec(shape, lambda i, j: (0,) * len(shape))
    tok = lambda width: pl.BlockSpec((1, tm, width), lambda i, j: (i, j, 0))
    tab = pl.BlockSpec((tm, HEAD_PAD), lambda i, j: (j, 0))
    tab_t = pl.BlockSpec((HEAD_PAD, tm), lambda i, j: (0, j))
    return pl.pallas_call(
        _proj_kernel,
        grid=(b, s // tm),
        in_specs=[
            tok(d),
            pl.BlockSpec((1, 6, d), lambda i, j: (i, 0, 0)),
            full((d, P_COLS)), full((1, Q_LORA)), full((1, KV_LORA)),
            full((QK_WIDTH, Q_LORA)), full((QK_WIDTH, Q_LORA)),
            full((2 * KV_LORA, QK_WIDTH)), full((2 * KV_LORA, QK_WIDTH)),
            full((VT_WIDTH, KV_LORA)), full((VT_WIDTH, 1)),
            tab_t, tab_t, tab, tab,
        ],
        out_specs=[
            pl.BlockSpec((1, QK_WIDTH, tm), lambda i, j: (i, 0, j)),
            tok(QK_WIDTH),
            pl.BlockSpec((1, 1, VT_WIDTH, tm), lambda i, j: (i, j, 0, 0)),
            tok(POOL_WIDTH), tok(CONV_WIDTH)],
        out_shape=[
            jax.ShapeDtypeStruct((b, QK_WIDTH, s), BF16),
            jax.ShapeDtypeStruct((b, s, QK_WIDTH), BF16),
            jax.ShapeDtypeStruct((b, s // tm, VT_WIDTH, tm), BF16),
            jax.ShapeDtypeStruct((b, s, POOL_WIDTH), F32),
            jax.ShapeDtypeStruct((b, s, CONV_WIDTH), F32),
        ],
        compiler_params=_cparams(("parallel", "parallel")),
        name="proj",
    )(z, mod, w["w_in"], w["q_g"], w["kv_g"], w["wqat"], w["wqbt"], w["wka"], w["wkb"],
      w["wvt"], w["vbt"], *tabs)


HEADS_PER_STEP = 2
QCOLS = 256
ATTN_UNROLL = 4


def _attn_body(qt_ref, kc_ref, vct_ref, kl_ref, vlt_ref, o_ref, s_refs):
    heads = range(HEADS_PER_STEP)
    sls = [slice(hh * HEAD_PAD, (hh + 1) * HEAD_PAD) for hh in heads]
    vsl = [slice(hh * V_ROWS, (hh + 1) * V_ROWS) for hh in heads]
    qts = [qt_ref[0, sl, :] for sl in sls]
    ms, accs = [], []
    for hh in heads:
        st = _dot(kc_ref[0, :, sls[hh]], qts[hh])
        m = jnp.max(st, axis=0, keepdims=True)
        pt = jnp.exp2(st - m)
        ms.append(m)
        accs.append(_dot(vct_ref[0, 0, vsl[hh], :], pt.astype(BF16)))

    if kl_ref is not None:
        n, bk = vlt_ref.shape[1], vlt_ref.shape[3]
        assert n >= 2 and n % 2 == 0

        def score(c, slot):
            r0 = c * bk if isinstance(c, int) else pl.multiple_of(c * bk, bk)
            cms = []
            for hh in heads:
                st = _dot(kl_ref[0, pl.ds(r0, bk), sls[hh]], qts[hh])
                s_refs[slot][hh] = st
                cms.append(jnp.max(st, axis=0, keepdims=True))
            return cms

        def half(c, cur, nxt, ms, cms, accs, with_score=True):
            m_new = [jnp.maximum(ms[hh], cms[hh]) for hh in heads]
            alphas = [jnp.exp2(ms[hh] - m_new[hh]) for hh in heads]
            if with_score:
                cms = score(c + 1, nxt)
            out = []
            for hh in heads:
                cols = []
                for g in range(0, qts[hh].shape[1], QCOLS):
                    gs = slice(g, g + QCOLS)
                    pt = jnp.exp2(s_refs[cur][hh, :, gs] - m_new[hh][:, gs]).astype(BF16)
                    cols.append(accs[hh][:, gs] * alphas[hh][:, gs]
                                + _dot(vlt_ref[0, c, vsl[hh], :], pt))
                out.append(jnp.concatenate(cols, axis=1))
            return m_new, cms, out

        cms = score(0, 0)
        unroll = ATTN_UNROLL if n % ATTN_UNROLL == 0 else 2
        trips = (n - 2) // unroll

        def steps(t, carry):
            ms, cms, accs = [list(x) for x in carry]
            for u in range(unroll):
                ms, cms, accs = half(unroll * t + u, u % 2, (u + 1) % 2, ms, cms, accs)
            return tuple(tuple(x) for x in (ms, cms, accs))

        carry = tuple(tuple(x) for x in (ms, cms, accs))
        carry = lax.fori_loop(0, trips, steps, carry)
        ms, cms, accs = [list(x) for x in carry]
        for c in range(trips * unroll, n):
            ms, cms, accs = half(c, c % 2, (c + 1) % 2, ms, cms, accs, with_score=c + 1 < n)

    ot = jnp.concatenate([acc[0:V_DIM, :] / acc[V_DIM:V_DIM + 1, :] for acc in accs], axis=0)
    o_ref[0] = ot.T.astype(o_ref.dtype)


def _attn_kernel_full(qt_ref, kc_ref, vct_ref, kl_ref, vlt_ref, o_ref, s0_ref, s1_ref):
    _attn_body(qt_ref, kc_ref, vct_ref, kl_ref, vlt_ref, o_ref, (s0_ref, s1_ref))


def _attn_kernel_ctx(qt_ref, kc_ref, vct_ref, o_ref):
    _attn_body(qt_ref, kc_ref, vct_ref, None, None, o_ref, None)


def _attn(qt, kc, vct, kl, vlt, bq):
    b, _, s = qt.shape
    nc = kc.shape[1]
    w2 = HEADS_PER_STEP * HEAD_PAD
    v2 = HEADS_PER_STEP * V_ROWS
    in_specs = [
        pl.BlockSpec((1, w2, bq), lambda i, h, j: (i, h, j)),
        pl.BlockSpec((1, nc, w2), lambda i, h, j: (i, 0, h)),
        pl.BlockSpec((1, 1, v2, nc), lambda i, h, j: (i, 0, h, 0)),
    ]
    args = [qt, kc, vct]
    scratch = []
    if kl is None:
        body = _attn_kernel_ctx
    else:
        bk = vlt.shape[3]
        scratch = [pltpu.VMEM((HEADS_PER_STEP, bk, bq), F32)] * 2
        in_specs += [
            pl.BlockSpec((1, kl.shape[1], w2), lambda i, h, j: (i, 0, h)),
            pl.BlockSpec((1, vlt.shape[1], v2, vlt.shape[3]), lambda i, h, j: (i, 0, h, 0)),
        ]
        args += [kl, vlt]
        body = _attn_kernel_full
    return pl.pallas_call(
        body,
        grid=(b, MLA_HEADS // HEADS_PER_STEP, s // bq),
        in_specs=in_specs,
        out_specs=pl.BlockSpec((1, bq, HEADS_PER_STEP * V_DIM), lambda i, h, j: (i, j, h)),
        out_shape=jax.ShapeDtypeStruct((b, s, MLA_WIDTH), BF16),
        scratch_shapes=scratch,
        compiler_params=_cparams(("parallel", "parallel", "arbitrary")),
        name="attn",
    )(*args)


SUBLANES = 8


def _fill_ext(ext_ref, sh_ref, main_ref, left_ref, right_ref, tm):
    j = pl.program_id(1)
    last = pl.num_programs(1) - 1
    ext_ref[0:HALO, :] = jnp.where(j > 0, left_ref[0], 0.0)
    ext_ref[HALO:HALO + tm, :] = main_ref[0]
    ext_ref[HALO + tm:HALO + tm + HALO, :] = jnp.where(j < last, right_ref[0], 0.0)
    n = sh_ref.shape[1]
    for b in range(1, SUBLANES):
        sh_ref[b - 1, :, :] = ext_ref[b:b + n, :]


def _mix_kernel(att_ref, up_ref, upl_ref, upr_ref, uc_ref, ucl_ref, ucr_ref, z_ref, mod_ref,
                poolw_ref, pools_ref, dw_ref, cb_ref, clg_ref, clb_ref, cpw_ref, wout_ref,
                l1g_ref, l1b_ref, z1_ref, h_ref, hp_ref, ext_ref, sh_ref, *, tm, seq, alpha):
    j = pl.program_id(1)

    def shifted(d):
        a, b = divmod(HALO + d, SUBLANES)
        if b == 0:
            return ext_ref[a * SUBLANES:a * SUBLANES + tm, :]
        return sh_ref[b - 1, a * SUBLANES:a * SUBLANES + tm, :]

    _fill_ext(ext_ref, sh_ref, up_ref, upl_ref, upr_ref, tm)
    x = up_ref[0]
    lane = lax.broadcasted_iota(jnp.int32, (tm, POOL_WIDTH), 1)
    t = lax.broadcasted_iota(jnp.int32, (tm, POOL_WIDTH), 0) + j * tm

    run = x + shifted(-1)
    sums = [run]
    for lo in (2, 4, 8):
        for d in list(range(-lo, -lo // 2)) + list(range(lo // 2, lo)):
            run = run + shifted(d)
        sums.append(run)
    wsum = jnp.where(lane < POOL_GROUP, sums[0],
                     jnp.where(lane < 2 * POOL_GROUP, sums[1],
                               jnp.where(lane < 3 * POOL_GROUP, sums[2], sums[3])))
    lo = jnp.where(lane < POOL_GROUP, 1,
                   jnp.where(lane < 2 * POOL_GROUP, 2, jnp.where(lane < 3 * POOL_GROUP, 4, 8)))
    cnt = jnp.minimum(t + lo, seq) - jnp.maximum(t - lo, 0)
    diff = wsum / cnt.astype(F32) - x
    pooled = _dot(diff.astype(BF16), poolw_ref[...]) * pools_ref[...]

    _fill_ext(ext_ref, sh_ref, uc_ref, ucl_ref, ucr_ref, tm)
    acc = jnp.zeros((tm, CONV_WIDTH), F32) + cb_ref[...]
    for k in range(CONV_K):
        acc = acc + dw_ref[k:k + 1, :] * shifted(k - CONV_K // 2)
    y = _ln(acc) * clg_ref[...] + clb_ref[...]
    y = y * _sigmoid(y)
    conv = _dot(y.astype(BF16), cpw_ref[...])

    cat = jnp.concatenate([att_ref[0], pooled.astype(BF16), conv.astype(BF16)], axis=1)
    out = _dot(cat, wout_ref[...])
    gate = mod_ref[0, 2:3, :]
    z1 = _ln(alpha * z_ref[0] + gate * out) * l1g_ref[...] + l1b_ref[...]
    z1_ref[0] = z1
    h = _ln(z1) * (1.0 + mod_ref[0, 4:5, :]) + mod_ref[0, 3:4, :]
    h_ref[0] = h.astype(BF16)
    hp_ref[0] = _pack_rows(h)


def _mix(att, up, uc, z, mod, w, tm, alpha):
    b, s, d = z.shape
    hb = tm // HALO
    nhb = s // HALO
    full = lambda shape: pl.BlockSpec(shape, lambda i, j: (0,) * len(shape))
    tok = lambda width: pl.BlockSpec((1, tm, width), lambda i, j: (i, j, 0))
    left = lambda width: pl.BlockSpec((1, HALO, width),
                                      lambda i, j: (i, jnp.maximum(j * hb - 1, 0), 0))
    right = lambda width: pl.BlockSpec((1, HALO, width),
                                       lambda i, j: (i, jnp.minimum((j + 1) * hb, nhb - 1), 0))
    body = functools.partial(_mix_kernel, tm=tm, seq=s, alpha=alpha)
    return pl.pallas_call(
        body,
        grid=(b, s // tm),
        in_specs=[
            tok(MLA_WIDTH),
            tok(POOL_WIDTH), left(POOL_WIDTH), right(POOL_WIDTH),
            tok(CONV_WIDTH), left(CONV_WIDTH), right(CONV_WIDTH),
            tok(d),
            pl.BlockSpec((1, 6, d), lambda i, j: (i, 0, 0)),
            full((POOL_WIDTH, POOL_WIDTH)), full((1, POOL_WIDTH)),
            full((CONV_K + 1, CONV_WIDTH)), full((1, CONV_WIDTH)), full((1, CONV_WIDTH)),
            full((1, CONV_WIDTH)), full((CONV_WIDTH, CONV_WIDTH)),
            full((d, d)), full((1, d)), full((1, d)),
        ],
        out_specs=[tok(d), tok(d), tok(d // 2)],
        out_shape=[jax.ShapeDtypeStruct((b, s, d), F32), jax.ShapeDtypeStruct((b, s, d), BF16),
                   jax.ShapeDtypeStruct((b, s, d // 2), jnp.uint32)],
        scratch_shapes=[pltpu.VMEM((tm + 2 * HALO, POOL_WIDTH), F32),
                        pltpu.VMEM((SUBLANES - 1, tm + 2 * HALO - SUBLANES, POOL_WIDTH), F32)],
        compiler_params=_cparams(("parallel", "parallel")),
        name="mix",
    )(att, up, up, up, uc, uc, uc, z, mod, w["pool_w"], w["pool_s"], w["conv_dw"], w["conv_b"],
      w["conv_ln_g"], w["conv_ln_b"], w["conv_pw"], w["w_out"], w["ln1_g"], w["ln1_b"])


def _select_experts(h, rw, rb):
    tm = h.shape[0]
    logits = _dot(h, rw)
    scores = _sigmoid(logits.T[0:N_EXPERTS, :])
    sel = scores + rb
    neg = -jnp.inf

    row8 = lax.broadcasted_iota(jnp.int32, (GROUP_SIZE, tm), 0)
    gscore = []
    for g in range(N_GROUPS):
        xg = sel[g * GROUP_SIZE:(g + 1) * GROUP_SIZE, :]
        m1 = jnp.max(xg, axis=0, keepdims=True)
        i1 = jnp.min(jnp.where(xg == m1, row8, GROUP_SIZE), axis=0, keepdims=True)
        m2 = jnp.max(jnp.where(row8 == i1, neg, xg), axis=0, keepdims=True)
        gscore.append(m1 + m2)
    masked = []
    for g in range(N_GROUPS):
        ahead = jnp.zeros((1, tm), F32)
        for g2 in range(N_GROUPS):
            if g2 < g:
                ahead = ahead + jnp.where(gscore[g2] >= gscore[g], 1.0, 0.0)
            elif g2 > g:
                ahead = ahead + jnp.where(gscore[g2] > gscore[g], 1.0, 0.0)
        keep = ahead < TOPK_GROUPS
        masked.append(jnp.where(keep, sel[g * GROUP_SIZE:(g + 1) * GROUP_SIZE, :], neg))
    masked = jnp.concatenate(masked, axis=0)

    row = lax.broadcasted_iota(jnp.int32, (N_EXPERTS, tm), 0)
    chosen = jnp.zeros((N_EXPERTS, tm), F32)
    hits = []
    for _ in range(TOP_K):
        m = jnp.max(masked, axis=0, keepdims=True)
        idx = jnp.min(jnp.where(masked == m, row, N_EXPERTS), axis=0, keepdims=True)
        hit = row == idx
        hits.append(hit)
        chosen = jnp.where(hit, 1.0, chosen)
        masked = jnp.where(hit, neg, masked)
    w = chosen * scores
    gates = w / jnp.sum(w, axis=0, keepdims=True) * ROUTED_SCALE
    return gates, chosen, hits


def _lanes_to_rows(rows):
    tm = rows[0].shape[1]
    pad = jnp.zeros((LANES - len(rows), tm), F32)
    return jnp.concatenate(rows + [pad], axis=0).T


def _route_sorted_kernel(h_ref, rw_ref, rb_ref, g4_ref, eidx_ref, rank_ref, cnt_ref, base_ref):
    @pl.when(pl.program_id(0) == 0)
    def _():
        base_ref[...] = jnp.zeros_like(base_ref)

    gates, chosen, hits = _select_experts(h_ref[...], rw_ref[...], rb_ref[...])
    tm = gates.shape[1]
    ii = lax.broadcasted_iota(jnp.int32, (tm, tm), 0)
    jj = lax.broadcasted_iota(jnp.int32, (tm, tm), 1)
    upper = jnp.where(ii <= jj, 1.0, 0.0).astype(BF16)
    prefix = _dot(chosen.astype(BF16), upper)
    rank_full = base_ref[...] + prefix - 1.0
    row = lax.broadcasted_iota(jnp.int32, (N_EXPERTS, tm), 0).astype(F32)
    pick = lambda hit, val: jnp.sum(jnp.where(hit, val, 0.0), axis=0, keepdims=True)
    g4_ref[...] = _lanes_to_rows([pick(hit, gates) for hit in hits])
    eidx_ref[...] = jnp.concatenate([pick(hit, row) for hit in hits], axis=0).astype(jnp.int32)
    rank_ref[...] = jnp.concatenate([pick(hit, rank_full) for hit in hits], axis=0).astype(jnp.int32)
    base_ref[...] += jnp.sum(chosen, axis=1, keepdims=True)
    cnt_ref[...] = jnp.broadcast_to(base_ref[...], cnt_ref.shape)


def _route_sorted(h, rw, rb, tm):
    t, d = h.shape
    pick = pl.BlockSpec((TOP_K, tm), lambda i: (0, i))
    return pl.pallas_call(
        _route_sorted_kernel,
        grid=(t // tm,),
        in_specs=[
            pl.BlockSpec((tm, d), lambda i: (i, 0)),
            pl.BlockSpec((d, LANES), lambda i: (0, 0)),
            pl.BlockSpec((N_EXPERTS, 1), lambda i: (0, 0)),
        ],
        out_specs=[pl.BlockSpec((tm, LANES), lambda i: (i, 0)), pick, pick,
                   pl.BlockSpec((N_EXPERTS, LANES), lambda i: (0, 0))],
        out_shape=[jax.ShapeDtypeStruct((t, LANES), F32),
                   jax.ShapeDtypeStruct((TOP_K, t), jnp.int32),
                   jax.ShapeDtypeStruct((TOP_K, t), jnp.int32),
                   jax.ShapeDtypeStruct((N_EXPERTS, LANES), F32)],
        scratch_shapes=[pltpu.VMEM((N_EXPERTS, 1), F32)],
        compiler_params=_cparams(("arbitrary",)),
        name="route_sorted",
    )(h, rw, rb)


def _swiglu_hidden(x, wg, wu):
    hg = _dot(x, wg)
    return hg * _sigmoid(hg) * _dot(x, wu)


ROW_TILE = 1024
MIN_ROW_TILE = 256


def _pack_rows(x):
    n = x.shape[1] // 2
    bits = lax.bitcast_convert_type(x.astype(BF16).astype(F32), jnp.uint32)
    return bits[:, :n] | (bits[:, n:] >> 16)


def _unpack_rows(w):
    hi = lax.bitcast_convert_type(w & jnp.uint32(0xFFFF0000), F32)
    lo = lax.bitcast_convert_type(w << 16, F32)
    return jnp.concatenate([hi, lo], axis=1)


def _row_copies(pos_ref, tm, make_copy):
    def body(t, carry):
        for k in range(TOP_K):
            make_copy(k, t, pos_ref[0, 0, k * tm + t]).start()
        return carry
    lax.fori_loop(0, tm, body, 0, unroll=8)


def _dispatch_kernel(ends_ref, sizes_ref, pos_ref, hp_ref, xs_ref, zero_ref, sem, zsem):
    tm = hp_ref.shape[0]

    @pl.when(pl.program_id(0) == 0)
    def _():
        zero_ref[...] = jnp.zeros_like(zero_ref)
        n_rows = xs_ref.shape[0]
        row_tile = zero_ref.shape[0]

        def zero_tile(r0):
            return pltpu.make_async_copy(
                zero_ref, xs_ref.at[pl.ds(pl.multiple_of(r0, row_tile), row_tile), :], zsem)

        for start_or_wait in ("start", "wait"):
            for e in range(N_EXPERTS):
                @pl.when(sizes_ref[e] > 0)
                def _():
                    getattr(zero_tile(ends_ref[e] - row_tile), start_or_wait)()

                tail = ends_ref[N_EXPERTS - 1] + e * row_tile

                @pl.when(tail < n_rows)
                def _():
                    getattr(zero_tile(tail), start_or_wait)()

    _row_copies(pos_ref, tm, lambda k, t, p: pltpu.make_async_copy(
        hp_ref.at[pl.ds(t, 1), :], xs_ref.at[pl.ds(p, 1), :], sem))
    n = TOP_K * tm
    pltpu.make_async_copy(xs_ref.at[pl.ds(0, n), :], xs_ref.at[pl.ds(0, n), :], sem).wait()


def _dispatch(ends, sizes, pos_tiles, hp, n_rows, tm, row_tile):
    t, half = hp.shape
    return pl.pallas_call(
        _dispatch_kernel,
        grid_spec=pltpu.PrefetchScalarGridSpec(
            num_scalar_prefetch=2,
            grid=(t // tm,),
            in_specs=[
                pl.BlockSpec((1, 1, TOP_K * tm), lambda i, en, sz: (i, 0, 0), memory_space=pltpu.SMEM),
                pl.BlockSpec((tm, half), lambda i, en, sz: (i, 0)),
            ],
            out_specs=pl.BlockSpec(memory_space=pl.ANY),
            scratch_shapes=[pltpu.VMEM((row_tile, half), jnp.uint32),
                            pltpu.SemaphoreType.DMA(()), pltpu.SemaphoreType.DMA(())],
        ),
        out_shape=jax.ShapeDtypeStruct((n_rows, half), jnp.uint32),
        compiler_params=_cparams(("arbitrary",)),
        name="dispatch",
    )(ends, sizes, pos_tiles, hp)


def _experts_kernel(te_ref, nu_ref, xs_ref, wg_ref, wu_ref, wd_ref, ys_ref, wgb_ref, wub_ref, wdb_ref):
    i = pl.program_id(0)

    @pl.when((i == 0) | (te_ref[i] != te_ref[jnp.maximum(i - 1, 0)]))
    def _():
        wgb_ref[...] = wg_ref[0, 0].astype(BF16)
        wub_ref[...] = wu_ref[0, 0].astype(BF16)
        wdb_ref[...] = wd_ref[0, 0].astype(BF16)

    @pl.when(i < nu_ref[0])
    def _():
        x = _unpack_rows(xs_ref[...]).astype(BF16)
        a = _swiglu_hidden(x, wgb_ref[...], wub_ref[...])
        ys_ref[...] = _pack_rows(_dot(a.astype(BF16), wdb_ref[...]))

    @pl.when(i >= nu_ref[0])
    def _():
        ys_ref[...] = jnp.zeros_like(ys_ref)


def _experts(tile_expert, n_used, xs, l, ew, row_tile):
    wg, wu, wd = ew
    d, de = wg.shape[2], wg.shape[3]
    n_rows, half = xs.shape
    return pl.pallas_call(
        _experts_kernel,
        grid_spec=pltpu.PrefetchScalarGridSpec(
            num_scalar_prefetch=2,
            grid=(n_rows // row_tile,),
            in_specs=[
                pl.BlockSpec((row_tile, half), lambda i, te, nu: (i, 0)),
                pl.BlockSpec((1, 1, d, de), lambda i, te, nu: (l, te[i], 0, 0)),
                pl.BlockSpec((1, 1, d, de), lambda i, te, nu: (l, te[i], 0, 0)),
                pl.BlockSpec((1, 1, de, d), lambda i, te, nu: (l, te[i], 0, 0)),
            ],
            out_specs=pl.BlockSpec((row_tile, half), lambda i, te, nu: (i, 0)),
            scratch_shapes=[pltpu.VMEM((d, de), BF16), pltpu.VMEM((d, de), BF16), pltpu.VMEM((de, d), BF16)],
        ),
        out_shape=jax.ShapeDtypeStruct((n_rows, half), jnp.uint32),
        compiler_params=_cparams(("arbitrary",)),
        name="experts",
    )(tile_expert, n_used, xs, wg, wu, wd)


def _combine_kernel(pos0_ref, pos_ref, g4_ref, h_ref, swg_ref, swu_ref, swd_ref, z1_ref, mod_ref,
                    l2g_ref, l2b_ref, ys_ref, o_ref, buf0_ref, buf1_ref, sems, *, alpha, tm):
    j = pl.program_id(0)
    bufs = (buf0_ref, buf1_ref)
    n = TOP_K * tm

    def gather(idx_ref, off, slot):
        for k in range(TOP_K):
            for t in range(tm):
                pltpu.make_async_copy(ys_ref.at[pl.ds(idx_ref[0, 0, off + k * tm + t], 1), :],
                                      bufs[slot].at[k, pl.ds(t, 1), :], sems.at[slot]).start()

    def wait_tile(slot):
        for k in range(TOP_K):
            pltpu.make_async_copy(ys_ref.at[pl.ds(0, tm), :], bufs[slot].at[k], sems.at[slot]).wait()

    @pl.when(j == 0)
    def _():
        gather(pos0_ref, 0, 0)

    gate = mod_ref[0, 5:6, :]
    for slot in range(2):
        rows = slice(slot * tm, (slot + 1) * tm)
        wait_tile(slot)
        gather(pos_ref, slot * n, 1 - slot)
        a = _swiglu_hidden(h_ref[rows, :], swg_ref[...], swu_ref[...])
        acc = _dot(a.astype(BF16), swd_ref[...])
        for k in range(TOP_K):
            acc = acc + g4_ref[rows, k:k + 1] * _unpack_rows(bufs[slot][k])
        o_ref[rows, :] = _ln(alpha * z1_ref[rows, :] + gate * acc) * l2g_ref[...] + l2b_ref[...]

    @pl.when(j == pl.num_programs(0) - 1)
    def _():
        wait_tile(0)


def _combine(pos_tiles, g4, h, ys, w, z1, mod, tm, tiles_per_batch, alpha):
    t, d = h.shape
    de = w["sh_wg"].shape[1]
    half = ys.shape[1]
    nt = t // tm
    assert nt % 2 == 0 and tiles_per_batch % 2 == 0
    n = TOP_K * tm
    nxt = jnp.minimum(jnp.arange(nt) + 1, nt - 1)
    pos_next = pos_tiles[nxt].reshape(nt // 2, 1, 2 * n)
    full = lambda shape: pl.BlockSpec(shape, lambda i: (0,) * len(shape))
    two = lambda width: pl.BlockSpec((2 * tm, width), lambda i: (i, 0))
    body = functools.partial(_combine_kernel, alpha=alpha, tm=tm)
    buf = pltpu.VMEM((TOP_K, tm, half), jnp.uint32)
    return pl.pallas_call(
        body,
        grid=(nt // 2,),
        in_specs=[
            pl.BlockSpec((1, 1, n), lambda i: (0, 0, 0), memory_space=pltpu.SMEM),
            pl.BlockSpec((1, 1, 2 * n), lambda i: (i, 0, 0), memory_space=pltpu.SMEM),
            two(LANES), two(d),
            full((d, de)), full((d, de)), full((de, d)),
            two(d),
            pl.BlockSpec((1, 6, d), lambda i: (2 * i // tiles_per_batch, 0, 0)),
            full((1, d)), full((1, d)),
            pl.BlockSpec(memory_space=pl.ANY),
        ],
        out_specs=two(d),
        out_shape=jax.ShapeDtypeStruct((t, d), F32),
        scratch_shapes=[buf, buf, pltpu.SemaphoreType.DMA((2,))],
        compiler_params=_cparams(("arbitrary",)),
        name="combine",
    )(pos_tiles, pos_next, g4, h, w["sh_wg"], w["sh_wu"], w["sh_wd"], z1, mod, w["ln2_g"], w["ln2_b"], ys)


def _moe_sorted(h, hp, l, ew, w, z1, mod, tm, tiles_per_batch, alpha):
    t = h.shape[0]
    row_tile = ROW_TILE if t * TOP_K >= 2 * N_EXPERTS * ROW_TILE else MIN_ROW_TILE
    g4, eidx, rank, cnt = _route_sorted(h, w["router_w"], w["router_b"], tm)
    counts = cnt[:, 0].astype(jnp.int32)
    sizes = (counts + row_tile - 1) // row_tile * row_tile
    ends = jnp.cumsum(sizes)
    starts = ends - sizes
    n_tiles = t * TOP_K // row_tile + N_EXPERTS
    n_used = (ends[-1] // row_tile).reshape(1)
    tile_expert = jnp.minimum(
        jnp.sum(jnp.arange(n_tiles)[:, None] >= (ends // row_tile)[None, :], axis=1), N_EXPERTS - 1
    ).astype(jnp.int32)
    onehot = eidx[:, :, None] == jnp.arange(N_EXPERTS)[None, None, :]
    pos = rank + jnp.sum(jnp.where(onehot, starts[None, None, :], 0), axis=-1)
    pos_tiles = pos.reshape(TOP_K, t // tm, tm).transpose(1, 0, 2).reshape(t // tm, 1, TOP_K * tm)
    xs = _dispatch(ends.astype(jnp.int32), sizes.astype(jnp.int32), pos_tiles, hp, n_tiles * row_tile, tm,
                   row_tile)
    ys = _experts(tile_expert, n_used, xs, l, ew, row_tile)
    return _combine(pos_tiles, g4, h, ys, w, z1, mod, tm, tiles_per_batch, alpha)


def _rope_swap(r):
    r4 = r.reshape(r.shape[:-1] + (2, 2, ROPE_AXIS // 2))
    return jnp.stack([-r4[..., 1, :], r4[..., 0, :]], axis=-2).reshape(r.shape)


def _prep_layer(l, w_in, q_norm_g, w_uq, kv_norm_g, w_ukv, pool_w, pool_scale, conv_dw, conv_b,
                conv_ln_g, conv_ln_b, conv_pw, w_out, ln1_g, ln1_b, router_w, router_bias,
                exp_wg, exp_wu, exp_wd, sh_wg, sh_wu, sh_wd, ln2_g, ln2_b):
    d = w_in.shape[1]
    wi = w_in[l]
    w_in_p = jnp.concatenate([
        wi[:, 0:OFF_KR], wi[:, OFF_KR:OFF_POOL], jnp.zeros((d, PC_POOL - PC_KR - QK_ROPE), F32),
        wi[:, OFF_POOL:]], axis=1).astype(BF16)

    uq = w_uq[l].reshape(Q_LORA, MLA_HEADS, QK_NOPE + QK_ROPE)
    nope, rope = uq[..., :QK_NOPE], uq[..., QK_NOPE:]
    wqat = jnp.concatenate([nope, rope, rope], axis=-1).reshape(Q_LORA, QK_WIDTH).T.astype(BF16)
    wqbt = jnp.concatenate([jnp.zeros_like(nope), jnp.zeros_like(rope), _rope_swap(rope)],
                           axis=-1).reshape(Q_LORA, QK_WIDTH).T.astype(BF16)

    ukv = w_ukv[l].reshape(KV_LORA, MLA_HEADS, QK_NOPE + V_DIM)
    kn, vv = ukv[..., :QK_NOPE], ukv[..., QK_NOPE:]
    z32 = jnp.zeros((KV_LORA, MLA_HEADS, QK_ROPE), F32)
    top = jnp.concatenate([kn, z32, z32], axis=-1)
    eye = jnp.broadcast_to(jnp.eye(QK_ROPE, dtype=F32)[:, None, :], (QK_ROPE, MLA_HEADS, QK_ROPE))
    zr = jnp.zeros((QK_ROPE, MLA_HEADS, QK_ROPE), F32)
    zn = jnp.zeros((QK_ROPE, MLA_HEADS, QK_NOPE), F32)
    padrows = jnp.zeros((KV_LORA - QK_ROPE, MLA_HEADS, HEAD_PAD), F32)

    def kmat(kr_rows):
        return jnp.concatenate([top, kr_rows, padrows], axis=0).reshape(2 * KV_LORA, QK_WIDTH).astype(BF16)

    wka_lat = kmat(jnp.concatenate([zn, zr, eye], axis=-1))
    wkb_lat = kmat(jnp.concatenate([zn, zr, _rope_swap(eye)], axis=-1))
    wka_ctx = kmat(jnp.concatenate([zn, eye, zr], axis=-1))
    wkb_ctx = jnp.zeros((2 * KV_LORA, QK_WIDTH), BF16)

    wv = jnp.concatenate([vv, jnp.zeros((KV_LORA, MLA_HEADS, V_ROWS - V_DIM), F32)], axis=-1)
    wvt = wv.reshape(KV_LORA, VT_WIDTH).T.astype(BF16)
    vbt = (jnp.arange(VT_WIDTH) % V_ROWS == V_DIM).astype(F32)[:, None]

    pw = jnp.zeros((POOL_WIDTH, POOL_WIDTH), F32)
    for gi in range(len(POOL_WINDOWS)):
        pw = pw.at[gi * POOL_GROUP:(gi + 1) * POOL_GROUP, gi * POOL_GROUP:(gi + 1) * POOL_GROUP].set(pool_w[l, gi])

    common = dict(
        w_in=w_in_p, q_g=q_norm_g[l][None], kv_g=kv_norm_g[l][None], wqat=wqat, wqbt=wqbt, wvt=wvt, vbt=vbt,
        pool_w=pw.astype(BF16), pool_s=pool_scale[l][None],
        conv_dw=jnp.concatenate([conv_dw[l], jnp.zeros((1, CONV_WIDTH), F32)], axis=0),
        conv_b=conv_b[l][None], conv_ln_g=conv_ln_g[l][None], conv_ln_b=conv_ln_b[l][None],
        conv_pw=conv_pw[l].astype(BF16), w_out=w_out[l].astype(BF16),
        ln1_g=ln1_g[l][None], ln1_b=ln1_b[l][None],
        router_w=jnp.pad(router_w[l], ((0, 0), (0, LANES - N_EXPERTS))).astype(BF16),
        router_b=router_bias[l][:, None],
        sh_wg=sh_wg[l].astype(BF16), sh_wu=sh_wu[l].astype(BF16), sh_wd=sh_wd[l].astype(BF16),
        ln2_g=ln2_g[l][None], ln2_b=ln2_b[l][None],
    )
    return dict(common, wka=wka_lat, wkb=wkb_lat), dict(common, wka=wka_ctx, wkb=wkb_ctx)


def _rope_tables(n_lat, n_ctx):
    n_rows = n_lat // GRID_W
    inv = ROPE_BASE ** (-jnp.arange(0, ROPE_AXIS, 2, dtype=F32) / ROPE_AXIS)
    ang_r = jnp.arange(n_rows, dtype=F32)[:, None] * inv
    ang_c = jnp.arange(GRID_W, dtype=F32)[:, None] * inv
    by_row = lambda a: jnp.repeat(a, GRID_W, axis=0)
    by_col = lambda a: jnp.tile(a, (n_rows, 1))
    cos_r, sin_r, cos_c, sin_c = by_row(jnp.cos(ang_r)), by_row(jnp.sin(ang_r)), by_col(jnp.cos(ang_c)), by_col(jnp.sin(ang_c))
    cos = jnp.concatenate([cos_r, cos_r, cos_c, cos_c], axis=1)
    sin = jnp.concatenate([sin_r, sin_r, sin_c, sin_c], axis=1)
    ck = jnp.concatenate([jnp.ones((n_lat, HEAD_PAD - QK_ROPE), F32), cos], axis=1)
    sk = jnp.concatenate([jnp.zeros((n_lat, HEAD_PAD - QK_ROPE), F32), sin], axis=1)
    qs = ATTN_SCALE * LOG2E
    lat = ((ck * qs).T, (sk * qs).T, ck, sk)
    one = jnp.ones((n_ctx, HEAD_PAD), F32)
    zero = jnp.zeros((n_ctx, HEAD_PAD), F32)
    ctx = ((one * qs).T, zero.T, one, zero)
    return lat, ctx


def _sublayers(z, mod, l, ew, w, att, up, uc, tm, tm_moe, alpha):
    b, s, d = z.shape
    z1, h, hp = _mix(att, up, uc, z, mod, w, tm, alpha)
    out = _moe_sorted(h.reshape(b * s, d), hp.reshape(b * s, d // 2), l, ew, w, z1.reshape(b * s, d),
                      mod, tm_moe, s // tm_moe, alpha)
    return out.reshape(b, s, d)


def kernel(x, c, ctx, c_ctx, ada_w, ada_b, w_in, q_norm_g, w_uq, kv_norm_g, w_ukv, pool_w, pool_scale, conv_dw, conv_b, conv_ln_g, conv_ln_b, conv_pw, w_out, ln1_g, ln1_b, router_w, router_bias, exp_wg, exp_wu, exp_wd, sh_wg, sh_wu, sh_wd, ln2_g, ln2_b):
    b, s, d = x.shape
    n_ctx = ctx.shape[1]
    depth = ada_w.shape[0]
    alpha = (2 * depth) ** 0.25
    assert b + 1 <= 8 and s % GRID_W == 0

    tm_l = min(512, s)
    tm_c = min(256, n_ctx)
    tm_moe_l = min(512, s)
    tm_moe_c = min(128, n_ctx)
    bq = min(1024, s)

    cond = jnp.concatenate([c, c_ctx[None], jnp.zeros((8 - b - 1, d), F32)], axis=0)
    mods = _ada(cond, ada_w, ada_b)
    tab_l, tab_c = _rope_tables(s, n_ctx)
    ew = (exp_wg, exp_wu, exp_wd)

    zl, zc = x, ctx
    for l in range(depth):
        last = l == depth - 1
        w_l, w_c = _prep_layer(l, w_in, q_norm_g, w_uq, kv_norm_g, w_ukv, pool_w, pool_scale,
                               conv_dw, conv_b, conv_ln_g, conv_ln_b, conv_pw, w_out, ln1_g, ln1_b,
                               router_w, router_bias, exp_wg, exp_wu, exp_wd, sh_wg, sh_wu, sh_wd,
                               ln2_g, ln2_b)
        mod_l = mods[l, :b].reshape(b, 6, d)
        mod_c = jnp.broadcast_to(mods[l, b].reshape(1, 6, d), (b, 6, d))

        q_l, k_l, v_l, up_l, uc_l = _proj(zl, mod_l, w_l, tab_l, tm_l)
        q_c, k_c, v_c, up_c, uc_c = _proj(zc, mod_c, w_c, tab_c, tm_c)
        att_l = _attn(q_l, k_c, v_c, k_l, v_l, bq)
        zl = _sublayers(zl, mod_l, l, ew, w_l, att_l, up_l, uc_l, tm_l, tm_moe_l, alpha)
        if not last:
            att_c = _attn(q_c, k_c, v_c, None, None, tm_c)
            zc = _sublayers(zc, mod_c, l, ew, w_c, att_c, up_c, uc_c, tm_c, tm_moe_c, alpha)
    return zl
```
